```python
import jax, jax.numpy as jnp
from jax import lax
import numpy as np

D_MODEL = 1024
BATCH = 32
SEQ = 256
DEPTH = 2
DEC_BATCH = 4
DEC_SEQ = 4096
PAST_LEN = 256

GRID_W = 64
LN_EPS = 1e-6
DN_ALPHA = float((2 * DEPTH) ** 0.25)
DN_BETA = float((8 * DEPTH) ** -0.25)

N_BRANCH = 3
BRANCH_W = D_MODEL
GLA_HEADS = 4
GLA_DK = D_MODEL // (2 * GLA_HEADS)
GLA_DV = BRANCH_W // GLA_HEADS
GLA_LR = 16
GLA_TAU = 16.0
GLA_CHUNK = 32
CONV_W = BRANCH_W
CONV_K = 3
ATT_HD = 64
ATT_HEADS = BRANCH_W // ATT_HD
ATT_KV_HEADS = 4
ATT_GROUP = ATT_HEADS // ATT_KV_HEADS
WINDOW = 128
ATT_BLOCK = 128
ROPE_THETA = 10000.0
PEER_HEADS = 8
PEER_NKEYS = 128
PEER_EXPERTS = PEER_NKEYS * PEER_NKEYS
PEER_TOPK = 16
PEER_DQ = 256
PEER_BLOCK = 128

O_GQ = 0
O_GK = O_GQ + GLA_HEADS * GLA_DK
O_GV = O_GK + GLA_HEADS * GLA_DK
O_GG = O_GV + GLA_HEADS * GLA_DV
O_GA = O_GG + GLA_HEADS * GLA_DV
O_CH = O_GA + 2 * GLA_LR
O_CB = O_CH + CONV_W
O_CC = O_CB + CONV_W
O_AQ = O_CC + CONV_W
O_AK = O_AQ + ATT_HEADS * ATT_HD
O_AV = O_AK + ATT_KV_HEADS * ATT_HD
O_MG = O_AV + ATT_KV_HEADS * ATT_HD
N_IN = O_MG + N_BRANCH * D_MODEL

kernel_name = 'hybrid_gla_conv_swa_peer_dit_step'


def layer_norm(x, g, b):
    xf = x.astype(jnp.float32)
    mu = jnp.mean(xf, -1, keepdims=True)
    var = jnp.mean(jnp.square(xf - mu), -1, keepdims=True)
    y = (xf - mu) * lax.rsqrt(var + LN_EPS) * g.astype(jnp.float32) + b.astype(jnp.float32)
    return y.astype(x.dtype)


def ada_modulation(cond, w_mod, b_mod):
    m = jax.nn.silu(cond) @ w_mod + b_mod
    return jnp.split(m[:, None, :], 6, axis=-1)


def axial_rope_tables(T):
    rows = T // GRID_W
    row = jnp.repeat(jnp.arange(rows, dtype=jnp.float32), GRID_W)
    col = jnp.tile(jnp.arange(GRID_W, dtype=jnp.float32), rows)
    half = ATT_HD // 2
    inv = ROPE_THETA ** (-jnp.arange(0, half, 2, dtype=jnp.float32) / half)
    ang = jnp.concatenate([row[:, None] * inv, col[:, None] * inv], -1)
    return jnp.cos(ang), jnp.sin(ang)


def apply_rope(x, cos, sin):
    B, T, H, hd = x.shape
    xr = x.astype(jnp.float32).reshape(B, T, H, hd // 2, 2)
    x1, x2 = xr[..., 0], xr[..., 1]
    c = cos[None, :, None, :]
    s = sin[None, :, None, :]
    return jnp.stack([x1 * c - x2 * s, x1 * s + x2 * c], -1).reshape(B, T, H, hd).astype(x.dtype)


def gla_scan(q, k, v, log_a, s0):
    B, T, H, DK = q.shape
    DV = v.shape[-1]
    n = T // GLA_CHUNK

    def chunks(z):
        return z.astype(jnp.float32).reshape(B, n, GLA_CHUNK, H, z.shape[-1]).transpose(1, 0, 3, 2, 4)

    tri = jnp.tril(jnp.ones((GLA_CHUNK, GLA_CHUNK), dtype=bool))[:, :, None]

    def step(S, inp):
        qc, kc, vc, ac = inp
        b = jnp.cumsum(ac, axis=2)
        rel = jnp.where(tri, b[:, :, :, None, :] - b[:, :, None, :, :], -jnp.inf)
        att = jnp.einsum('bhtsd,bhsd->bhts', qc[:, :, :, None, :] * jnp.exp(rel), kc)
        o = jnp.einsum('bhts,bhsv->bhtv', att, vc) + jnp.einsum('bhtd,bhdv->bhtv', qc * jnp.exp(b), S)
        b_end = b[:, :, -1:, :]
        S = jnp.exp(b_end[:, :, 0, :, None]) * S + jnp.einsum('bhsd,bhsv->bhdv', kc * jnp.exp(b_end - b), vc)
        return S, o

    S, o = lax.scan(step, s0.astype(jnp.float32), (chunks(q), chunks(k), chunks(v), chunks(log_a)))
    return o.transpose(1, 0, 3, 2, 4).reshape(B, T, H, DV), S


def short_conv(h, gate_b, gate_c, w):
    z = gate_c * h
    zp = jnp.pad(z, ((0, 0), (1, 1), (0, 0)))
    y = w[0] * zp[:, :-2] + w[1] * zp[:, 1:-1] + w[2] * zp[:, 2:]
    return gate_b * y


def context_attention(q, k, v, sink):
    B, T, _, hd = q.shape
    qg = q.reshape(B, T, ATT_KV_HEADS, ATT_GROUP, hd)
    s = jnp.einsum('bqhgd,bkhd->bhgqk', qg, k).astype(jnp.float32) * (hd ** -0.5)
    sk = sink.astype(jnp.float32).reshape(ATT_KV_HEADS, ATT_GROUP)[None, :, :, None, None]
    m = jnp.maximum(jnp.max(s, -1, keepdims=True), sk)
    p = jnp.exp(s - m)
    p = p / (jnp.sum(p, -1, keepdims=True) + jnp.exp(sk - m))
    o = jnp.einsum('bhgqk,bkhd->bqhgd', p, v.astype(jnp.float32))
    return o.reshape(B, T, ATT_HEADS * hd).astype(q.dtype)


def latent_window_attention(q, k, v, kc, vc, sink):
    B, T, _, hd = q.shape
    nb = T // ATT_BLOCK
    qb = q.reshape(B, nb, ATT_BLOCK, ATT_KV_HEADS, ATT_GROUP, hd)

    def band(z):
        zp = jnp.pad(z, ((0, 0), (ATT_BLOCK, ATT_BLOCK), (0, 0), (0, 0)))
        zp = zp.reshape(B, nb + 2, ATT_BLOCK, ATT_KV_HEADS, hd)
        return jnp.concatenate([zp[:, :-2], zp[:, 1:-1], zp[:, 2:]], axis=2)

    kb, vb = band(k), band(v)
    scale = hd ** -0.5
    s_loc = jnp.einsum('bnqhgd,bnkhd->bnhgqk', qb, kb).astype(jnp.float32) * scale
    qpos = jnp.arange(nb)[:, None] * ATT_BLOCK + jnp.arange(ATT_BLOCK)[None, :]
    kpos = (jnp.arange(nb)[:, None] - 1) * ATT_BLOCK + jnp.arange(3 * ATT_BLOCK)[None, :]
    valid = ((jnp.abs(qpos[:, :, None] - kpos[:, None, :]) <= WINDOW)
             & (kpos[:, None, :] >= 0) & (kpos[:, None, :] < T))
    s_loc = jnp.where(valid[None, :, None, None], s_loc, -jnp.inf)
    s_ctx = jnp.einsum('bnqhgd,bchd->bnhgqc', qb, kc).astype(jnp.float32) * scale
    sk = sink.astype(jnp.float32).reshape(ATT_KV_HEADS, ATT_GROUP)[None, None, :, :, None, None]
    m = jnp.maximum(jnp.maximum(jnp.max(s_loc, -1, keepdims=True), jnp.max(s_ctx, -1, keepdims=True)), sk)
    p_loc = jnp.exp(s_loc - m)
    p_ctx = jnp.exp(s_ctx - m)
    den = jnp.sum(p_loc, -1, keepdims=True) + jnp.sum(p_ctx, -1, keepdims=True) + jnp.exp(sk - m)
    o = (jnp.einsum('bnhgqk,bnkhd->bnqhgd', p_loc / den, vb.astype(jnp.float32))
         + jnp.einsum('bnhgqc,bchd->bnqhgd', p_ctx / den, vc.astype(jnp.float32)))
    return o.reshape(B, T, ATT_HEADS * hd).astype(q.dtype)


def token_mixer(h, lp, ctx):
    B, T, _ = h.shape
    p = h @ lp['w_in']
    gq = p[..., O_GQ:O_GK].reshape(B, T, GLA_HEADS, GLA_DK) * (GLA_DK ** -0.5)
    gk = p[..., O_GK:O_GV].reshape(B, T, GLA_HEADS, GLA_DK)
    gv = p[..., O_GV:O_GG].reshape(B, T, GLA_HEADS, GLA_DV)
    gg = p[..., O_GG:O_GA]

    def log_decay(i):
        z = p[..., O_GA + i * GLA_LR:O_GA + (i + 1) * GLA_LR] @ lp['w_gla_a2'][i] + lp['b_gla_a'][i]
        return (jax.nn.log_sigmoid(z.astype(jnp.float32)) / GLA_TAU).reshape(B, T, GLA_HEADS, GLA_DK)

    if ctx is None:
        s0f = jnp.zeros((B, GLA_HEADS, GLA_DK, GLA_DV), jnp.float32)
        s0b = s0f
    else:
        s0f, s0b = ctx['s_f'], ctx['s_b']
    o_f, s_f = gla_scan(gq, gk, gv, log_decay(0), s0f)
    flip = lambda z: jnp.flip(z, axis=1)
    o_b, s_b = gla_scan(flip(gq), flip(gk), flip(gv), flip(log_decay(1)), s0b)
    o = o_f + flip(o_b)
    o = o * lax.rsqrt(jnp.mean(jnp.square(o), -1, keepdims=True) + LN_EPS) * lp['gla_norm_g'].astype(jnp.float32)
    y_a = (o.reshape(B, T, BRANCH_W) * jax.nn.silu(gg.astype(jnp.float32))).astype(h.dtype)
    y_b = short_conv(p[..., O_CH:O_CB], p[..., O_CB:O_CC], p[..., O_CC:O_AQ], lp['conv_w'])
    aq = p[..., O_AQ:O_AK].reshape(B, T, ATT_HEADS, ATT_HD)
    ak = p[..., O_AK:O_AV].reshape(B, T, ATT_KV_HEADS, ATT_HD)
    av = p[..., O_AV:O_MG].reshape(B, T, ATT_KV_HEADS, ATT_HD)
    if ctx is None:
        y_c = context_attention(aq, ak, av, lp['attn_sink'])
        new = (ak, av, s_f, s_b)
    else:
        aq = apply_rope(aq, ctx['cos'], ctx['sin'])
        ak = apply_rope(ak, ctx['cos'], ctx['sin'])
        y_c = latent_window_attention(aq, ak, av, ctx['k'], ctx['v'], lp['attn_sink'])
        new = None
    ys = jnp.stack([y_a, y_b, y_c], axis=2)
    gates = jax.nn.sigmoid(p[..., O_MG:].reshape(B, T, N_BRANCH, D_MODEL))
    merged = jnp.sum(gates * jnp.einsum('btnw,nwd->btnd', ys, lp['w_branch']), axis=2)
    return merged @ lp['w_out'], new


def peer_ffn(h, w_pq, keys, u_tab, v_tab):
    B, T, D = h.shape
    nt = B * T
    xf = h.reshape(nt, D)
    q = (xf @ w_pq).reshape(nt, PEER_HEADS, 2, PEER_DQ // 2)
    s = jnp.einsum('thpc,hpkc->thpk', q, keys).astype(jnp.float32)
    s1, i1 = lax.top_k(s[:, :, 0], PEER_TOPK)
    s2, i2 = lax.top_k(s[:, :, 1], PEER_TOPK)
    cand = (s1[..., :, None] + s2[..., None, :]).reshape(nt, PEER_HEADS, PEER_TOPK * PEER_TOPK)
    cidx = (i1[..., :, None] * PEER_NKEYS + i2[..., None, :]).reshape(nt, PEER_HEADS, PEER_TOPK * PEER_TOPK)
    top_s, top_pos = lax.top_k(cand, PEER_TOPK)
    eidx = jnp.take_along_axis(cidx, top_pos, axis=-1)
    gate = jax.nn.softmax(top_s, axis=-1).astype(h.dtype)
    nb = nt // PEER_BLOCK

    def block(args):
        xb, ib, gb = args
        act = jax.nn.gelu(jnp.einsum('td,thkd->thk', xb, u_tab[ib]), approximate=False)
        return jnp.einsum('thk,thkd->td', gb * act, v_tab[ib])

    out = lax.map(block, (xf.reshape(nb, PEER_BLOCK, D),
                          eidx.reshape(nb, PEER_BLOCK, PEER_HEADS, PEER_TOPK),
                          gate.reshape(nb, PEER_BLOCK, PEER_HEADS, PEER_TOPK)))
    return out.reshape(B, T, D)


def trunk_layer(x, lp, mod, ctx):
    sh1, sc1, g1, sh2, sc2, g2 = mod
    mix, new = token_mixer(x * (1 + sc1) + sh1, lp, ctx)
    x = layer_norm(DN_ALPHA * x + g1 * mix, lp['ln1_g'], lp['ln1_b'])
    ffn = peer_ffn(x * (1 + sc2) + sh2, lp['w_pq'], lp['peer_keys'], lp['peer_u'], lp['peer_v'])
    x = layer_norm(DN_ALPHA * x + g2 * ffn, lp['ln2_g'], lp['ln2_b'])
    return x, new


def setup_inputs(seed: int = 0) -> dict:
    key = jax.random.key(seed)
    ks = jax.random.split(key, 32)
    f32 = jnp.float32

    def nrm(k, shape, s):
        return jax.random.normal(k, shape, f32) * s

    D = D_MODEL
    return {
        'x_prompt': nrm(ks[0], (BATCH, SEQ, D), 1.0),
        'x_sample': nrm(ks[1], (DEC_BATCH, DEC_SEQ, D), 1.0),
        'cache_k': nrm(ks[2], (DEC_BATCH, DEPTH, PAST_LEN, ATT_KV_HEADS, ATT_HD), 1.0),
        'cache_v': nrm(ks[3], (DEC_BATCH, DEPTH, PAST_LEN, ATT_KV_HEADS, ATT_HD), 1.0),
        'state_gla': nrm(ks[4], (DEC_BATCH, DEPTH, 2, GLA_HEADS, GLA_DK, GLA_DV), 1.0),
        'c': nrm(ks[5], (DEC_BATCH, D), 1.0),
        'c_ctx': nrm(ks[6], (D,), 1.0),
        'ln_in_g': 1.0 + nrm(ks[7], (D,), 0.02),
        'ln_in_b': nrm(ks[8], (D,), 0.02),
        'w_mod': nrm(ks[9], (DEPTH, D, 6 * D), 0.5 * D ** -0.5),
        'b_mod': nrm(ks[10], (DEPTH, 6 * D), 0.02),
        'w_in': nrm(ks[11], (DEPTH, D, N_IN), D ** -0.5),
        'w_gla_a2': nrm(ks[12], (DEPTH, 2, GLA_LR, GLA_HEADS * GLA_DK), GLA_LR ** -0.5),
        'b_gla_a': nrm(ks[13], (DEPTH, 2, GLA_HEADS * GLA_DK), 0.1),
        'gla_norm_g': 1.0 + nrm(ks[14], (DEPTH, GLA_DV), 0.02),
        'conv_w': nrm(ks[15], (DEPTH, CONV_K, CONV_W), CONV_K ** -0.5),
        'attn_sink': nrm(ks[16], (DEPTH, ATT_HEADS), 0.5),
        'w_branch': nrm(ks[17], (DEPTH, N_BRANCH, BRANCH_W, D), BRANCH_W ** -0.5),
        'w_out': nrm(ks[18], (DEPTH, D, D), DN_BETA * D ** -0.5),
        'ln1_g': 1.0 + nrm(ks[19], (DEPTH, D), 0.02),
        'ln1_b': nrm(ks[20], (DEPTH, D), 0.02),
        'w_pq': nrm(ks[21], (DEPTH, D, PEER_HEADS * PEER_DQ), D ** -0.5),
        'peer_keys': nrm(ks[22], (DEPTH, PEER_HEADS, 2, PEER_NKEYS, PEER_DQ // 2), (PEER_DQ // 2) ** -0.5),
        'peer_u': nrm(ks[23], (DEPTH, PEER_EXPERTS, D), D ** -0.5),
        'peer_v': nrm(ks[24], (DEPTH, PEER_EXPERTS, D), DN_BETA * PEER_HEADS ** -0.5),
        'ln2_g': 1.0 + nrm(ks[25], (DEPTH, D), 0.02),
        'ln2_b': nrm(ks[26], (DEPTH, D), 0.02),
    }


def reference(x_prompt, x_sample, cache_k, cache_v, state_gla, c, c_ctx, ln_in_g, ln_in_b,
              w_mod, b_mod, w_in, w_gla_a2, b_gla_a, gla_norm_g, conv_w, attn_sink, w_branch,
              w_out, ln1_g, ln1_b, w_pq, peer_keys, peer_u, peer_v, ln2_g, ln2_b):
    cos, sin = axial_rope_tables(x_sample.shape[1])
    xp = layer_norm(x_prompt, ln_in_g, ln_in_b)
    xs = layer_norm(x_sample, ln_in_g, ln_in_b)
    ks, vs, ss = [], [], []
    for l in range(DEPTH):
        lp = {'w_in': w_in[l], 'w_gla_a2': w_gla_a2[l], 'b_gla_a': b_gla_a[l], 'gla_norm_g': gla_norm_g[l],
              'conv_w': conv_w[l], 'attn_sink': attn_sink[l], 'w_branch': w_branch[l], 'w_out': w_out[l],
              'ln1_g': ln1_g[l], 'ln1_b': ln1_b[l], 'w_pq': w_pq[l], 'peer_keys': peer_keys[l],
              'peer_u': peer_u[l], 'peer_v': peer_v[l], 'ln2_g': ln2_g[l], 'ln2_b': ln2_b[l]}
        mod_ctx = ada_modulation(c_ctx[None, :], w_mod[l], b_mod[l])
        xp, (k_l, v_l, sf_l, sb_l) = trunk_layer(xp, lp, mod_ctx, None)
        ks.append(k_l)
        vs.append(v_l)
        ss.append(jnp.stack([sf_l, sb_l], axis=1))
        mod_lat = ada_modulation(c, w_mod[l], b_mod[l])
        ctx = {'k': cache_k[:, l], 'v': cache_v[:, l], 's_f': state_gla[:, l, 0], 's_b': state_gla[:, l, 1],
               'cos': cos, 'sin': sin}
        xs, _ = trunk_layer(xs, lp, mod_lat, ctx)
    new_cache_k = jnp.stack(ks, axis=1)
    new_cache_v = jnp.stack(vs, axis=1)
    new_state_gla = jnp.stack(ss, axis=1)
    return (xp, xs, new_cache_k, new_cache_v, new_state_gla)
```

```python
import functools

import numpy as np
import jax
import jax.numpy as jnp
from jax import lax
from jax.experimental import pallas as pl
from jax.experimental.pallas import tpu as pltpu

F32 = jnp.float32
BF16 = jnp.bfloat16
HIGHEST = lax.Precision.HIGHEST

D_MODEL = 1024
DEPTH = 2
GRID_W = 64
LN_EPS = 1e-6
DN_ALPHA = float((2 * DEPTH) ** 0.25)

GLA_HEADS = 4
GLA_DK = 128
GLA_DV = 256
GLA_LR = 16
GLA_TAU = 16.0
CONV_K = 3
ATT_HD = 64
ATT_HEADS = 16
ATT_KV_HEADS = 4
ATT_GROUP = 4
ATT_BLOCK = 128
ROPE_THETA = 10000.0
PEER_HEADS = 8
PEER_NKEYS = 128
PEER_EXPERTS = PEER_NKEYS * PEER_NKEYS
PEER_TOPK = 16
PEER_DQ = 256

O_GQ, O_GK, O_GV, O_GG, O_GA = 0, 512, 1024, 2048, 3072
O_CH, O_CB, O_CC, O_AQ, O_AK, O_AV, O_MG = 3104, 4128, 5152, 6176, 7200, 7456, 7712
N_IN = 10784
P_GQ, P_GK, P_GV, P_GG = 0, 512, 1024, 2048
P_CH, P_CB, P_CC, P_AQ, P_MG = 3072, 4096, 5120, 6144, 7168
P_AK, P_AV, P_GA = 10240, 10496, 10752
N_P = 10880

LANE = 128
SUBLANE = 8
VMEM_LIMIT = 56 * 1024 * 1024

TM = 256
TM_IN = 512
TN_IN = 2176
GLA_BLK = 256
GLA_LEVELS = 8
PEER_TT = 128
PEER_GROUP = 8
PEER_SLOTS = 16


def _cparams(*sem):
    return pltpu.CompilerParams(dimension_semantics=sem, vmem_limit_bytes=VMEM_LIMIT)


def _layer_norm(x, g, b):
    mu = jnp.mean(x, axis=-1, keepdims=True)
    xc = x - mu
    var = jnp.mean(xc * xc, axis=-1, keepdims=True)
    return xc * lax.rsqrt(var + LN_EPS) * g + b


def _nt_dot(a, b):
    return lax.dot_general(a, b, (((1,), (1,)), ((), ())), preferred_element_type=F32)


def _mod_kernel(c_ref, w_ref, b_ref, o_ref):
    c = c_ref[...]
    s = c * jax.nn.sigmoid(c)
    o_ref[0] = jnp.dot(s, w_ref[0], precision=HIGHEST, preferred_element_type=F32) + b_ref[0]


def _modulation(cond8, w_mod, b_mod):
    tn = 1536
    return pl.pallas_call(
        _mod_kernel,
        grid=(DEPTH, 6 * D_MODEL // tn),
        in_specs=[pl.BlockSpec((8, D_MODEL), lambda l, j: (0, 0)),
                  pl.BlockSpec((1, D_MODEL, tn), lambda l, j: (l, 0, j)),
                  pl.BlockSpec((1, 1, tn), lambda l, j: (l, 0, j))],
        out_specs=pl.BlockSpec((1, 8, tn), lambda l, j: (l, 0, j)),
        out_shape=jax.ShapeDtypeStruct((DEPTH, 8, 6 * D_MODEL), F32),
        compiler_params=_cparams("arbitrary", "arbitrary"),
        name="ada_modulation",
    )(cond8, w_mod, b_mod.reshape(DEPTH, 1, 6 * D_MODEL))


def _mod_spec(piece, row_fn):
    return pl.BlockSpec((None, 1, D_MODEL), lambda *ids: (row_fn(*ids), 0, piece))


def _ln_kernel(x_ref, g_ref, b_ref, o_ref):
    o_ref[...] = _layer_norm(x_ref[...], g_ref[...], b_ref[...])


def _ln_in(x2d, g, b):
    nt = x2d.shape[0]
    return pl.pallas_call(
        _ln_kernel,
        grid=(nt // TM_IN,),
        in_specs=[pl.BlockSpec((TM_IN, D_MODEL), lambda i: (i, 0)),
                  pl.BlockSpec((1, D_MODEL), lambda i: (0, 0)),
                  pl.BlockSpec((1, D_MODEL), lambda i: (0, 0))],
        out_specs=pl.BlockSpec((TM_IN, D_MODEL), lambda i: (i, 0)),
        out_shape=jax.ShapeDtypeStruct((nt, D_MODEL), F32),
        compiler_params=_cparams("arbitrary"),
        name="ln_in",
    )(x2d, g.reshape(1, D_MODEL), b.reshape(1, D_MODEL))


def _inproj_kernel(x_ref, sh_ref, sc_ref, w_ref, o_ref):
    h = x_ref[...] * (1.0 + sc_ref[...]) + sh_ref[...]
    o_ref[...] = jnp.dot(h.astype(BF16), w_ref[...], preferred_element_type=F32)


def _inproj(x2d, mod3, w_in_p, seq_len, lat):
    nt = x2d.shape[0]
    tiles_per_seq = max(seq_len // TM_IN, 1)
    row = (lambda j, i: 1 + i // tiles_per_seq) if lat else (lambda j, i: 0)
    return pl.pallas_call(
        _inproj_kernel,
        grid=(N_P // TN_IN, nt // TM_IN),
        in_specs=[pl.BlockSpec((TM_IN, D_MODEL), lambda j, i: (i, 0)),
                  _mod_spec(0, row), _mod_spec(1, row),
                  pl.BlockSpec((D_MODEL, TN_IN), lambda j, i: (0, j))],
        out_specs=pl.BlockSpec((TM_IN, TN_IN), lambda j, i: (i, j)),
        out_shape=jax.ShapeDtypeStruct((nt, N_P), F32),
        compiler_params=_cparams("arbitrary", "arbitrary"),
        name="in_projection",
    )(x2d, mod3, mod3, w_in_p)


def _gla_tables(rev):
    n = GLA_BLK
    idx = np.arange(n)
    if rev:
        cum = (idx[None, :] >= idx[:, None]).astype(np.float32)
    else:
        cum = (idx[None, :] <= idx[:, None]).astype(np.float32)
    rows = [cum]
    for k in range(GLA_LEVELS):
        h = 1 << k
        base = idx & ~(2 * h - 1)
        piv = base + (h if rev else h - 1)
        rows.append(cum[piv])
    rows.append(cum[np.zeros(n, np.int64) if rev else np.full(n, n - 1)])
    mall = np.concatenate(rows, axis=0)
    t, s = idx[:, None], idx[None, :]
    x = t ^ s
    top = np.where(x > 0, np.floor(np.log2(np.maximum(x, 1))).astype(np.int64) + 1, 0)
    attend = (s >= t) if rev else (s <= t)
    lv = np.where(attend, top, -1).astype(np.int32)
    return jnp.asarray(mall, BF16), jnp.asarray(lv)


def _gla_kernel(rev, q_ref, k_ref, v_ref, ga_ref, wa_ref, ba_ref, mall_ref, lv_ref, s0_ref,
                o_ref, sout_ref, st_ref, t_ref):
    j = pl.program_id(1)
    n = GLA_BLK

    @pl.when(j == 0)
    def _():
        for h in range(GLA_HEADS):
            st_ref[h] = s0_ref[0, h].T

    z = jnp.dot(ga_ref[...], wa_ref[...], precision=HIGHEST, preferred_element_type=F32) + ba_ref[...]
    a = (jnp.minimum(z, 0.0) - jnp.log(1.0 + jnp.exp(-jnp.abs(z)))) * (1.0 / GLA_TAU)
    a_hi = a.astype(BF16)
    r1 = a - a_hi.astype(F32)
    a_mid = r1.astype(BF16)
    a_lo = (r1 - a_mid.astype(F32)).astype(BF16)
    mall = mall_ref[...]
    t_ref[...] = (jnp.dot(mall, a_hi, preferred_element_type=F32)
                  + jnp.dot(mall, a_mid, preferred_element_type=F32)
                  + jnp.dot(mall, a_lo, preferred_element_type=F32))

    lv = lv_ref[...]
    for h in range(GLA_HEADS):
        kc = slice(GLA_DK * h, GLA_DK * (h + 1))
        vc = slice(GLA_DV * h, GLA_DV * (h + 1))
        b = t_ref[0:n, kc]
        btot = t_ref[(GLA_LEVELS + 1) * n:(GLA_LEVELS + 2) * n, kc]
        q = q_ref[:, kc] * (GLA_DK ** -0.5)
        k = k_ref[:, kc]
        v = v_ref[:, vc]
        att = jnp.where(lv == 0, _nt_dot(q.astype(BF16), k.astype(BF16)), 0.0)
        for lev in range(GLA_LEVELS):
            pv = t_ref[(lev + 1) * n:(lev + 2) * n, kc]
            ql = (q * jnp.exp(jnp.minimum(b - pv, 0.0))).astype(BF16)
            kl = (k * jnp.exp(jnp.minimum(pv - b, 0.0))).astype(BF16)
            att = jnp.where(lv == lev + 1, _nt_dot(ql, kl), att)
        st = st_ref[h]
        qi = (q * jnp.exp(b)).astype(BF16)
        o = (jnp.dot(att.astype(BF16), v.astype(BF16), preferred_element_type=F32)
             + _nt_dot(qi, st.astype(BF16)))
        o_ref[:, vc] = o
        kh = (k * jnp.exp(btot - b)).astype(BF16)
        st_ref[h] = st * jnp.exp(btot[0:1, :]) + jnp.dot(v.T.astype(BF16), kh, preferred_element_type=F32)

    @pl.when(j == pl.num_programs(1) - 1)
    def _():
        for h in range(GLA_HEADS):
            sout_ref[0, h] = st_ref[h].T


def _gla(p2d, s0, wa_pad, ba, nseq, seq_len, rev):
    nt = p2d.shape[0]
    nblk = seq_len // GLA_BLK
    mall, lv = _gla_tables(rev)

    def rb(b, j):
        return b * nblk + (nblk - 1 - j if rev else j)

    return pl.pallas_call(
        functools.partial(_gla_kernel, rev),
        grid=(nseq, nblk),
        in_specs=[pl.BlockSpec((GLA_BLK, 512), lambda b, j: (rb(b, j), P_GQ // 512)),
                  pl.BlockSpec((GLA_BLK, 512), lambda b, j: (rb(b, j), P_GK // 512)),
                  pl.BlockSpec((GLA_BLK, 1024), lambda b, j: (rb(b, j), P_GV // 1024)),
                  pl.BlockSpec((GLA_BLK, LANE), lambda b, j: (rb(b, j), P_GA // LANE)),
                  pl.BlockSpec((LANE, 512), lambda b, j: (0, 0)),
                  pl.BlockSpec((1, 512), lambda b, j: (0, 0)),
                  pl.BlockSpec(mall.shape, lambda b, j: (0, 0)),
                  pl.BlockSpec(lv.shape, lambda b, j: (0, 0)),
                  pl.BlockSpec((1, GLA_HEADS, GLA_DK, GLA_DV), lambda b, j: (b, 0, 0, 0))],
        out_specs=[pl.BlockSpec((GLA_BLK, 1024), lambda b, j: (rb(b, j), 0)),
                   pl.BlockSpec((1, GLA_HEADS, GLA_DK, GLA_DV), lambda b, j: (b, 0, 0, 0))],
        out_shape=[jax.ShapeDtypeStruct((nt, 1024), F32),
                   jax.ShapeDtypeStruct((nseq, GLA_HEADS, GLA_DK, GLA_DV), F32)],
        scratch_shapes=[pltpu.VMEM((GLA_HEADS, GLA_DV, GLA_DK), F32),
                        pltpu.VMEM(((GLA_LEVELS + 2) * GLA_BLK, 512), F32)],
        compiler_params=_cparams("arbitrary", "arbitrary"),
        name="gla_bwd" if rev else "gla_fwd",
    )(p2d, p2d, p2d, p2d, wa_pad, ba, mall, lv, s0)


def _softmax_sink_heads(q_rows, k_all, v_all, bias, sink_col):
    s = _nt_dot(q_rows.astype(BF16), k_all.astype(BF16))
    if bias is not None:
        s = s + bias
    m = jnp.maximum(jnp.max(s, axis=-1, keepdims=True), sink_col)
    p = jnp.exp(s - m)
    den = jnp.sum(p, axis=-1, keepdims=True) + jnp.exp(sink_col - m)
    o = jnp.dot(p.astype(BF16), v_all.astype(BF16), preferred_element_type=F32)
    return o / den


def _ctx_attn_kernel(sink_ref, q_ref, k_ref, v_ref, o_ref):
    t = q_ref.shape[0]
    rows = lax.broadcasted_iota(jnp.int32, (ATT_GROUP * t, 1), 0)
    for g in range(ATT_KV_HEADS):
        kg = k_ref[:, ATT_HD * g:ATT_HD * (g + 1)]
        vg = v_ref[:, ATT_HD * g:ATT_HD * (g + 1)]
        qs, sink_col = [], jnp.zeros((ATT_GROUP * t, 1), F32)
        for i in range(ATT_GROUP):
            hh = g * ATT_GROUP + i
            qs.append(q_ref[:, ATT_HD * hh:ATT_HD * (hh + 1)] * (ATT_HD ** -0.5))
            sink_col = jnp.where(rows // t == i, sink_ref[hh], sink_col)
        o = _softmax_sink_heads(jnp.concatenate(qs, axis=0), kg, vg, None, sink_col)
        for i in range(ATT_GROUP):
            hh = g * ATT_GROUP + i
            o_ref[:, ATT_HD * hh:ATT_HD * (hh + 1)] = o[i * t:(i + 1) * t]


def _ctx_attention(p2d, sink, nseq, seq_len):
    nt = p2d.shape[0]
    return pl.pallas_call(
        _ctx_attn_kernel,
        grid=(nseq,),
        in_specs=[pl.BlockSpec(memory_space=pltpu.SMEM),
                  pl.BlockSpec((seq_len, 1024), lambda b: (b, P_AQ // 1024)),
                  pl.BlockSpec((seq_len, 256), lambda b: (b, P_AK // 256)),
                  pl.BlockSpec((seq_len, 256), lambda b: (b, P_AV // 256))],
        out_specs=pl.BlockSpec((seq_len, 1024), lambda b: (b, 0)),
        out_shape=jax.ShapeDtypeStruct((nt, 1024), F32),
        compiler_params=_cparams("arbitrary"),
        name="context_attention",
    )(sink, p2d, p2d, p2d)


def _rope(x, cos_t, sin_t):
    even = lax.broadcasted_iota(jnp.int32, x.shape, 1) % 2 == 0
    swapped = jnp.where(even, pltpu.roll(x, LANE - 1, 1), pltpu.roll(x, 1, 1))
    return x * cos_t + swapped * sin_t


def _lat_attn_kernel(sink_ref, q_ref, kp_ref, kc_ref, kn_ref, vp_ref, vc_ref, vn_ref,
                     ck_ref, cv_ref, cq_ref, sq_ref, cp_ref, sp_ref, cn_ref, sn_ref, o_ref):
    n = pl.program_id(1)
    nb = pl.num_programs(1)
    t = ATT_BLOCK
    kblocks = []
    for kref, c_ref, s_ref in ((kp_ref, cp_ref, sp_ref), (kc_ref, cq_ref, sq_ref), (kn_ref, cn_ref, sn_ref)):
        kblocks.append(jnp.concatenate(
            [_rope(kref[:, LANE * c:LANE * (c + 1)], c_ref[...], s_ref[...]) for c in range(2)], axis=1))
    k_loc = jnp.concatenate(kblocks, axis=0)
    v_loc = jnp.concatenate([vp_ref[...], vc_ref[...], vn_ref[...]], axis=0)
    k_all = jnp.concatenate([k_loc, ck_ref[...]], axis=0)
    v_all = jnp.concatenate([v_loc, cv_ref[...]], axis=0)
    tc = ck_ref.shape[0]
    qi = lax.broadcasted_iota(jnp.int32, (t, 3 * t + tc), 0)
    kj = lax.broadcasted_iota(jnp.int32, (t, 3 * t + tc), 1)
    ninf = jnp.float32(-jnp.inf)
    prev_bias = jnp.where(n > 0, jnp.float32(0.0), ninf)
    next_bias = jnp.where(n < nb - 1, jnp.float32(0.0), ninf)
    bias1 = jnp.where(kj < t, jnp.where(kj >= qi, prev_bias, ninf),
                      jnp.where(kj < 2 * t, 0.0,
                                jnp.where(kj < 3 * t, jnp.where(kj - 2 * t <= qi, next_bias, ninf), 0.0)))
    bias = jnp.concatenate([bias1] * ATT_GROUP, axis=0)
    rows = lax.broadcasted_iota(jnp.int32, (ATT_GROUP * t, 1), 0)
    qr = [_rope(q_ref[:, LANE * c:LANE * (c + 1)], cq_ref[...], sq_ref[...]) * (ATT_HD ** -0.5)
          for c in range(ATT_HEADS * ATT_HD // LANE)]
    for g in range(ATT_KV_HEADS):
        qs, sink_col = [], jnp.zeros((ATT_GROUP * t, 1), F32)
        for i in range(ATT_GROUP):
            hh = g * ATT_GROUP + i
            slab = qr[hh // 2]
            qs.append(slab[:, ATT_HD * (hh % 2):ATT_HD * (hh % 2 + 1)])
            sink_col = jnp.where(rows // t == i, sink_ref[hh], sink_col)
        kg = k_all[:, ATT_HD * g:ATT_HD * (g + 1)]
        vg = v_all[:, ATT_HD * g:ATT_HD * (g + 1)]
        o = _softmax_sink_heads(jnp.concatenate(qs, axis=0), kg, vg, bias, sink_col)
        for i in range(ATT_GROUP):
            hh = g * ATT_GROUP + i
            o_ref[:, ATT_HD * hh:ATT_HD * (hh + 1)] = o[i * t:(i + 1) * t]


def _lat_attention(p2d, sink, cache_k, cache_v, cos_t, sin_t, nseq, seq_len):
    nt = p2d.shape[0]
    nb = seq_len // ATT_BLOCK
    tc = cache_k.shape[1]
    cur = lambda b, n: b * nb + n
    prv = lambda b, n: b * nb + jnp.maximum(n - 1, 0)
    nxt = lambda b, n: b * nb + jnp.minimum(n + 1, nb - 1)
    kv = lambda f, col: pl.BlockSpec((ATT_BLOCK, 256), lambda b, n: (f(b, n), col))
    tab = lambda f: pl.BlockSpec((ATT_BLOCK, LANE), lambda b, n: (f(0, n), 0))
    return pl.pallas_call(
        _lat_attn_kernel,
        grid=(nseq, nb),
        in_specs=[pl.BlockSpec(memory_space=pltpu.SMEM),
                  pl.BlockSpec((ATT_BLOCK, 1024), lambda b, n: (cur(b, n), P_AQ // 1024)),
                  kv(prv, P_AK // 256), kv(cur, P_AK // 256), kv(nxt, P_AK // 256),
                  kv(prv, P_AV // 256), kv(cur, P_AV // 256), kv(nxt, P_AV // 256),
                  pl.BlockSpec((None, tc, 256), lambda b, n: (b, 0, 0)),
                  pl.BlockSpec((None, tc, 256), lambda b, n: (b, 0, 0)),
                  tab(cur), tab(cur), tab(prv), tab(prv), tab(nxt), tab(nxt)],
        out_specs=pl.BlockSpec((ATT_BLOCK, 1024), lambda b, n: (cur(b, n), 0)),
        out_shape=jax.ShapeDtypeStruct((nt, 1024), F32),
        compiler_params=_cparams("arbitrary", "arbitrary"),
        name="latent_window_attention",
    )(sink, p2d, p2d, p2d, p2d, p2d, p2d, p2d, cache_k, cache_v,
      cos_t, sin_t, cos_t, sin_t, cos_t, sin_t)


def _rope_tables(seq_len):
    rows = seq_len // GRID_W
    row = jnp.repeat(jnp.arange(rows, dtype=F32), GRID_W)
    col = jnp.tile(jnp.arange(GRID_W, dtype=F32), rows)
    half = ATT_HD // 2
    inv = ROPE_THETA ** (-jnp.arange(0, half, 2, dtype=F32) / half)
    ang = jnp.concatenate([row[:, None] * inv, col[:, None] * inv], -1)
    cos, sin = jnp.cos(ang), jnp.sin(ang)
    cos_t = jnp.tile(jnp.repeat(cos, 2, axis=1), (1, LANE // ATT_HD))
    sign = jnp.tile(jnp.asarray([-1.0, 1.0], F32), ATT_HD // 2)
    sin_t = jnp.tile(jnp.repeat(sin, 2, axis=1) * sign, (1, LANE // ATT_HD))
    return cos_t, sin_t


def _merge_kernel(tiles_per_seq, of_ref, ob_ref, gg_ref, ch_ref, cb_ref, cc_ref,
                  chp_ref, ccp_ref, chn_ref, ccn_ref, yc_ref, mga_ref, mgb_ref, mgc_ref, x_ref, g1_ref,
                  gn_ref, cw_ref, wb_ref, wo_ref, lg_ref, lb_ref, o_ref):
    i = pl.program_id(0)
    tm = x_ref.shape[0]
    gn = gn_ref[...]
    ya = []
    for h in range(GLA_HEADS):
        vc = slice(GLA_DV * h, GLA_DV * (h + 1))
        o = of_ref[:, vc] + ob_ref[:, vc]
        o = o * lax.rsqrt(jnp.mean(o * o, axis=-1, keepdims=True) + LN_EPS) * gn
        gg = gg_ref[:, vc]
        ya.append(o * (gg * jax.nn.sigmoid(gg)))
    ya = jnp.concatenate(ya, axis=1)
    z = cc_ref[...] * ch_ref[...]
    has_prev = (i % tiles_per_seq) != 0
    has_next = (i % tiles_per_seq) != tiles_per_seq - 1
    z_before = jnp.where(has_prev, ccp_ref[SUBLANE - 1:SUBLANE, :] * chp_ref[SUBLANE - 1:SUBLANE, :], 0.0)
    z_after = jnp.where(has_next, ccn_ref[0:1, :] * chn_ref[0:1, :], 0.0)
    r = lax.broadcasted_iota(jnp.int32, z.shape, 0)
    zp = jnp.where(r == 0, z_before, pltpu.roll(z, 1, 0))
    zn = jnp.where(r == tm - 1, z_after, pltpu.roll(z, tm - 1, 0))
    yb = cb_ref[...] * (cw_ref[0:1, :] * zp + cw_ref[1:2, :] * z + cw_ref[2:3, :] * zn)
    merged = jnp.zeros((tm, D_MODEL), F32)
    for bi, (y, mg_ref) in enumerate(((ya, mga_ref), (yb, mgb_ref), (yc_ref[...], mgc_ref))):
        proj = jnp.dot(y.astype(BF16), wb_ref[bi], preferred_element_type=F32)
        merged = merged + jax.nn.sigmoid(mg_ref[...]) * proj
    mix = jnp.dot(merged.astype(BF16), wo_ref[...], preferred_element_type=F32)
    o_ref[...] = _layer_norm(DN_ALPHA * x_ref[...] + g1_ref[...] * mix, lg_ref[...], lb_ref[...])


def _merge(x2d, p2d, o_f, o_b, y_c, mod3, gn, conv_w, wb, wo, ln_g, ln_b, seq_len, lat):
    nt = x2d.shape[0]
    tps = seq_len // TM
    row = (lambda i: 1 + i // tps) if lat else (lambda i: 0)
    col = lambda c: pl.BlockSpec((TM, 1024), lambda i: (i, c // 1024))
    halo_rows = TM // SUBLANE
    nhalo = nt // SUBLANE
    prev = lambda c: pl.BlockSpec((SUBLANE, 1024), lambda i: (jnp.maximum(i * halo_rows - 1, 0), c // 1024))
    nxt = lambda c: pl.BlockSpec((SUBLANE, 1024),
                                 lambda i: (jnp.minimum((i + 1) * halo_rows, nhalo - 1), c // 1024))
    full = lambda shape: pl.BlockSpec(shape, lambda i: (0,) * len(shape))
    return pl.pallas_call(
        functools.partial(_merge_kernel, tps),
        grid=(nt // TM,),
        in_specs=[pl.BlockSpec((TM, 1024), lambda i: (i, 0)), pl.BlockSpec((TM, 1024), lambda i: (i, 0)),
                  col(P_GG), col(P_CH), col(P_CB), col(P_CC),
                  prev(P_CH), prev(P_CC), nxt(P_CH), nxt(P_CC),
                  pl.BlockSpec((TM, 1024), lambda i: (i, 0)),
                  col(P_MG), col(P_MG + 1024), col(P_MG + 2048),
                  pl.BlockSpec((TM, 1024), lambda i: (i, 0)),
                  _mod_spec(2, row),
                  full((1, GLA_DV)), full((SUBLANE, 1024)), full((3, 1024, 1024)), full((1024, 1024)),
                  full((1, 1024)), full((1, 1024))],
        out_specs=pl.BlockSpec((TM, 1024), lambda i: (i, 0)),
        out_shape=jax.ShapeDtypeStruct((nt, 1024), F32),
        compiler_params=_cparams("arbitrary"),
        name="merge_ln1",
    )(o_f, o_b, p2d, p2d, p2d, p2d, p2d, p2d, p2d, p2d, y_c, p2d, p2d, p2d, x2d, mod3,
      gn, conv_w, wb, wo, ln_g, ln_b)


def _topk_rows(s, kk):
    n = s.shape[0]
    iota = lax.broadcasted_iota(jnp.int32, s.shape, 0).astype(F32)
    vals, idxs = [], []
    for _ in range(kk):
        m = jnp.max(s, axis=0, keepdims=True)
        idx = jnp.min(jnp.where(s == m, iota, float(n)), axis=0, keepdims=True)
        vals.append(m)
        idxs.append(idx)
        s = jnp.where(iota == idx, -jnp.inf, s)
    return jnp.concatenate(vals, axis=0), jnp.concatenate(idxs, axis=0)


def _peer_topk_kernel(x_ref, sh_ref, sc_ref, w_ref, keys_ref, eidx_ref, gate_ref):
    h2 = x_ref[...] * (1.0 + sc_ref[...]) + sh_ref[...]
    pq = jnp.dot(h2.astype(BF16), w_ref[...], preferred_element_type=F32)
    half = PEER_DQ // 2
    tops = []
    for p in range(2):
        s = _nt_dot(keys_ref[p], pq[:, half * p:half * (p + 1)].astype(BF16))
        tops.append(_topk_rows(s, PEER_TOPK))
    (s1, i1), (s2, i2) = tops
    tm = s1.shape[1]
    kk = PEER_TOPK
    cand = (s1[:, None, :] + s2[None, :, :]).reshape(kk * kk, tm)
    cidx = (i1[:, None, :] * float(PEER_NKEYS) + i2[None, :, :]).reshape(kk * kk, tm)
    pos = lax.broadcasted_iota(jnp.int32, cand.shape, 0).astype(F32)
    top_s, top_e = [], []
    for _ in range(kk):
        m = jnp.max(cand, axis=0, keepdims=True)
        at = jnp.min(jnp.where(cand == m, pos, float(kk * kk)), axis=0, keepdims=True)
        hit = pos == at
        top_s.append(m)
        top_e.append(jnp.sum(jnp.where(hit, cidx, 0.0), axis=0, keepdims=True))
        cand = jnp.where(hit, -jnp.inf, cand)
    top_s = jnp.concatenate(top_s, axis=0)
    e = jnp.exp(top_s - top_s[0:1])
    gate_ref[...] = e / jnp.sum(e, axis=0, keepdims=True)
    eidx_ref[...] = jnp.concatenate(top_e, axis=0).astype(jnp.int32)


def _peer_topk(x2d, mod3, w_pq, keys, seq_len, lat):
    nt = x2d.shape[0]
    tps = seq_len // TM
    row = (lambda i, h: 1 + i // tps) if lat else (lambda i, h: 0)
    return pl.pallas_call(
        _peer_topk_kernel,
        grid=(nt // TM, PEER_HEADS),
        in_specs=[pl.BlockSpec((TM, 1024), lambda i, h: (i, 0)),
                  _mod_spec(3, row), _mod_spec(4, row),
                  pl.BlockSpec((1024, PEER_DQ), lambda i, h: (0, h)),
                  pl.BlockSpec((None, 2, PEER_NKEYS, PEER_DQ // 2), lambda i, h: (h, 0, 0, 0))],
        out_specs=[pl.BlockSpec((None, PEER_TOPK, TM), lambda i, h: (h, 0, i)),
                   pl.BlockSpec((None, PEER_TOPK, TM), lambda i, h: (h, 0, i))],
        out_shape=[jax.ShapeDtypeStruct((PEER_HEADS, PEER_TOPK, nt), jnp.int32),
                   jax.ShapeDtypeStruct((PEER_HEADS, PEER_TOPK, nt), F32)],
        compiler_params=_cparams("arbitrary", "arbitrary"),
        name="peer_topk",
    )(x2d, mod3, mod3, w_pq, keys)


def _peer_expert_kernel(eidx_ref, x_ref, sh_ref, sc_ref, g2_ref, gate_ref, lg_ref, lb_ref, uv_hbm,
                        o_ref, ring, sems, ffn_ref):
    npair = PEER_HEADS * PEER_TOPK
    ngroup = PEER_TT // PEER_GROUP

    def slot_copy(slot):
        return pltpu.make_async_copy(uv_hbm.at[pl.ds(0, npair), :], ring.at[slot], sems.at[slot])

    def issue_group(g):
        base = (g % 2) * PEER_GROUP

        def per_token(jj, carry):
            tok = g * PEER_GROUP + jj
            for r in range(npair):
                e = eidx_ref[tok, r]
                pltpu.make_async_copy(uv_hbm.at[pl.ds(e, 1), :],
                                      ring.at[base + jj, pl.ds(r, 1), :],
                                      sems.at[base + jj]).start()
            return carry

        lax.fori_loop(0, PEER_GROUP, per_token, 0)

    issue_group(0)
    issue_group(1)
    rowid = lax.broadcasted_iota(jnp.int32, (PEER_GROUP, 1), 0)

    def group(g, carry):
        base = (g % 2) * PEER_GROUP
        r0 = pl.multiple_of(g * PEER_GROUP, PEER_GROUP)
        h8 = (x_ref[pl.ds(r0, PEER_GROUP), :] * (1.0 + sc_ref[...]) + sh_ref[...]).astype(BF16)
        sc8 = jnp.zeros((PEER_GROUP, npair), F32)
        for jj in range(PEER_GROUP):
            slot_copy(base + jj).wait()
            u = ring[base + jj, :, 0:D_MODEL].astype(BF16)
            sc8 = jnp.where(rowid == jj, _nt_dot(h8, u), sc8)
        act = 0.5 * sc8 * (1.0 + lax.erf(sc8 * (2.0 ** -0.5)))
        w8 = (gate_ref[pl.ds(r0, PEER_GROUP), :] * act).astype(BF16)
        out8 = jnp.zeros((PEER_GROUP, D_MODEL), F32)
        for jj in range(PEER_GROUP):
            v = ring[base + jj, :, D_MODEL:2 * D_MODEL].astype(BF16)
            out8 = jnp.where(rowid == jj, jnp.dot(w8, v, preferred_element_type=F32), out8)
        ffn_ref[pl.ds(r0, PEER_GROUP), :] = out8

        @pl.when(g + 2 < ngroup)
        def _():
            issue_group(g + 2)

        return carry

    lax.fori_loop(0, ngroup, group, 0)
    o_ref[...] = _layer_norm(DN_ALPHA * x_ref[...] + g2_ref[...] * ffn_ref[...], lg_ref[...], lb_ref[...])


def _peer_experts(x2d, eidx, gate, mod3, uv, ln_g, ln_b, seq_len, lat):
    nt = x2d.shape[0]
    tps = seq_len // PEER_TT
    row = (lambda i: 1 + i // tps) if lat else (lambda i: 0)
    npair = PEER_HEADS * PEER_TOPK
    full = lambda shape: pl.BlockSpec(shape, lambda i: (0,) * len(shape))
    return pl.pallas_call(
        _peer_expert_kernel,
        grid=(nt // PEER_TT,),
        in_specs=[pl.BlockSpec((PEER_TT, npair), lambda i: (i, 0), memory_space=pltpu.SMEM),
                  pl.BlockSpec((PEER_TT, 1024), lambda i: (i, 0)),
                  _mod_spec(3, row), _mod_spec(4, row), _mod_spec(5, row),
                  pl.BlockSpec((PEER_TT, npair), lambda i: (i, 0)),
                  full((1, 1024)), full((1, 1024)),
                  pl.BlockSpec(memory_space=pl.ANY)],
        out_specs=pl.BlockSpec((PEER_TT, 1024), lambda i: (i, 0)),
        out_shape=jax.ShapeDtypeStruct((nt, 1024), F32),
        scratch_shapes=[pltpu.VMEM((PEER_SLOTS, npair, 2 * D_MODEL), F32),
                        pltpu.SemaphoreType.DMA((PEER_SLOTS,)),
                        pltpu.VMEM((PEER_TT, D_MODEL), F32)],
        compiler_params=_cparams("arbitrary"),
        name="peer_experts_ln2",
    )(eidx, x2d, mod3, mod3, mod3, gate, ln_g, ln_b, uv)


def _layer(x2d, mod3, lw, nseq, seq_len, lat, s0_f, s0_b, ctx):
    p2d = _inproj(x2d, mod3, lw["w_in"], seq_len, lat)
    o_f, s_f = _gla(p2d, s0_f, lw["wa_f"], lw["ba_f"], nseq, seq_len, rev=False)
    o_b, s_b = _gla(p2d, s0_b, lw["wa_b"], lw["ba_b"], nseq, seq_len, rev=True)
    if lat:
        y_c = _lat_attention(p2d, lw["sink"], ctx["k"], ctx["v"], ctx["cos"], ctx["sin"], nseq, seq_len)
    else:
        y_c = _ctx_attention(p2d, lw["sink"], nseq, seq_len)
    x1 = _merge(x2d, p2d, o_f, o_b, y_c, mod3, lw["gn"], lw["conv_w"], lw["wb"], lw["wo"],
                lw["ln1_g"], lw["ln1_b"], seq_len, lat)
    eidx_t, gate_t = _peer_topk(x1, mod3, lw["w_pq"], lw["keys"], seq_len, lat)
    nt = x2d.shape[0]
    eidx = eidx_t.reshape(PEER_HEADS * PEER_TOPK, nt).T
    gate = gate_t.reshape(PEER_HEADS * PEER_TOPK, nt).T
    x2 = _peer_experts(x1, eidx, gate, mod3, lw["uv"], lw["ln2_g"], lw["ln2_b"], seq_len, lat)
    return x2, p2d, s_f, s_b


def _layer_weights(l, w_in, w_gla_a2, b_gla_a, gla_norm_g, conv_w, attn_sink, w_branch, w_out,
                   ln1_g, ln1_b, w_pq, peer_keys, peer_u, peer_v, ln2_g, ln2_b):
    w = w_in[l]
    w_in_p = jnp.concatenate(
        [w[:, O_GQ:O_GA], w[:, O_CH:O_AK], w[:, O_MG:N_IN], w[:, O_AK:O_MG], w[:, O_GA:O_CH],
         jnp.zeros((D_MODEL, N_P - N_IN), F32)], axis=1).astype(BF16)

    def wa_pad(d):
        return jnp.zeros((LANE, 512), F32).at[GLA_LR * d:GLA_LR * (d + 1)].set(w_gla_a2[l, d])

    return {
        "w_in": w_in_p,
        "wa_f": wa_pad(0), "wa_b": wa_pad(1),
        "ba_f": b_gla_a[l, 0].reshape(1, 512), "ba_b": b_gla_a[l, 1].reshape(1, 512),
        "gn": gla_norm_g[l].reshape(1, GLA_DV),
        "conv_w": jnp.zeros((SUBLANE, 1024), F32).at[0:CONV_K].set(conv_w[l]),
        "sink": attn_sink[l],
        "wb": w_branch[l].astype(BF16), "wo": w_out[l].astype(BF16),
        "ln1_g": ln1_g[l].reshape(1, 1024), "ln1_b": ln1_b[l].reshape(1, 1024),
        "w_pq": w_pq[l].astype(BF16),
        "keys": peer_keys[l].astype(BF16),
        "uv": jnp.concatenate([peer_u[l], peer_v[l]], axis=1),
        "ln2_g": ln2_g[l].reshape(1, 1024), "ln2_b": ln2_b[l].reshape(1, 1024),
    }


def kernel(x_prompt, x_sample, cache_k, cache_v, state_gla, c, c_ctx, ln_in_g, ln_in_b, w_mod, b_mod, w_in, w_gla_a2, b_gla_a, gla_norm_g, conv_w, attn_sink, w_branch, w_out, ln1_g, ln1_b, w_pq, peer_keys, peer_u, peer_v, ln2_g, ln2_b):
    batch, seq, _ = x_prompt.shape
    dec_batch, dec_seq, _ = x_sample.shape
    past = cache_k.shape[2]
    assert dec_batch + 1 <= 8 and seq % TM == 0 and dec_seq % TM_IN == 0 and seq % GLA_BLK == 0

    cond8 = jnp.zeros((8, D_MODEL), F32).at[0].set(c_ctx).at[1:1 + dec_batch].set(c)
    mod = _modulation(cond8, w_mod, b_mod)
    cos_t, sin_t = _rope_tables(dec_seq)

    xp = _ln_in(x_prompt.reshape(batch * seq, D_MODEL), ln_in_g, ln_in_b)
    xs = _ln_in(x_sample.reshape(dec_batch * dec_seq, D_MODEL), ln_in_g, ln_in_b)
    zeros_state = jnp.zeros((batch, GLA_HEADS, GLA_DK, GLA_DV), F32)
    ks, vs, ss = [], [], []
    for l in range(DEPTH):
        lw = _layer_weights(l, w_in, w_gla_a2, b_gla_a, gla_norm_g, conv_w, attn_sink, w_branch, w_out,
                            ln1_g, ln1_b, w_pq, peer_keys, peer_u, peer_v, ln2_g, ln2_b)
        mod3 = mod[l].reshape(8, 1, 6 * D_MODEL)
        xp, p_ctx, s_f, s_b = _layer(xp, mod3, lw, batch, seq, False, zeros_state, zeros_state, None)
        ks.append(p_ctx[:, P_AK:P_AK + 256].reshape(batch, seq, ATT_KV_HEADS, ATT_HD))
        vs.append(p_ctx[:, P_AV:P_AV + 256].reshape(batch, seq, ATT_KV_HEADS, ATT_HD))
        ss.append(jnp.stack([s_f, s_b], axis=1))
        ctx = {"k": cache_k[:, l].reshape(dec_batch, past, ATT_KV_HEADS * ATT_HD),
               "v": cache_v[:, l].reshape(dec_batch, past, ATT_KV_HEADS * ATT_HD),
               "cos": cos_t, "sin": sin_t}
        xs, _, _, _ = _layer(xs, mod3, lw, dec_batch, dec_seq, True,
                             state_gla[:, l, 0], state_gla[:, l, 1], ctx)
    return (xp.reshape(batch, seq, D_MODEL), xs.reshape(dec_batch, dec_seq, D_MODEL),
            jnp.stack(ks, axis=1), jnp.stack(vs, axis=1), jnp.stack(ss, axis=1))
```

```python
import functools

import numpy as np
import jax
import jax.numpy as jnp
from jax import lax
from jax.experimental import pallas as pl
from jax.experimental.pallas import tpu as pltpu

F32 = jnp.float32
BF16 = jnp.bfloat16
HIGHEST = lax.Precision.HIGHEST

D_MODEL = 1024
DEPTH = 2
GRID_W = 64
LN_EPS = 1e-6
DN_ALPHA = float((2 * DEPTH) ** 0.25)

GLA_HEADS = 4
GLA_DK = 128
GLA_DV = 256
GLA_LR = 16
GLA_TAU = 16.0
CONV_K = 3
ATT_HD = 64
ATT_HEADS = 16
ATT_KV_HEADS = 4
ATT_GROUP = 4
ATT_BLOCK = 128
ROPE_THETA = 10000.0
PEER_HEADS = 8
PEER_NKEYS = 128
PEER_EXPERTS = PEER_NKEYS * PEER_NKEYS
PEER_TOPK = 16
PEER_DQ = 256

O_GQ, O_GK, O_GV, O_GG, O_GA = 0, 512, 1024, 2048, 3072
O_CH, O_CB, O_CC, O_AQ, O_AK, O_AV, O_MG = 3104, 4128, 5152, 6176, 7200, 7456, 7712
N_IN = 10784
P_GQ, P_GK, P_GV, P_GG = 0, 512, 1024, 2048
P_CH, P_CB, P_CC, P_AQ, P_MG = 3072, 4096, 5120, 6144, 7168
P_AK, P_AV, P_GA = 10240, 10496, 10752
N_P = 10880

LANE = 128
SUBLANE = 8
VMEM_LIMIT = 56 * 1024 * 1024

TM = 256
TM_IN = 512
TN_IN = 2176
GLA_BLK = 256
GLA_LEVELS = 8
PEER_TT = 128
PEER_GROUP = 8
PEER_SLOTS = 16


def _cparams(*sem):
    return pltpu.CompilerParams(dimension_semantics=sem, vmem_limit_bytes=VMEM_LIMIT)


def _layer_norm(x, g, b):
    mu = jnp.mean(x, axis=-1, keepdims=True)
    xc = x - mu
    var = jnp.mean(xc * xc, axis=-1, keepdims=True)
    return xc * lax.rsqrt(var + LN_EPS) * g + b


def _nt_dot(a, b):
    return lax.dot_general(a, b, (((1,), (1,)), ((), ())), preferred_element_type=F32)


def _mod_kernel(c_ref, w_ref, b_ref, o_ref):
    c = c_ref[...]
    s = c * jax.nn.sigmoid(c)
    o_ref[0] = jnp.dot(s, w_ref[0], precision=HIGHEST, preferred_element_type=F32) + b_ref[0]


def _modulation(cond8, w_mod, b_mod):
    tn = 1536
    return pl.pallas_call(
        _mod_kernel,
        grid=(DEPTH, 6 * D_MODEL // tn),
        in_specs=[pl.BlockSpec((8, D_MODEL), lambda l, j: (0, 0)),
                  pl.BlockSpec((1, D_MODEL, tn), lambda l, j: (l, 0, j)),
                  pl.BlockSpec((1, 1, tn), lambda l, j: (l, 0, j))],
        out_specs=pl.BlockSpec((1, 8, tn), lambda l, j: (l, 0, j)),
        out_shape=jax.ShapeDtypeStruct((DEPTH, 8, 6 * D_MODEL), F32),
        compiler_params=_cparams("arbitrary", "arbitrary"),
        name="ada_modulation",
    )(cond8, w_mod, b_mod.reshape(DEPTH, 1, 6 * D_MODEL))


def _mod_spec(piece, row_fn):
    return pl.BlockSpec((None, 1, D_MODEL), lambda *ids: (row_fn(*ids), 0, piece))


def _ln_kernel(x_ref, g_ref, b_ref, o_ref):
    o_ref[...] = _layer_norm(x_ref[...], g_ref[...], b_ref[...])


def _ln_in(x2d, g, b):
    nt = x2d.shape[0]
    return pl.pallas_call(
        _ln_kernel,
        grid=(nt // TM_IN,),
        in_specs=[pl.BlockSpec((TM_IN, D_MODEL), lambda i: (i, 0)),
                  pl.BlockSpec((1, D_MODEL), lambda i: (0, 0)),
                  pl.BlockSpec((1, D_MODEL), lambda i: (0, 0))],
        out_specs=pl.BlockSpec((TM_IN, D_MODEL), lambda i: (i, 0)),
        out_shape=jax.ShapeDtypeStruct((nt, D_MODEL), F32),
        compiler_params=_cparams("arbitrary"),
        name="ln_in",
    )(x2d, g.reshape(1, D_MODEL), b.reshape(1, D_MODEL))


def _inproj_kernel(x_ref, sh_ref, sc_ref, w_ref, o_ref):
    h = x_ref[...] * (1.0 + sc_ref[...]) + sh_ref[...]
    o_ref[...] = jnp.dot(h.astype(BF16), w_ref[...], preferred_element_type=F32)


def _inproj(x2d, mod3, w_in_p, seq_len, lat):
    nt = x2d.shape[0]
    tiles_per_seq = max(seq_len // TM_IN, 1)
    row = (lambda j, i: 1 + i // tiles_per_seq) if lat else (lambda j, i: 0)
    return pl.pallas_call(
        _inproj_kernel,
        grid=(N_P // TN_IN, nt // TM_IN),
        in_specs=[pl.BlockSpec((TM_IN, D_MODEL), lambda j, i: (i, 0)),
                  _mod_spec(0, row), _mod_spec(1, row),
                  pl.BlockSpec((D_MODEL, TN_IN), lambda j, i: (0, j))],
        out_specs=pl.BlockSpec((TM_IN, TN_IN), lambda j, i: (i, j)),
        out_shape=jax.ShapeDtypeStruct((nt, N_P), F32),
        compiler_params=_cparams("arbitrary", "arbitrary"),
        name="in_projection",
    )(x2d, mod3, mod3, w_in_p)


def _gla_tables(rev):
    n = GLA_BLK
    idx = np.arange(n)
    if rev:
        cum = (idx[None, :] >= idx[:, None]).astype(np.float32)
    else:
        cum = (idx[None, :] <= idx[:, None]).astype(np.float32)
    rows = [cum]
    for k in range(GLA_LEVELS):
        h = 1 << k
        base = idx & ~(2 * h - 1)
        piv = base + (h if rev else h - 1)
        rows.append(cum[piv])
    rows.append(cum[np.zeros(n, np.int64) if rev else np.full(n, n - 1)])
    mall = np.concatenate(rows, axis=0)
    t, s = idx[:, None], idx[None, :]
    x = t ^ s
    top = np.where(x > 0, np.floor(np.log2(np.maximum(x, 1))).astype(np.int64) + 1, 0)
    attend = (s >= t) if rev else (s <= t)
    lv = np.where(attend, top, -1).astype(np.int32)
    return jnp.asarray(mall, BF16), jnp.asarray(lv)


def _gla_kernel(rev, q_ref, k_ref, v_ref, ga_ref, wa_ref, ba_ref, mall_ref, lv_ref, s0_ref,
                o_ref, sout_ref, st_ref, t_ref):
    j = pl.program_id(1)
    n = GLA_BLK

    @pl.when(j == 0)
    def _():
        for h in range(GLA_HEADS):
            st_ref[h] = s0_ref[0, h].T

    z = jnp.dot(ga_ref[...], wa_ref[...], precision=HIGHEST, preferred_element_type=F32) + ba_ref[...]
    a = (jnp.minimum(z, 0.0) - jnp.log(1.0 + jnp.exp(-jnp.abs(z)))) * (1.0 / GLA_TAU)
    a_hi = a.astype(BF16)
    r1 = a - a_hi.astype(F32)
    a_mid = r1.astype(BF16)
    a_lo = (r1 - a_mid.astype(F32)).astype(BF16)
    mall = mall_ref[...]
    t_ref[...] = (jnp.dot(mall, a_hi, preferred_element_type=F32)
                  + jnp.dot(mall, a_mid, preferred_element_type=F32)
                  + jnp.dot(mall, a_lo, preferred_element_type=F32))

    lv = lv_ref[...]
    for h in range(GLA_HEADS):
        kc = slice(GLA_DK * h, GLA_DK * (h + 1))
        vc = slice(GLA_DV * h, GLA_DV * (h + 1))
        b = t_ref[0:n, kc]
        btot = t_ref[(GLA_LEVELS + 1) * n:(GLA_LEVELS + 2) * n, kc]
        q = q_ref[:, kc] * (GLA_DK ** -0.5)
        k = k_ref[:, kc]
        v = v_ref[:, vc]
        att = jnp.where(lv == 0, _nt_dot(q.astype(BF16), k.astype(BF16)), 0.0)
        for lev in range(GLA_LEVELS):
            pv = t_ref[(lev + 1) * n:(lev + 2) * n, kc]
            ql = (q * jnp.exp(jnp.minimum(b - pv, 0.0))).astype(BF16)
            kl = (k * jnp.exp(jnp.minimum(pv - b, 0.0))).astype(BF16)
            att = jnp.where(lv == lev + 1, _nt_dot(ql, kl), att)
        st = st_ref[h]
        qi = (q * jnp.exp(b)).astype(BF16)
        o = (jnp.dot(att.astype(BF16), v.astype(BF16), preferred_element_type=F32)
             + _nt_dot(qi, st.astype(BF16)))
        o_ref[:, vc] = o
        kh = (k * jnp.exp(btot - b)).astype(BF16)
        st_ref[h] = st * jnp.exp(btot[0:1, :]) + jnp.dot(v.T.astype(BF16), kh, preferred_element_type=F32)

    @pl.when(j == pl.num_programs(1) - 1)
    def _():
        for h in range(GLA_HEADS):
            sout_ref[0, h] = st_ref[h].T


def _gla(p2d, s0, wa_pad, ba, nseq, seq_len, rev):
    nt = p2d.shape[0]
    nblk = seq_len // GLA_BLK
    mall, lv = _gla_tables(rev)

    def rb(b, j):
        return b * nblk + (nblk - 1 - j if rev else j)

    return pl.pallas_call(
        functools.partial(_gla_kernel, rev),
        grid=(nseq, nblk),
        in_specs=[pl.BlockSpec((GLA_BLK, 512), lambda b, j: (rb(b, j), P_GQ // 512)),
                  pl.BlockSpec((GLA_BLK, 512), lambda b, j: (rb(b, j), P_GK // 512)),
                  pl.BlockSpec((GLA_BLK, 1024), lambda b, j: (rb(b, j), P_GV // 1024)),
                  pl.BlockSpec((GLA_BLK, LANE), lambda b, j: (rb(b, j), P_GA // LANE)),
                  pl.BlockSpec((LANE, 512), lambda b, j: (0, 0)),
                  pl.BlockSpec((1, 512), lambda b, j: (0, 0)),
                  pl.BlockSpec(mall.shape, lambda b, j: (0, 0)),
                  pl.BlockSpec(lv.shape, lambda b, j: (0, 0)),
                  pl.BlockSpec((1, GLA_HEADS, GLA_DK, GLA_DV), lambda b, j: (b, 0, 0, 0))],
        out_specs=[pl.BlockSpec((GLA_BLK, 1024), lambda b, j: (rb(b, j), 0)),
                   pl.BlockSpec((1, GLA_HEADS, GLA_DK, GLA_DV), lambda b, j: (b, 0, 0, 0))],
        out_shape=[jax.ShapeDtypeStruct((nt, 1024), F32),
                   jax.ShapeDtypeStruct((nseq, GLA_HEADS, GLA_DK, GLA_DV), F32)],
        scratch_shapes=[pltpu.VMEM((GLA_HEADS, GLA_DV, GLA_DK), F32),
                        pltpu.VMEM(((GLA_LEVELS + 2) * GLA_BLK, 512), F32)],
        compiler_params=_cparams("arbitrary", "arbitrary"),
        name="gla_bwd" if rev else "gla_fwd",
    )(p2d, p2d, p2d, p2d, wa_pad, ba, mall, lv, s0)


def _softmax_sink_heads(q_rows, k_all, v_all, bias, sink_col):
    s = _nt_dot(q_rows.astype(BF16), k_all.astype(BF16))
    if bias is not None:
        s = s + bias
    m = jnp.maximum(jnp.max(s, axis=-1, keepdims=True), sink_col)
    p = jnp.exp(s - m)
    den = jnp.sum(p, axis=-1, keepdims=True) + jnp.exp(sink_col - m)
    o = jnp.dot(p.astype(BF16), v_all.astype(BF16), preferred_element_type=F32)
    return o / den


def _ctx_attn_kernel(sink_ref, q_ref, k_ref, v_ref, o_ref):
    t = q_ref.shape[0]
    rows = lax.broadcasted_iota(jnp.int32, (ATT_GROUP * t, 1), 0)
    for g in range(ATT_KV_HEADS):
        kg = k_ref[:, ATT_HD * g:ATT_HD * (g + 1)]
        vg = v_ref[:, ATT_HD * g:ATT_HD * (g + 1)]
        qs, sink_col = [], jnp.zeros((ATT_GROUP * t, 1), F32)
        for i in range(ATT_GROUP):
            hh = g * ATT_GROUP + i
            qs.append(q_ref[:, ATT_HD * hh:ATT_HD * (hh + 1)] * (ATT_HD ** -0.5))
            sink_col = jnp.where(rows // t == i, sink_ref[hh], sink_col)
        o = _softmax_sink_heads(jnp.concatenate(qs, axis=0), kg, vg, None, sink_col)
        for i in range(ATT_GROUP):
            hh = g * ATT_GROUP + i
            o_ref[:, ATT_HD * hh:ATT_HD * (hh + 1)] = o[i * t:(i + 1) * t]


def _ctx_attention(p2d, sink, nseq, seq_len):
    nt = p2d.shape[0]
    return pl.pallas_call(
        _ctx_attn_kernel,
        grid=(nseq,),
        in_specs=[pl.BlockSpec(memory_space=pltpu.SMEM),
                  pl.BlockSpec((seq_len, 1024), lambda b: (b, P_AQ // 1024)),
                  pl.BlockSpec((seq_len, 256), lambda b: (b, P_AK // 256)),
                  pl.BlockSpec((seq_len, 256), lambda b: (b, P_AV // 256))],
        out_specs=pl.BlockSpec((seq_len, 1024), lambda b: (b, 0)),
        out_shape=jax.ShapeDtypeStruct((nt, 1024), F32),
        compiler_params=_cparams("arbitrary"),
        name="context_attention",
    )(sink, p2d, p2d, p2d)


def _rope(x, cos_t, sin_t):
    even = lax.broadcasted_iota(jnp.int32, x.shape, 1) % 2 == 0
    swapped = jnp.where(even, pltpu.roll(x, LANE - 1, 1), pltpu.roll(x, 1, 1))
    return x * cos_t + swapped * sin_t


def _lat_attn_kernel(sink_ref, q_ref, kp_ref, kc_ref, kn_ref, vp_ref, vc_ref, vn_ref,
                     ck_ref, cv_ref, cq_ref, sq_ref, cp_ref, sp_ref, cn_ref, sn_ref, o_ref):
    n = pl.program_id(1)
    nb = pl.num_programs(1)
    t = ATT_BLOCK
    kblocks = []
    for kref, c_ref, s_ref in ((kp_ref, cp_ref, sp_ref), (kc_ref, cq_ref, sq_ref), (kn_ref, cn_ref, sn_ref)):
        kblocks.append(jnp.concatenate(
            [_rope(kref[:, LANE * c:LANE * (c + 1)], c_ref[...], s_ref[...]) for c in range(2)], axis=1))
    k_loc = jnp.concatenate(kblocks, axis=0)
    v_loc = jnp.concatenate([vp_ref[...], vc_ref[...], vn_ref[...]], axis=0)
    k_all = jnp.concatenate([k_loc, ck_ref[...]], axis=0)
    v_all = jnp.concatenate([v_loc, cv_ref[...]], axis=0)
    tc = ck_ref.shape[0]
    qi = lax.broadcasted_iota(jnp.int32, (t, 3 * t + tc), 0)
    kj = lax.broadcasted_iota(jnp.int32, (t, 3 * t + tc), 1)
    ninf = jnp.float32(-jnp.inf)
    prev_bias = jnp.where(n > 0, jnp.float32(0.0), ninf)
    next_bias = jnp.where(n < nb - 1, jnp.float32(0.0), ninf)
    bias1 = jnp.where(kj < t, jnp.where(kj >= qi, prev_bias, ninf),
                      jnp.where(kj < 2 * t, 0.0,
                                jnp.where(kj < 3 * t, jnp.where(kj - 2 * t <= qi, next_bias, ninf), 0.0)))
    bias = jnp.concatenate([bias1] * ATT_GROUP, axis=0)
    rows = lax.broadcasted_iota(jnp.int32, (ATT_GROUP * t, 1), 0)
    qr = [_rope(q_ref[:, LANE * c:LANE * (c + 1)], cq_ref[...], sq_ref[...]) * (ATT_HD ** -0.5)
          for c in range(ATT_HEADS * ATT_HD // LANE)]
    for g in range(ATT_KV_HEADS):
        qs, sink_col = [], jnp.zeros((ATT_GROUP * t, 1), F32)
        for i in range(ATT_GROUP):
            hh = g * ATT_GROUP + i
            slab = qr[hh // 2]
            qs.append(slab[:, ATT_HD * (hh % 2):ATT_HD * (hh % 2 + 1)])
            sink_col = jnp.where(rows // t == i, sink_ref[hh], sink_col)
        kg = k_all[:, ATT_HD * g:ATT_HD * (g + 1)]
        vg = v_all[:, ATT_HD * g:ATT_HD * (g + 1)]
        o = _softmax_sink_heads(jnp.concatenate(qs, axis=0), kg, vg, bias, sink_col)
        for i in range(ATT_GROUP):
            hh = g * ATT_GROUP + i
            o_ref[:, ATT_HD * hh:ATT_HD * (hh + 1)] = o[i * t:(i + 1) * t]


def _lat_attention(p2d, sink, cache_k, cache_v, cos_t, sin_t, nseq, seq_len):
    nt = p2d.shape[0]
    nb = seq_len // ATT_BLOCK
    tc = cache_k.shape[1]
    cur = lambda b, n: b * nb + n
    prv = lambda b, n: b * nb + jnp.maximum(n - 1, 0)
    nxt = lambda b, n: b * nb + jnp.minimum(n + 1, nb - 1)
    kv = lambda f, col: pl.BlockSpec((ATT_BLOCK, 256), lambda b, n: (f(b, n), col))
    tab = lambda f: pl.BlockSpec((ATT_BLOCK, LANE), lambda b, n: (f(0, n), 0))
    return pl.pallas_call(
        _lat_attn_kernel,
        grid=(nseq, nb),
        in_specs=[pl.BlockSpec(memory_space=pltpu.SMEM),
                  pl.BlockSpec((ATT_BLOCK, 1024), lambda b, n: (cur(b, n), P_AQ // 1024)),
                  kv(prv, P_AK // 256), kv(cur, P_AK // 256), kv(nxt, P_AK // 256),
                  kv(prv, P_AV // 256), kv(cur, P_AV // 256), kv(nxt, P_AV // 256),
                  pl.BlockSpec((None, tc, 256), lambda b, n: (b, 0, 0)),
                  pl.BlockSpec((None, tc, 256), lambda b, n: (b, 0, 0)),
                  tab(cur), tab(cur), tab(prv), tab(prv), tab(nxt), tab(nxt)],
        out_specs=pl.BlockSpec((ATT_BLOCK, 1024), lambda b, n: (cur(b, n), 0)),
        out_shape=jax.ShapeDtypeStruct((nt, 1024), F32),
        compiler_params=_cparams("arbitrary", "arbitrary"),
        name="latent_window_attention",
    )(sink, p2d, p2d, p2d, p2d, p2d, p2d, p2d, cache_k, cache_v,
      cos_t, sin_t, cos_t, sin_t, cos_t, sin_t)


def _rope_tables(seq_len):
    rows = seq_len // GRID_W
    row = jnp.repeat(jnp.arange(rows, dtype=F32), GRID_W)
    col = jnp.tile(jnp.arange(GRID_W, dtype=F32), rows)
    half = ATT_HD // 2
    inv = ROPE_THETA ** (-jnp.arange(0, half, 2, dtype=F32) / half)
    ang = jnp.concatenate([row[:, None] * inv, col[:, None] * inv], -1)
    cos, sin = jnp.cos(ang), jnp.sin(ang)
    cos_t = jnp.tile(jnp.repeat(cos, 2, axis=1), (1, LANE // ATT_HD))
    sign = jnp.tile(jnp.asarray([-1.0, 1.0], F32), ATT_HD // 2)
    sin_t = jnp.tile(jnp.repeat(sin, 2, axis=1) * sign, (1, LANE // ATT_HD))
    return cos_t, sin_t


def _merge_kernel(tiles_per_seq, of_ref, ob_ref, gg_ref, ch_ref, cb_ref, cc_ref,
                  chp_ref, ccp_ref, chn_ref, ccn_ref, yc_ref, mga_ref, mgb_ref, mgc_ref, x_ref, g1_ref,
                  gn_ref, cw_ref, wb_ref, wo_ref, lg_ref, lb_ref, o_ref):
    i = pl.program_id(0)
    tm = x_ref.shape[0]
    gn = gn_ref[...]
    ya = []
    for h in range(GLA_HEADS):
        vc = slice(GLA_DV * h, GLA_DV * (h + 1))
        o = of_ref[:, vc] + ob_ref[:, vc]
        o = o * lax.rsqrt(jnp.mean(o * o, axis=-1, keepdims=True) + LN_EPS) * gn
        gg = gg_ref[:, vc]
        ya.append(o * (gg * jax.nn.sigmoid(gg)))
    ya = jnp.concatenate(ya, axis=1)
    z = cc_ref[...] * ch_ref[...]
    has_prev = (i % tiles_per_seq) != 0
    has_next = (i % tiles_per_seq) != tiles_per_seq - 1
    z_before = jnp.where(has_prev, ccp_ref[SUBLANE - 1:SUBLANE, :] * chp_ref[SUBLANE - 1:SUBLANE, :], 0.0)
    z_after = jnp.where(has_next, ccn_ref[0:1, :] * chn_ref[0:1, :], 0.0)
    r = lax.broadcasted_iota(jnp.int32, z.shape, 0)
    zp = jnp.where(r == 0, z_before, pltpu.roll(z, 1, 0))
    zn = jnp.where(r == tm - 1, z_after, pltpu.roll(z, tm - 1, 0))
    yb = cb_ref[...] * (cw_ref[0:1, :] * zp + cw_ref[1:2, :] * z + cw_ref[2:3, :] * zn)
    merged = jnp.zeros((tm, D_MODEL), F32)
    for bi, (y, mg_ref) in enumerate(((ya, mga_ref), (yb, mgb_ref), (yc_ref[...], mgc_ref))):
        proj = jnp.dot(y.astype(BF16), wb_ref[bi], preferred_element_type=F32)
        merged = merged + jax.nn.sigmoid(mg_ref[...]) * proj
    mix = jnp.dot(merged.astype(BF16), wo_ref[...], preferred_element_type=F32)
    o_ref[...] = _layer_norm(DN_ALPHA * x_ref[...] + g1_ref[...] * mix, lg_ref[...], lb_ref[...])


def _merge(x2d, p2d, o_f, o_b, y_c, mod3, gn, conv_w, wb, wo, ln_g, ln_b, seq_len, lat):
    nt = x2d.shape[0]
    tps = seq_len // TM
    row = (lambda i: 1 + i // tps) if lat else (lambda i: 0)
    col = lambda c: pl.BlockSpec((TM, 1024), lambda i: (i, c // 1024))
    halo_rows = TM // SUBLANE
    nhalo = nt // SUBLANE
    prev = lambda c: pl.BlockSpec((SUBLANE, 1024), lambda i: (jnp.maximum(i * halo_rows - 1, 0), c // 1024))
    nxt = lambda c: pl.BlockSpec((SUBLANE, 1024),
                                 lambda i: (jnp.minimum((i + 1) * halo_rows, nhalo - 1), c // 1024))
    full = lambda shape: pl.BlockSpec(shape, lambda i: (0,) * len(shape))
    return pl.pallas_call(
        functools.partial(_merge_kernel, tps),
        grid=(nt // TM,),
        in_specs=[pl.BlockSpec((TM, 1024), lambda i: (i, 0)), pl.BlockSpec((TM, 1024), lambda i: (i, 0)),
                  col(P_GG), col(P_CH), col(P_CB), col(P_CC),
                  prev(P_CH), prev(P_CC), nxt(P_CH), nxt(P_CC),
                  pl.BlockSpec((TM, 1024), lambda i: (i, 0)),
                  col(P_MG), col(P_MG + 1024), col(P_MG + 2048),
                  pl.BlockSpec((TM, 1024), lambda i: (i, 0)),
                  _mod_spec(2, row),
                  full((1, GLA_DV)), full((SUBLANE, 1024)), full((3, 1024, 1024)), full((1024, 1024)),
                  full((1, 1024)), full((1, 1024))],
        out_specs=pl.BlockSpec((TM, 1024), lambda i: (i, 0)),
        out_shape=jax.ShapeDtypeStruct((nt, 1024), F32),
        compiler_params=_cparams("arbitrary"),
        name="merge_ln1",
    )(o_f, o_b, p2d, p2d, p2d, p2d, p2d, p2d, p2d, p2d, y_c, p2d, p2d, p2d, x2d, mod3,
      gn, conv_w, wb, wo, ln_g, ln_b)


def _topk_rows(s, kk):
    n = s.shape[0]
    iota = lax.broadcasted_iota(jnp.int32, s.shape, 0).astype(F32)
    vals, idxs = [], []
    for _ in range(kk):
        m = jnp.max(s, axis=0, keepdims=True)
        idx = jnp.min(jnp.where(s == m, iota, float(n)), axis=0, keepdims=True)
        vals.append(m)
        idxs.append(idx)
        s = jnp.where(iota == idx, -jnp.inf, s)
    return jnp.concatenate(vals, axis=0), jnp.concatenate(idxs, axis=0)


def _staircase():
    return [(i, PEER_TOPK // (i + 1)) for i in range(PEER_TOPK)]


def _peer_topk_kernel(x_ref, sh_ref, sc_ref, w_ref, keys_ref, pos_ref, eidx_ref, gate_ref):
    h2 = x_ref[...] * (1.0 + sc_ref[...]) + sh_ref[...]
    pq = jnp.dot(h2.astype(BF16), w_ref[...], preferred_element_type=F32)
    half = PEER_DQ // 2
    tops = []
    for p in range(2):
        s = _nt_dot(keys_ref[p], pq[:, half * p:half * (p + 1)].astype(BF16))
        tops.append(_topk_rows(s, PEER_TOPK))
    (s1, i1), (s2, i2) = tops
    tm = s1.shape[1]
    kk = PEER_TOPK
    pos = pos_ref[...]
    npad = pos.shape[0] - sum(nj for _, nj in _staircase())
    cand = jnp.concatenate([s1[i:i + 1] + s2[0:nj] for i, nj in _staircase()]
                           + [jnp.full((npad, tm), -jnp.inf, F32)], axis=0)
    cidx = jnp.concatenate([i1[i:i + 1] * float(PEER_NKEYS) + i2[0:nj] for i, nj in _staircase()]
                           + [jnp.zeros((npad, tm), F32)], axis=0)
    top_s, top_e = [], []
    for _ in range(kk):
        m = jnp.max(cand, axis=0, keepdims=True)
        at = jnp.min(jnp.where(cand == m, pos, float(2 * kk * kk)), axis=0, keepdims=True)
        hit = pos == at
        top_s.append(m)
        top_e.append(jnp.sum(jnp.where(hit, cidx, 0.0), axis=0, keepdims=True))
        cand = jnp.where(hit, -jnp.inf, cand)
    top_s = jnp.concatenate(top_s, axis=0)
    e = jnp.exp(top_s - top_s[0:1])
    gate_ref[...] = e / jnp.sum(e, axis=0, keepdims=True)
    eidx_ref[...] = jnp.concatenate(top_e, axis=0).astype(jnp.int32)


def _peer_topk(x2d, mod3, w_pq, keys, seq_len, lat):
    nt = x2d.shape[0]
    tps = seq_len // TM
    row = (lambda i, h: 1 + i // tps) if lat else (lambda i, h: 0)
    flat = [i * PEER_TOPK + j for i, nj in _staircase() for j in range(nj)]
    nrows = -(-len(flat) // SUBLANE) * SUBLANE
    flat = flat + [PEER_TOPK * PEER_TOPK] * (nrows - len(flat))
    pos = jnp.asarray(np.repeat(np.asarray(flat, np.float32)[:, None], TM, axis=1))
    return pl.pallas_call(
        _peer_topk_kernel,
        grid=(nt // TM, PEER_HEADS),
        in_specs=[pl.BlockSpec((TM, 1024), lambda i, h: (i, 0)),
                  _mod_spec(3, row), _mod_spec(4, row),
                  pl.BlockSpec((1024, PEER_DQ), lambda i, h: (0, h)),
                  pl.BlockSpec((None, 2, PEER_NKEYS, PEER_DQ // 2), lambda i, h: (h, 0, 0, 0)),
                  pl.BlockSpec((nrows, TM), lambda i, h: (0, 0))],
        out_specs=[pl.BlockSpec((None, PEER_TOPK, TM), lambda i, h: (h, 0, i)),
                   pl.BlockSpec((None, PEER_TOPK, TM), lambda i, h: (h, 0, i))],
        out_shape=[jax.ShapeDtypeStruct((PEER_HEADS, PEER_TOPK, nt), jnp.int32),
                   jax.ShapeDtypeStruct((PEER_HEADS, PEER_TOPK, nt), F32)],
        compiler_params=_cparams("arbitrary", "arbitrary"),
        name="peer_topk",
    )(x2d, mod3, mod3, w_pq, keys, pos)


def _peer_expert_kernel(eidx_ref, x_ref, sh_ref, sc_ref, g2_ref, gate_ref, lg_ref, lb_ref, col_ref, exp_ref,
                        uv_hbm, o_ref, ring, sems, ffn_ref):
    npair = PEER_HEADS * PEER_TOPK
    ngroup = PEER_TT // PEER_GROUP
    nrow = npair * SUBLANE

    def slot_copy(slot):
        return pltpu.make_async_copy(uv_hbm.at[pl.ds(0, npair)], ring.at[slot], sems.at[slot])

    def issue_group(g):
        base = (g % 2) * PEER_GROUP

        def per_token(jj, carry):
            tok = g * PEER_GROUP + jj
            for r in range(npair):
                pltpu.make_async_copy(uv_hbm.at[eidx_ref[tok, r]], ring.at[base + jj, r],
                                      sems.at[base + jj]).start()
            return carry

        lax.fori_loop(0, PEER_GROUP, per_token, 0)

    issue_group(0)
    issue_group(1)
    rowid = lax.broadcasted_iota(jnp.int32, (PEER_GROUP, 1), 0)
    own = (lax.broadcasted_iota(jnp.int32, (SUBLANE, nrow), 1) % SUBLANE
           == lax.broadcasted_iota(jnp.int32, (SUBLANE, nrow), 0))
    sc, sh = sc_ref[...], sh_ref[...]

    def group(g, carry):
        base = (g % 2) * PEER_GROUP
        r0 = pl.multiple_of(g * PEER_GROUP, PEER_GROUP)
        for jj in range(PEER_GROUP):
            slot_copy(base + jj).wait()
        z8 = jnp.zeros((PEER_GROUP, nrow), F32)
        for jj in range(PEER_GROUP):
            h = (x_ref[r0 + jj] * (1.0 + sc) + sh).astype(BF16)
            u = ring[base + jj, :, 0].reshape(nrow, LANE).astype(BF16)
            y = jnp.where(own, _nt_dot(h, u), 0.0)
            z8 = jnp.where(rowid == jj, jnp.sum(y, axis=0, keepdims=True), z8)
        z_hi = z8.astype(BF16)
        z_lo = (z8 - z_hi.astype(F32)).astype(BF16)
        s8 = (jnp.dot(z_hi, col_ref[...], preferred_element_type=F32)
              + jnp.dot(z_lo, col_ref[...], preferred_element_type=F32))
        act = 0.5 * s8 * (1.0 + lax.erf(s8 * (2.0 ** -0.5)))
        w8 = (gate_ref[pl.ds(r0, PEER_GROUP), :] * act).astype(BF16)
        wexp = jnp.dot(w8, exp_ref[...], preferred_element_type=F32)
        for jj in range(PEER_GROUP):
            wsel = jnp.where(own, wexp[jj:jj + 1, :], 0.0).astype(BF16)
            v = ring[base + jj, :, 1].reshape(nrow, LANE).astype(BF16)
            ffn_ref[r0 + jj] = jnp.dot(wsel, v, preferred_element_type=F32)

        @pl.when(g + 2 < ngroup)
        def _():
            issue_group(g + 2)

        return carry

    lax.fori_loop(0, ngroup, group, 0)
    y = DN_ALPHA * x_ref[...] + g2_ref[...] * ffn_ref[...]
    mu = jnp.sum(jnp.sum(y, axis=2, keepdims=True), axis=1, keepdims=True) * (1.0 / D_MODEL)
    yc = y - mu
    var = jnp.sum(jnp.sum(yc * yc, axis=2, keepdims=True), axis=1, keepdims=True) * (1.0 / D_MODEL)
    o_ref[...] = yc * lax.rsqrt(var + LN_EPS) * lg_ref[...] + lb_ref[...]


def _peer_experts(x2d, eidx, gate, mod, uv, ln_g, ln_b, seq_len, lat):
    nt = x2d.shape[0]
    tps = seq_len // PEER_TT
    row = (lambda i: 1 + i // tps) if lat else (lambda i: 0)
    npair = PEER_HEADS * PEER_TOPK
    nrow = npair * SUBLANE
    slab = (SUBLANE, LANE)
    mod4 = mod.reshape(8, 6, *slab)
    mod_spec = lambda piece: pl.BlockSpec((None, None) + slab, lambda i: (row(i), piece, 0, 0))
    full = lambda shape: pl.BlockSpec(shape, lambda i: (0,) * len(shape))
    collapse = jnp.asarray(np.arange(nrow)[:, None] // SUBLANE == np.arange(npair)[None, :], BF16)
    out = pl.pallas_call(
        _peer_expert_kernel,
        grid=(nt // PEER_TT,),
        in_specs=[pl.BlockSpec((PEER_TT, npair), lambda i: (i, 0), memory_space=pltpu.SMEM),
                  pl.BlockSpec((PEER_TT,) + slab, lambda i: (i, 0, 0)),
                  mod_spec(3), mod_spec(4), mod_spec(5),
                  pl.BlockSpec((PEER_TT, npair), lambda i: (i, 0)),
                  full(slab), full(slab), full((nrow, npair)), full((npair, nrow)),
                  pl.BlockSpec(memory_space=pl.ANY)],
        out_specs=pl.BlockSpec((PEER_TT,) + slab, lambda i: (i, 0, 0)),
        out_shape=jax.ShapeDtypeStruct((nt,) + slab, F32),
        scratch_shapes=[pltpu.VMEM((PEER_SLOTS, npair, 2) + slab, F32),
                        pltpu.SemaphoreType.DMA((PEER_SLOTS,)),
                        pltpu.VMEM((PEER_TT,) + slab, F32)],
        compiler_params=_cparams("arbitrary"),
        name="peer_experts_ln2",
    )(eidx, x2d.reshape((nt,) + slab), mod4, mod4, mod4, gate, ln_g.reshape(slab), ln_b.reshape(slab),
      collapse, collapse.T, uv)
    return out.reshape(nt, D_MODEL)


def _layer(x2d, mod3, lw, nseq, seq_len, lat, s0_f, s0_b, ctx):
    p2d = _inproj(x2d, mod3, lw["w_in"], seq_len, lat)
    o_f, s_f = _gla(p2d, s0_f, lw["wa_f"], lw["ba_f"], nseq, seq_len, rev=False)
    o_b, s_b = _gla(p2d, s0_b, lw["wa_b"], lw["ba_b"], nseq, seq_len, rev=True)
    if lat:
        y_c = _lat_attention(p2d, lw["sink"], ctx["k"], ctx["v"], ctx["cos"], ctx["sin"], nseq, seq_len)
    else:
        y_c = _ctx_attention(p2d, lw["sink"], nseq, seq_len)
    x1 = _merge(x2d, p2d, o_f, o_b, y_c, mod3, lw["gn"], lw["conv_w"], lw["wb"], lw["wo"],
                lw["ln1_g"], lw["ln1_b"], seq_len, lat)
    eidx_t, gate_t = _peer_topk(x1, mod3, lw["w_pq"], lw["keys"], seq_len, lat)
    nt = x2d.shape[0]
    eidx = eidx_t.reshape(PEER_HEADS * PEER_TOPK, nt).T
    gate = gate_t.reshape(PEER_HEADS * PEER_TOPK, nt).T
    x2 = _peer_experts(x1, eidx, gate, mod3, lw["uv"], lw["ln2_g"], lw["ln2_b"], seq_len, lat)
    return x2, p2d, s_f, s_b


def _layer_weights(l, w_in, w_gla_a2, b_gla_a, gla_norm_g, conv_w, attn_sink, w_branch, w_out,
                   ln1_g, ln1_b, w_pq, peer_keys, peer_u, peer_v, ln2_g, ln2_b):
    w = w_in[l]
    w_in_p = jnp.concatenate(
        [w[:, O_GQ:O_GA], w[:, O_CH:O_AK], w[:, O_MG:N_IN], w[:, O_AK:O_MG], w[:, O_GA:O_CH],
         jnp.zeros((D_MODEL, N_P - N_IN), F32)], axis=1).astype(BF16)

    def wa_pad(d):
        return jnp.zeros((LANE, 512), F32).at[GLA_LR * d:GLA_LR * (d + 1)].set(w_gla_a2[l, d])

    return {
        "w_in": w_in_p,
        "wa_f": wa_pad(0), "wa_b": wa_pad(1),
        "ba_f": b_gla_a[l, 0].reshape(1, 512), "ba_b": b_gla_a[l, 1].reshape(1, 512),
        "gn": gla_norm_g[l].reshape(1, GLA_DV),
        "conv_w": jnp.zeros((SUBLANE, 1024), F32).at[0:CONV_K].set(conv_w[l]),
        "sink": attn_sink[l],
        "wb": w_branch[l].astype(BF16), "wo": w_out[l].astype(BF16),
        "ln1_g": ln1_g[l].reshape(1, 1024), "ln1_b": ln1_b[l].reshape(1, 1024),
        "w_pq": w_pq[l].astype(BF16),
        "keys": peer_keys[l].astype(BF16),
        "uv": jnp.stack([peer_u[l].reshape(PEER_EXPERTS, SUBLANE, LANE),
                         peer_v[l].reshape(PEER_EXPERTS, SUBLANE, LANE)], axis=1),
        "ln2_g": ln2_g[l].reshape(1, 1024), "ln2_b": ln2_b[l].reshape(1, 1024),
    }


def kernel(x_prompt, x_sample, cache_k, cache_v, state_gla, c, c_ctx, ln_in_g, ln_in_b, w_mod, b_mod, w_in, w_gla_a2, b_gla_a, gla_norm_g, conv_w, attn_sink, w_branch, w_out, ln1_g, ln1_b, w_pq, peer_keys, peer_u, peer_v, ln2_g, ln2_b):
    batch, seq, _ = x_prompt.shape
    dec_batch, dec_seq, _ = x_sample.shape
    past = cache_k.shape[2]
    assert dec_batch + 1 <= 8 and seq % TM == 0 and dec_seq % TM_IN == 0 and seq % GLA_BLK == 0

    cond8 = jnp.zeros((8, D_MODEL), F32).at[0].set(c_ctx).at[1:1 + dec_batch].set(c)
    mod = _modulation(cond8, w_mod, b_mod)
    cos_t, sin_t = _rope_tables(dec_seq)

    xp = _ln_in(x_prompt.reshape(batch * seq, D_MODEL), ln_in_g, ln_in_b)
    xs = _ln_in(x_sample.reshape(dec_batch * dec_seq, D_MODEL), ln_in_g, ln_in_b)
    zeros_state = jnp.zeros((batch, GLA_HEADS, GLA_DK, GLA_DV), F32)
    ks, vs, ss = [], [], []
    for l in range(DEPTH):
        lw = _layer_weights(l, w_in, w_gla_a2, b_gla_a, gla_norm_g, conv_w, attn_sink, w_branch, w_out,
                            ln1_g, ln1_b, w_pq, peer_keys, peer_u, peer_v, ln2_g, ln2_b)
        mod3 = mod[l].reshape(8, 1, 6 * D_MODEL)
        xp, p_ctx, s_f, s_b = _layer(xp, mod3, lw, batch, seq, False, zeros_state, zeros_state, None)
        ks.append(p_ctx[:, P_AK:P_AK + 256].reshape(batch, seq, ATT_KV_HEADS, ATT_HD))
        vs.append(p_ctx[:, P_AV:P_AV + 256].reshape(batch, seq, ATT_KV_HEADS, ATT_HD))
        ss.append(jnp.stack([s_f, s_b], axis=1))
        ctx = {"k": cache_k[:, l].reshape(dec_batch, past, ATT_KV_HEADS * ATT_HD),
               "v": cache_v[:, l].reshape(dec_batch, past, ATT_KV_HEADS * ATT_HD),
               "cos": cos_t, "sin": sin_t}
        xs, _, _, _ = _layer(xs, mod3, lw, dec_batch, dec_seq, True,
                             state_gla[:, l, 0], state_gla[:, l, 1], ctx)
    return (xp.reshape(batch, seq, D_MODEL), xs.reshape(dec_batch, dec_seq, D_MODEL),
            jnp.stack(ks, axis=1), jnp.stack(vs, axis=1), jnp.stack(ss, axis=1))
```

```python
import functools

import numpy as np
import jax
import jax.numpy as jnp
from jax import lax
from jax.experimental import pallas as pl
from jax.experimental.pallas import tpu as pltpu

F32 = jnp.float32
BF16 = jnp.bfloat16
HIGHEST = lax.Precision.HIGHEST

D_MODEL = 1024
DEPTH = 2
GRID_W = 64
LN_EPS = 1e-6
DN_ALPHA = float((2 * DEPTH) ** 0.25)

GLA_HEADS = 4
GLA_DK = 128
GLA_DV = 256
GLA_LR = 16
GLA_TAU = 16.0
CONV_K = 3
ATT_HD = 64
ATT_HEADS = 16
ATT_KV_HEADS = 4
ATT_GROUP = 4
ATT_BLOCK = 128
ROPE_THETA = 10000.0
PEER_HEADS = 8
PEER_NKEYS = 128
PEER_EXPERTS = PEER_NKEYS * PEER_NKEYS
PEER_TOPK = 16
PEER_DQ = 256

O_GQ, O_GK, O_GV, O_GG, O_GA = 0, 512, 1024, 2048, 3072
O_CH, O_CB, O_CC, O_AQ, O_AK, O_AV, O_MG = 3104, 4128, 5152, 6176, 7200, 7456, 7712
N_IN = 10784
P_GQ, P_GK, P_GV, P_GG = 0, 512, 1024, 2048
P_CH, P_CB, P_CC, P_AQ, P_MG = 3072, 4096, 5120, 6144, 7168
P_AK, P_AV, P_GA = 10240, 10496, 10752
N_P = 10880

LANE = 128
SUBLANE = 8
VMEM_LIMIT = 56 * 1024 * 1024

TM = 256
TM_IN = 512
TN_IN = 2176
GLA_BLK = 256
GLA_LEVELS = 8
PEER_TT = 128
PEER_GROUP = 8
PEER_RING = 4
PEER_AHEAD = 2
PEER_SLOTS = PEER_RING * PEER_GROUP


def _cparams(*sem):
    return pltpu.CompilerParams(dimension_semantics=sem, vmem_limit_bytes=VMEM_LIMIT)


def _layer_norm(x, g, b):
    mu = jnp.mean(x, axis=-1, keepdims=True)
    xc = x - mu
    var = jnp.mean(xc * xc, axis=-1, keepdims=True)
    return xc * lax.rsqrt(var + LN_EPS) * g + b


def _nt_dot(a, b):
    return lax.dot_general(a, b, (((1,), (1,)), ((), ())), preferred_element_type=F32)


def _mod_kernel(c_ref, w_ref, b_ref, o_ref):
    c = c_ref[...]
    s = c * jax.nn.sigmoid(c)
    o_ref[0] = jnp.dot(s, w_ref[0], precision=HIGHEST, preferred_element_type=F32) + b_ref[0]


def _modulation(cond8, w_mod, b_mod):
    tn = 1536
    return pl.pallas_call(
        _mod_kernel,
        grid=(DEPTH, 6 * D_MODEL // tn),
        in_specs=[pl.BlockSpec((8, D_MODEL), lambda l, j: (0, 0)),
                  pl.BlockSpec((1, D_MODEL, tn), lambda l, j: (l, 0, j)),
                  pl.BlockSpec((1, 1, tn), lambda l, j: (l, 0, j))],
        out_specs=pl.BlockSpec((1, 8, tn), lambda l, j: (l, 0, j)),
        out_shape=jax.ShapeDtypeStruct((DEPTH, 8, 6 * D_MODEL), F32),
        compiler_params=_cparams("arbitrary", "arbitrary"),
        name="ada_modulation",
    )(cond8, w_mod, b_mod.reshape(DEPTH, 1, 6 * D_MODEL))


def _mod_spec(piece, row_fn):
    return pl.BlockSpec((None, 1, D_MODEL), lambda *ids: (row_fn(*ids), 0, piece))


def _ln_kernel(x_ref, g_ref, b_ref, o_ref):
    o_ref[...] = _layer_norm(x_ref[...], g_ref[...], b_ref[...])


def _ln_in(x2d, g, b):
    nt = x2d.shape[0]
    return pl.pallas_call(
        _ln_kernel,
        grid=(nt // TM_IN,),
        in_specs=[pl.BlockSpec((TM_IN, D_MODEL), lambda i: (i, 0)),
                  pl.BlockSpec((1, D_MODEL), lambda i: (0, 0)),
                  pl.BlockSpec((1, D_MODEL), lambda i: (0, 0))],
        out_specs=pl.BlockSpec((TM_IN, D_MODEL), lambda i: (i, 0)),
        out_shape=jax.ShapeDtypeStruct((nt, D_MODEL), F32),
        compiler_params=_cparams("arbitrary"),
        name="ln_in",
    )(x2d, g.reshape(1, D_MODEL), b.reshape(1, D_MODEL))


def _inproj_kernel(x_ref, sh_ref, sc_ref, w_ref, o_ref):
    h = x_ref[...] * (1.0 + sc_ref[...]) + sh_ref[...]
    o_ref[...] = jnp.dot(h.astype(BF16), w_ref[...], preferred_element_type=F32)


def _inproj(x2d, mod3, w_in_p, seq_len, lat):
    nt = x2d.shape[0]
    tiles_per_seq = max(seq_len // TM_IN, 1)
    row = (lambda j, i: 1 + i // tiles_per_seq) if lat else (lambda j, i: 0)
    return pl.pallas_call(
        _inproj_kernel,
        grid=(N_P // TN_IN, nt // TM_IN),
        in_specs=[pl.BlockSpec((TM_IN, D_MODEL), lambda j, i: (i, 0)),
                  _mod_spec(0, row), _mod_spec(1, row),
                  pl.BlockSpec((D_MODEL, TN_IN), lambda j, i: (0, j))],
        out_specs=pl.BlockSpec((TM_IN, TN_IN), lambda j, i: (i, j)),
        out_shape=jax.ShapeDtypeStruct((nt, N_P), F32),
        compiler_params=_cparams("arbitrary", "arbitrary"),
        name="in_projection",
    )(x2d, mod3, mod3, w_in_p)


def _gla_tables(rev):
    n = GLA_BLK
    idx = np.arange(n)
    if rev:
        cum = (idx[None, :] >= idx[:, None]).astype(np.float32)
    else:
        cum = (idx[None, :] <= idx[:, None]).astype(np.float32)
    rows = [cum]
    for k in range(GLA_LEVELS):
        h = 1 << k
        base = idx & ~(2 * h - 1)
        piv = base + (h if rev else h - 1)
        rows.append(cum[piv])
    rows.append(cum[np.zeros(n, np.int64) if rev else np.full(n, n - 1)])
    mall = np.concatenate(rows, axis=0)
    t, s = idx[:, None], idx[None, :]
    x = t ^ s
    top = np.where(x > 0, np.floor(np.log2(np.maximum(x, 1))).astype(np.int64) + 1, 0)
    attend = (s >= t) if rev else (s <= t)
    lv = np.where(attend, top, -1).astype(np.int32)
    return jnp.asarray(mall, BF16), jnp.asarray(lv)


def _gla_kernel(rev, q_ref, k_ref, v_ref, ga_ref, wa_ref, ba_ref, mall_ref, lv_ref, s0_ref,
                o_ref, sout_ref, st_ref, t_ref):
    j = pl.program_id(1)
    n = GLA_BLK

    @pl.when(j == 0)
    def _():
        for h in range(GLA_HEADS):
            st_ref[h] = s0_ref[0, h].T

    z = jnp.dot(ga_ref[...], wa_ref[...], precision=HIGHEST, preferred_element_type=F32) + ba_ref[...]
    a = (jnp.minimum(z, 0.0) - jnp.log(1.0 + jnp.exp(-jnp.abs(z)))) * (1.0 / GLA_TAU)
    a_hi = a.astype(BF16)
    r1 = a - a_hi.astype(F32)
    a_mid = r1.astype(BF16)
    a_lo = (r1 - a_mid.astype(F32)).astype(BF16)
    mall = mall_ref[...]
    t_ref[...] = (jnp.dot(mall, a_hi, preferred_element_type=F32)
                  + jnp.dot(mall, a_mid, preferred_element_type=F32)
                  + jnp.dot(mall, a_lo, preferred_element_type=F32))

    lv = lv_ref[...]
    for h in range(GLA_HEADS):
        kc = slice(GLA_DK * h, GLA_DK * (h + 1))
        vc = slice(GLA_DV * h, GLA_DV * (h + 1))
        b = t_ref[0:n, kc]
        btot = t_ref[(GLA_LEVELS + 1) * n:(GLA_LEVELS + 2) * n, kc]
        q = q_ref[:, kc] * (GLA_DK ** -0.5)
        k = k_ref[:, kc]
        v = v_ref[:, vc]
        att = jnp.where(lv == 0, _nt_dot(q.astype(BF16), k.astype(BF16)), 0.0)
        for lev in range(GLA_LEVELS):
            pv = t_ref[(lev + 1) * n:(lev + 2) * n, kc]
            ql = (q * jnp.exp(jnp.minimum(b - pv, 0.0))).astype(BF16)
            kl = (k * jnp.exp(jnp.minimum(pv - b, 0.0))).astype(BF16)
            att = jnp.where(lv == lev + 1, _nt_dot(ql, kl), att)
        st = st_ref[h]
        qi = (q * jnp.exp(b)).astype(BF16)
        o = (jnp.dot(att.astype(BF16), v.astype(BF16), preferred_element_type=F32)
             + _nt_dot(qi, st.astype(BF16)))
        o_ref[:, vc] = o
        kh = (k * jnp.exp(btot - b)).astype(BF16)
        st_ref[h] = st * jnp.exp(btot[0:1, :]) + jnp.dot(v.T.astype(BF16), kh, preferred_element_type=F32)

    @pl.when(j == pl.num_programs(1) - 1)
    def _():
        for h in range(GLA_HEADS):
            sout_ref[0, h] = st_ref[h].T


def _gla(p2d, s0, wa_pad, ba, nseq, seq_len, rev):
    nt = p2d.shape[0]
    nblk = seq_len // GLA_BLK
    mall, lv = _gla_tables(rev)

    def rb(b, j):
        return b * nblk + (nblk - 1 - j if rev else j)

    return pl.pallas_call(
        functools.partial(_gla_kernel, rev),
        grid=(nseq, nblk),
        in_specs=[pl.BlockSpec((GLA_BLK, 512), lambda b, j: (rb(b, j), P_GQ // 512)),
                  pl.BlockSpec((GLA_BLK, 512), lambda b, j: (rb(b, j), P_GK // 512)),
                  pl.BlockSpec((GLA_BLK, 1024), lambda b, j: (rb(b, j), P_GV // 1024)),
                  pl.BlockSpec((GLA_BLK, LANE), lambda b, j: (rb(b, j), P_GA // LANE)),
                  pl.BlockSpec((LANE, 512), lambda b, j: (0, 0)),
                  pl.BlockSpec((1, 512), lambda b, j: (0, 0)),
                  pl.BlockSpec(mall.shape, lambda b, j: (0, 0)),
                  pl.BlockSpec(lv.shape, lambda b, j: (0, 0)),
                  pl.BlockSpec((1, GLA_HEADS, GLA_DK, GLA_DV), lambda b, j: (b, 0, 0, 0))],
        out_specs=[pl.BlockSpec((GLA_BLK, 1024), lambda b, j: (rb(b, j), 0)),
                   pl.BlockSpec((1, GLA_HEADS, GLA_DK, GLA_DV), lambda b, j: (b, 0, 0, 0))],
        out_shape=[jax.ShapeDtypeStruct((nt, 1024), F32),
                   jax.ShapeDtypeStruct((nseq, GLA_HEADS, GLA_DK, GLA_DV), F32)],
        scratch_shapes=[pltpu.VMEM((GLA_HEADS, GLA_DV, GLA_DK), F32),
                        pltpu.VMEM(((GLA_LEVELS + 2) * GLA_BLK, 512), F32)],
        compiler_params=_cparams("arbitrary", "arbitrary"),
        name="gla_bwd" if rev else "gla_fwd",
    )(p2d, p2d, p2d, p2d, wa_pad, ba, mall, lv, s0)


def _softmax_sink_heads(q_rows, k_all, v_all, bias, sink_col):
    s = _nt_dot(q_rows.astype(BF16), k_all.astype(BF16))
    if bias is not None:
        s = s + bias
    m = jnp.maximum(jnp.max(s, axis=-1, keepdims=True), sink_col)
    p = jnp.exp(s - m)
    den = jnp.sum(p, axis=-1, keepdims=True) + jnp.exp(sink_col - m)
    o = jnp.dot(p.astype(BF16), v_all.astype(BF16), preferred_element_type=F32)
    return o / den


def _ctx_attn_kernel(sink_ref, q_ref, k_ref, v_ref, o_ref):
    t = q_ref.shape[0]
    rows = lax.broadcasted_iota(jnp.int32, (ATT_GROUP * t, 1), 0)
    for g in range(ATT_KV_HEADS):
        kg = k_ref[:, ATT_HD * g:ATT_HD * (g + 1)]
        vg = v_ref[:, ATT_HD * g:ATT_HD * (g + 1)]
        qs, sink_col = [], jnp.zeros((ATT_GROUP * t, 1), F32)
        for i in range(ATT_GROUP):
            hh = g * ATT_GROUP + i
            qs.append(q_ref[:, ATT_HD * hh:ATT_HD * (hh + 1)] * (ATT_HD ** -0.5))
            sink_col = jnp.where(rows // t == i, sink_ref[hh], sink_col)
        o = _softmax_sink_heads(jnp.concatenate(qs, axis=0), kg, vg, None, sink_col)
        for i in range(ATT_GROUP):
            hh = g * ATT_GROUP + i
            o_ref[:, ATT_HD * hh:ATT_HD * (hh + 1)] = o[i * t:(i + 1) * t]


def _ctx_attention(p2d, sink, nseq, seq_len):
    nt = p2d.shape[0]
    return pl.pallas_call(
        _ctx_attn_kernel,
        grid=(nseq,),
        in_specs=[pl.BlockSpec(memory_space=pltpu.SMEM),
                  pl.BlockSpec((seq_len, 1024), lambda b: (b, P_AQ // 1024)),
                  pl.BlockSpec((seq_len, 256), lambda b: (b, P_AK // 256)),
                  pl.BlockSpec((seq_len, 256), lambda b: (b, P_AV // 256))],
        out_specs=pl.BlockSpec((seq_len, 1024), lambda b: (b, 0)),
        out_shape=jax.ShapeDtypeStruct((nt, 1024), F32),
        compiler_params=_cparams("arbitrary"),
        name="context_attention",
    )(sink, p2d, p2d, p2d)


def _rope(x, cos_t, sin_t):
    even = lax.broadcasted_iota(jnp.int32, x.shape, 1) % 2 == 0
    swapped = jnp.where(even, pltpu.roll(x, LANE - 1, 1), pltpu.roll(x, 1, 1))
    return x * cos_t + swapped * sin_t


def _lat_attn_kernel(sink_ref, q_ref, kp_ref, kc_ref, kn_ref, vp_ref, vc_ref, vn_ref,
                     ck_ref, cv_ref, cq_ref, sq_ref, cp_ref, sp_ref, cn_ref, sn_ref, o_ref):
    n = pl.program_id(1)
    nb = pl.num_programs(1)
    t = ATT_BLOCK
    kblocks = []
    for kref, c_ref, s_ref in ((kp_ref, cp_ref, sp_ref), (kc_ref, cq_ref, sq_ref), (kn_ref, cn_ref, sn_ref)):
        kblocks.append(jnp.concatenate(
            [_rope(kref[:, LANE * c:LANE * (c + 1)], c_ref[...], s_ref[...]) for c in range(2)], axis=1))
    k_loc = jnp.concatenate(kblocks, axis=0)
    v_loc = jnp.concatenate([vp_ref[...], vc_ref[...], vn_ref[...]], axis=0)
    k_all = jnp.concatenate([k_loc, ck_ref[...]], axis=0)
    v_all = jnp.concatenate([v_loc, cv_ref[...]], axis=0)
    tc = ck_ref.shape[0]
    qi = lax.broadcasted_iota(jnp.int32, (t, 3 * t + tc), 0)
    kj = lax.broadcasted_iota(jnp.int32, (t, 3 * t + tc), 1)
    ninf = jnp.float32(-jnp.inf)
    prev_bias = jnp.where(n > 0, jnp.float32(0.0), ninf)
    next_bias = jnp.where(n < nb - 1, jnp.float32(0.0), ninf)
    bias1 = jnp.where(kj < t, jnp.where(kj >= qi, prev_bias, ninf),
                      jnp.where(kj < 2 * t, 0.0,
                                jnp.where(kj < 3 * t, jnp.where(kj - 2 * t <= qi, next_bias, ninf), 0.0)))
    bias = jnp.concatenate([bias1] * ATT_GROUP, axis=0)
    rows = lax.broadcasted_iota(jnp.int32, (ATT_GROUP * t, 1), 0)
    qr = [_rope(q_ref[:, LANE * c:LANE * (c + 1)], cq_ref[...], sq_ref[...]) * (ATT_HD ** -0.5)
          for c in range(ATT_HEADS * ATT_HD // LANE)]
    for g in range(ATT_KV_HEADS):
        qs, sink_col = [], jnp.zeros((ATT_GROUP * t, 1), F32)
        for i in range(ATT_GROUP):
            hh = g * ATT_GROUP + i
            slab = qr[hh // 2]
            qs.append(slab[:, ATT_HD * (hh % 2):ATT_HD * (hh % 2 + 1)])
            sink_col = jnp.where(rows // t == i, sink_ref[hh], sink_col)
        kg = k_all[:, ATT_HD * g:ATT_HD * (g + 1)]
        vg = v_all[:, ATT_HD * g:ATT_HD * (g + 1)]
        o = _softmax_sink_heads(jnp.concatenate(qs, axis=0), kg, vg, bias, sink_col)
        for i in range(ATT_GROUP):
            hh = g * ATT_GROUP + i
            o_ref[:, ATT_HD * hh:ATT_HD * (hh + 1)] = o[i * t:(i + 1) * t]


def _lat_attention(p2d, sink, cache_k, cache_v, cos_t, sin_t, nseq, seq_len):
    nt = p2d.shape[0]
    nb = seq_len // ATT_BLOCK
    tc = cache_k.shape[1]
    cur = lambda b, n: b * nb + n
    prv = lambda b, n: b * nb + jnp.maximum(n - 1, 0)
    nxt = lambda b, n: b * nb + jnp.minimum(n + 1, nb - 1)
    kv = lambda f, col: pl.BlockSpec((ATT_BLOCK, 256), lambda b, n: (f(b, n), col))
    tab = lambda f: pl.BlockSpec((ATT_BLOCK, LANE), lambda b, n: (f(0, n), 0))
    return pl.pallas_call(
        _lat_attn_kernel,
        grid=(nseq, nb),
        in_specs=[pl.BlockSpec(memory_space=pltpu.SMEM),
                  pl.BlockSpec((ATT_BLOCK, 1024), lambda b, n: (cur(b, n), P_AQ // 1024)),
                  kv(prv, P_AK // 256), kv(cur, P_AK // 256), kv(nxt, P_AK // 256),
                  kv(prv, P_AV // 256), kv(cur, P_AV // 256), kv(nxt, P_AV // 256),
                  pl.BlockSpec((None, tc, 256), lambda b, n: (b, 0, 0)),
                  pl.BlockSpec((None, tc, 256), lambda b, n: (b, 0, 0)),
                  tab(cur), tab(cur), tab(prv), tab(prv), tab(nxt), tab(nxt)],
        out_specs=pl.BlockSpec((ATT_BLOCK, 1024), lambda b, n: (cur(b, n), 0)),
        out_shape=jax.ShapeDtypeStruct((nt, 1024), F32),
        compiler_params=_cparams("arbitrary", "arbitrary"),
        name="latent_window_attention",
    )(sink, p2d, p2d, p2d, p2d, p2d, p2d, p2d, cache_k, cache_v,
      cos_t, sin_t, cos_t, sin_t, cos_t, sin_t)


def _rope_tables(seq_len):
    rows = seq_len // GRID_W
    row = jnp.repeat(jnp.arange(rows, dtype=F32), GRID_W)
    col = jnp.tile(jnp.arange(GRID_W, dtype=F32), rows)
    half = ATT_HD // 2
    inv = ROPE_THETA ** (-jnp.arange(0, half, 2, dtype=F32) / half)
    ang = jnp.concatenate([row[:, None] * inv, col[:, None] * inv], -1)
    cos, sin = jnp.cos(ang), jnp.sin(ang)
    cos_t = jnp.tile(jnp.repeat(cos, 2, axis=1), (1, LANE // ATT_HD))
    sign = jnp.tile(jnp.asarray([-1.0, 1.0], F32), ATT_HD // 2)
    sin_t = jnp.tile(jnp.repeat(sin, 2, axis=1) * sign, (1, LANE // ATT_HD))
    return cos_t, sin_t


def _merge_kernel(tiles_per_seq, of_ref, ob_ref, gg_ref, ch_ref, cb_ref, cc_ref,
                  chp_ref, ccp_ref, chn_ref, ccn_ref, yc_ref, mga_ref, mgb_ref, mgc_ref, x_ref, g1_ref,
                  gn_ref, cw_ref, wb_ref, wo_ref, lg_ref, lb_ref, o_ref):
    i = pl.program_id(0)
    tm = x_ref.shape[0]
    gn = gn_ref[...]
    ya = []
    for h in range(GLA_HEADS):
        vc = slice(GLA_DV * h, GLA_DV * (h + 1))
        o = of_ref[:, vc] + ob_ref[:, vc]
        o = o * lax.rsqrt(jnp.mean(o * o, axis=-1, keepdims=True) + LN_EPS) * gn
        gg = gg_ref[:, vc]
        ya.append(o * (gg * jax.nn.sigmoid(gg)))
    ya = jnp.concatenate(ya, axis=1)
    z = cc_ref[...] * ch_ref[...]
    has_prev = (i % tiles_per_seq) != 0
    has_next = (i % tiles_per_seq) != tiles_per_seq - 1
    z_before = jnp.where(has_prev, ccp_ref[SUBLANE - 1:SUBLANE, :] * chp_ref[SUBLANE - 1:SUBLANE, :], 0.0)
    z_after = jnp.where(has_next, ccn_ref[0:1, :] * chn_ref[0:1, :], 0.0)
    r = lax.broadcasted_iota(jnp.int32, z.shape, 0)
    zp = jnp.where(r == 0, z_before, pltpu.roll(z, 1, 0))
    zn = jnp.where(r == tm - 1, z_after, pltpu.roll(z, tm - 1, 0))
    yb = cb_ref[...] * (cw_ref[0:1, :] * zp + cw_ref[1:2, :] * z + cw_ref[2:3, :] * zn)
    merged = jnp.zeros((tm, D_MODEL), F32)
    for bi, (y, mg_ref) in enumerate(((ya, mga_ref), (yb, mgb_ref), (yc_ref[...], mgc_ref))):
        proj = jnp.dot(y.astype(BF16), wb_ref[bi], preferred_element_type=F32)
        merged = merged + jax.nn.sigmoid(mg_ref[...]) * proj
    mix = jnp.dot(merged.astype(BF16), wo_ref[...], preferred_element_type=F32)
    o_ref[...] = _layer_norm(DN_ALPHA * x_ref[...] + g1_ref[...] * mix, lg_ref[...], lb_ref[...])


def _merge(x2d, p2d, o_f, o_b, y_c, mod3, gn, conv_w, wb, wo, ln_g, ln_b, seq_len, lat):
    nt = x2d.shape[0]
    tps = seq_len // TM
    row = (lambda i: 1 + i // tps) if lat else (lambda i: 0)
    col = lambda c: pl.BlockSpec((TM, 1024), lambda i: (i, c // 1024))
    halo_rows = TM // SUBLANE
    nhalo = nt // SUBLANE
    prev = lambda c: pl.BlockSpec((SUBLANE, 1024), lambda i: (jnp.maximum(i * halo_rows - 1, 0), c // 1024))
    nxt = lambda c: pl.BlockSpec((SUBLANE, 1024),
                                 lambda i: (jnp.minimum((i + 1) * halo_rows, nhalo - 1), c // 1024))
    full = lambda shape: pl.BlockSpec(shape, lambda i: (0,) * len(shape))
    return pl.pallas_call(
        functools.partial(_merge_kernel, tps),
        grid=(nt // TM,),
        in_specs=[pl.BlockSpec((TM, 1024), lambda i: (i, 0)), pl.BlockSpec((TM, 1024), lambda i: (i, 0)),
                  col(P_GG), col(P_CH), col(P_CB), col(P_CC),
                  prev(P_CH), prev(P_CC), nxt(P_CH), nxt(P_CC),
                  pl.BlockSpec((TM, 1024), lambda i: (i, 0)),
                  col(P_MG), col(P_MG + 1024), col(P_MG + 2048),
                  pl.BlockSpec((TM, 1024), lambda i: (i, 0)),
                  _mod_spec(2, row),
                  full((1, GLA_DV)), full((SUBLANE, 1024)), full((3, 1024, 1024)), full((1024, 1024)),
                  full((1, 1024)), full((1, 1024))],
        out_specs=pl.BlockSpec((TM, 1024), lambda i: (i, 0)),
        out_shape=jax.ShapeDtypeStruct((nt, 1024), F32),
        compiler_params=_cparams("arbitrary"),
        name="merge_ln1",
    )(o_f, o_b, p2d, p2d, p2d, p2d, p2d, p2d, p2d, p2d, y_c, p2d, p2d, p2d, x2d, mod3,
      gn, conv_w, wb, wo, ln_g, ln_b)


def _topk_rows(s, kk):
    n = s.shape[0]
    iota = lax.broadcasted_iota(jnp.int32, s.shape, 0).astype(F32)
    vals, idxs = [], []
    for _ in range(kk):
        m = jnp.max(s, axis=0, keepdims=True)
        idx = jnp.min(jnp.where(s == m, iota, float(n)), axis=0, keepdims=True)
        vals.append(m)
        idxs.append(idx)
        s = jnp.where(iota == idx, -jnp.inf, s)
    return jnp.concatenate(vals, axis=0), jnp.concatenate(idxs, axis=0)


def _staircase():
    return [(i, PEER_TOPK // (i + 1)) for i in range(PEER_TOPK)]


def _peer_topk_kernel(x_ref, sh_ref, sc_ref, w_ref, keys_ref, pos_ref, eidx_ref, gate_ref):
    h2 = x_ref[...] * (1.0 + sc_ref[...]) + sh_ref[...]
    pq = jnp.dot(h2.astype(BF16), w_ref[...], preferred_element_type=F32)
    half = PEER_DQ // 2
    tops = []
    for p in range(2):
        s = _nt_dot(keys_ref[p], pq[:, half * p:half * (p + 1)].astype(BF16))
        tops.append(_topk_rows(s, PEER_TOPK))
    (s1, i1), (s2, i2) = tops
    tm = s1.shape[1]
    kk = PEER_TOPK
    pos = pos_ref[...]
    npad = pos.shape[0] - sum(nj for _, nj in _staircase())
    cand = jnp.concatenate([s1[i:i + 1] + s2[0:nj] for i, nj in _staircase()]
                           + [jnp.full((npad, tm), -jnp.inf, F32)], axis=0)
    cidx = jnp.concatenate([i1[i:i + 1] * float(PEER_NKEYS) + i2[0:nj] for i, nj in _staircase()]
                           + [jnp.zeros((npad, tm), F32)], axis=0)
    top_s, top_e = [], []
    for _ in range(kk):
        m = jnp.max(cand, axis=0, keepdims=True)
        at = jnp.min(jnp.where(cand == m, pos, float(2 * kk * kk)), axis=0, keepdims=True)
        hit = pos == at
        top_s.append(m)
        top_e.append(jnp.sum(jnp.where(hit, cidx, 0.0), axis=0, keepdims=True))
        cand = jnp.where(hit, -jnp.inf, cand)
    top_s = jnp.concatenate(top_s, axis=0)
    e = jnp.exp(top_s - top_s[0:1])
    gate_ref[...] = e / jnp.sum(e, axis=0, keepdims=True)
    eidx_ref[...] = jnp.concatenate(top_e, axis=0).astype(jnp.int32)


def _peer_topk(x2d, mod3, w_pq, keys, seq_len, lat):
    nt = x2d.shape[0]
    tps = seq_len // TM
    row = (lambda i, h: 1 + i // tps) if lat else (lambda i, h: 0)
    flat = [i * PEER_TOPK + j for i, nj in _staircase() for j in range(nj)]
    nrows = -(-len(flat) // SUBLANE) * SUBLANE
    flat = flat + [PEER_TOPK * PEER_TOPK] * (nrows - len(flat))
    pos = jnp.asarray(np.repeat(np.asarray(flat, np.float32)[:, None], TM, axis=1))
    return pl.pallas_call(
        _peer_topk_kernel,
        grid=(nt // TM, PEER_HEADS),
        in_specs=[pl.BlockSpec((TM, 1024), lambda i, h: (i, 0)),
                  _mod_spec(3, row), _mod_spec(4, row),
                  pl.BlockSpec((1024, PEER_DQ), lambda i, h: (0, h)),
                  pl.BlockSpec((None, 2, PEER_NKEYS, PEER_DQ // 2), lambda i, h: (h, 0, 0, 0)),
                  pl.BlockSpec((nrows, TM), lambda i, h: (0, 0))],
        out_specs=[pl.BlockSpec((None, PEER_TOPK, TM), lambda i, h: (h, 0, i)),
                   pl.BlockSpec((None, PEER_TOPK, TM), lambda i, h: (h, 0, i))],
        out_shape=[jax.ShapeDtypeStruct((PEER_HEADS, PEER_TOPK, nt), jnp.int32),
                   jax.ShapeDtypeStruct((PEER_HEADS, PEER_TOPK, nt), F32)],
        compiler_params=_cparams("arbitrary", "arbitrary"),
        name="peer_topk",
    )(x2d, mod3, mod3, w_pq, keys, pos)


def _peer_expert_kernel(eidx_ref, x_ref, sh_ref, sc_ref, g2_ref, gate_ref, lg_ref, lb_ref, col_ref, exp_ref,
                        uv_hbm, o_ref, *scratch):
    npair = PEER_HEADS * PEER_TOPK
    ngroup = PEER_TT // PEER_GROUP
    nrow = npair * SUBLANE
    blocks, (sems, ffn_ref) = scratch[:PEER_RING], scratch[PEER_RING:]

    def ring(slot):
        return blocks[slot // PEER_GROUP].at[slot % PEER_GROUP]

    def slot_copy(slot):
        return pltpu.make_async_copy(uv_hbm.at[pl.ds(0, npair)], ring(slot), sems.at[slot])

    def issue_token(tok, slot):
        for r in range(npair):
            pltpu.make_async_copy(uv_hbm.at[eidx_ref[tok, r]], ring(slot).at[r],
                                  sems.at[slot]).start(priority=r % 2)

    @pl.when(pl.program_id(0) == 0)
    def _():
        for t in range(PEER_AHEAD * PEER_GROUP):
            issue_token(t, t)

    rowid = lax.broadcasted_iota(jnp.int32, (PEER_GROUP, 1), 0)
    own = (lax.broadcasted_iota(jnp.int32, (SUBLANE, nrow), 1) % SUBLANE
           == lax.broadcasted_iota(jnp.int32, (SUBLANE, nrow), 0))
    sc, sh = sc_ref[...], sh_ref[...]

    def ring_round(k, carry):
        for q in range(PEER_RING):
            g = k * PEER_RING + q
            base = q * PEER_GROUP
            ahead = ((q + PEER_AHEAD) % PEER_RING) * PEER_GROUP
            r0 = pl.multiple_of(g * PEER_GROUP, PEER_GROUP)
            for jj in range(PEER_GROUP):
                slot_copy(base + jj).wait()
            z8 = jnp.zeros((PEER_GROUP, nrow), F32)
            for jj in range(PEER_GROUP):
                issue_token((g + PEER_AHEAD) * PEER_GROUP + jj, ahead + jj)
                h = (x_ref[r0 + jj] * (1.0 + sc) + sh).astype(BF16)
                u = ring(base + jj)[:, 0].reshape(nrow, LANE).astype(BF16)
                y = jnp.where(own, _nt_dot(h, u), 0.0)
                z8 = jnp.where(rowid == jj, jnp.sum(y, axis=0, keepdims=True), z8)
            z_hi = z8.astype(BF16)
            z_lo = (z8 - z_hi.astype(F32)).astype(BF16)
            s8 = (jnp.dot(z_hi, col_ref[...], preferred_element_type=F32)
                  + jnp.dot(z_lo, col_ref[...], preferred_element_type=F32))
            act = 0.5 * s8 * (1.0 + lax.erf(s8 * (2.0 ** -0.5)))
            w8 = (gate_ref[pl.ds(r0, PEER_GROUP), :] * act).astype(BF16)
            wexp = jnp.dot(w8, exp_ref[...], preferred_element_type=F32)
            for jj in range(PEER_GROUP):
                wsel = jnp.where(own, wexp[jj:jj + 1, :], 0.0).astype(BF16)
                v = ring(base + jj)[:, 1].reshape(nrow, LANE).astype(BF16)
                ffn_ref[r0 + jj] = jnp.dot(wsel, v, preferred_element_type=F32)
        return carry

    lax.fori_loop(0, ngroup // PEER_RING, ring_round, 0)

    @pl.when(pl.program_id(0) == pl.num_programs(0) - 1)
    def _():
        for t in range(PEER_AHEAD * PEER_GROUP):
            slot_copy(t).wait()

    y = DN_ALPHA * x_ref[...] + g2_ref[...] * ffn_ref[...]
    mu = jnp.sum(jnp.sum(y, axis=2, keepdims=True), axis=1, keepdims=True) * (1.0 / D_MODEL)
    yc = y - mu
    var = jnp.sum(jnp.sum(yc * yc, axis=2, keepdims=True), axis=1, keepdims=True) * (1.0 / D_MODEL)
    o_ref[...] = yc * lax.rsqrt(var + LN_EPS) * lg_ref[...] + lb_ref[...]


def _peer_experts(x2d, eidx, gate, mod, uv, ln_g, ln_b, seq_len, lat):
    nt = x2d.shape[0]
    tps = seq_len // PEER_TT
    row = (lambda i: 1 + i // tps) if lat else (lambda i: 0)
    npair = PEER_HEADS * PEER_TOPK
    nrow = npair * SUBLANE
    slab = (SUBLANE, LANE)
    mod4 = mod.reshape(8, 6, *slab)
    mod_spec = lambda piece: pl.BlockSpec((None, None) + slab, lambda i: (row(i), piece, 0, 0))
    full = lambda shape: pl.BlockSpec(shape, lambda i: (0,) * len(shape))
    collapse = jnp.asarray(np.arange(nrow)[:, None] // SUBLANE == np.arange(npair)[None, :], BF16)
    ntile = nt // PEER_TT
    nahead = PEER_AHEAD * PEER_GROUP
    tiles = eidx.reshape(ntile, PEER_TT, npair)
    nxt = jnp.concatenate([tiles[1:, :nahead], tiles[-1:, :nahead]], axis=0)
    eidx_ext = jnp.concatenate([tiles, nxt], axis=1)
    assert (PEER_TT // PEER_GROUP) % PEER_RING == 0 and PEER_AHEAD < PEER_RING
    out = pl.pallas_call(
        _peer_expert_kernel,
        grid=(ntile,),
        in_specs=[pl.BlockSpec((None, PEER_TT + nahead, npair), lambda i: (i, 0, 0), memory_space=pltpu.SMEM),
                  pl.BlockSpec((PEER_TT,) + slab, lambda i: (i, 0, 0)),
                  mod_spec(3), mod_spec(4), mod_spec(5),
                  pl.BlockSpec((PEER_TT, npair), lambda i: (i, 0)),
                  full(slab), full(slab), full((nrow, npair)), full((npair, nrow)),
                  pl.BlockSpec(memory_space=pl.ANY)],
        out_specs=pl.BlockSpec((PEER_TT,) + slab, lambda i: (i, 0, 0)),
        out_shape=jax.ShapeDtypeStruct((nt,) + slab, F32),
        scratch_shapes=[pltpu.VMEM((PEER_GROUP, npair, 2) + slab, F32)] * PEER_RING
                       + [pltpu.SemaphoreType.DMA((PEER_SLOTS,)),
                        pltpu.VMEM((PEER_TT,) + slab, F32)],
        compiler_params=_cparams("arbitrary"),
        name="peer_experts_ln2",
    )(eidx_ext, x2d.reshape((nt,) + slab), mod4, mod4, mod4, gate, ln_g.reshape(slab), ln_b.reshape(slab),
      collapse, collapse.T, uv)
    return out.reshape(nt, D_MODEL)


def _layer(x2d, mod3, lw, nseq, seq_len, lat, s0_f, s0_b, ctx):
    p2d = _inproj(x2d, mod3, lw["w_in"], seq_len, lat)
    o_f, s_f = _gla(p2d, s0_f, lw["wa_f"], lw["ba_f"], nseq, seq_len, rev=False)
    o_b, s_b = _gla(p2d, s0_b, lw["wa_b"], lw["ba_b"], nseq, seq_len, rev=True)
    if lat:
        y_c = _lat_attention(p2d, lw["sink"], ctx["k"], ctx["v"], ctx["cos"], ctx["sin"], nseq, seq_len)
    else:
        y_c = _ctx_attention(p2d, lw["sink"], nseq, seq_len)
    x1 = _merge(x2d, p2d, o_f, o_b, y_c, mod3, lw["gn"], lw["conv_w"], lw["wb"], lw["wo"],
                lw["ln1_g"], lw["ln1_b"], seq_len, lat)
    eidx_t, gate_t = _peer_topk(x1, mod3, lw["w_pq"], lw["keys"], seq_len, lat)
    nt = x2d.shape[0]
    eidx = eidx_t.reshape(PEER_HEADS * PEER_TOPK, nt).T
    gate = gate_t.reshape(PEER_HEADS * PEER_TOPK, nt).T
    x2 = _peer_experts(x1, eidx, gate, mod3, lw["uv"], lw["ln2_g"], lw["ln2_b"], seq_len, lat)
    return x2, p2d, s_f, s_b


def _layer_weights(l, w_in, w_gla_a2, b_gla_a, gla_norm_g, conv_w, attn_sink, w_branch, w_out,
                   ln1_g, ln1_b, w_pq, peer_keys, peer_u, peer_v, ln2_g, ln2_b):
    w = w_in[l]
    w_in_p = jnp.concatenate(
        [w[:, O_GQ:O_GA], w[:, O_CH:O_AK], w[:, O_MG:N_IN], w[:, O_AK:O_MG], w[:, O_GA:O_CH],
         jnp.zeros((D_MODEL, N_P - N_IN), F32)], axis=1).astype(BF16)

    def wa_pad(d):
        return jnp.zeros((LANE, 512), F32).at[GLA_LR * d:GLA_LR * (d + 1)].set(w_gla_a2[l, d])

    return {
        "w_in": w_in_p,
        "wa_f": wa_pad(0), "wa_b": wa_pad(1),
        "ba_f": b_gla_a[l, 0].reshape(1, 512), "ba_b": b_gla_a[l, 1].reshape(1, 512),
        "gn": gla_norm_g[l].reshape(1, GLA_DV),
        "conv_w": jnp.zeros((SUBLANE, 1024), F32).at[0:CONV_K].set(conv_w[l]),
        "sink": attn_sink[l],
        "wb": w_branch[l].astype(BF16), "wo": w_out[l].astype(BF16),
        "ln1_g": ln1_g[l].reshape(1, 1024), "ln1_b": ln1_b[l].reshape(1, 1024),
        "w_pq": w_pq[l].astype(BF16),
        "keys": peer_keys[l].astype(BF16),
        "uv": jnp.stack([peer_u[l].reshape(PEER_EXPERTS, SUBLANE, LANE),
                         peer_v[l].reshape(PEER_EXPERTS, SUBLANE, LANE)], axis=1),
        "ln2_g": ln2_g[l].reshape(1, 1024), "ln2_b": ln2_b[l].reshape(1, 1024),
    }


def kernel(x_prompt, x_sample, cache_k, cache_v, state_gla, c, c_ctx, ln_in_g, ln_in_b, w_mod, b_mod, w_in, w_gla_a2, b_gla_a, gla_norm_g, conv_w, attn_sink, w_branch, w_out, ln1_g, ln1_b, w_pq, peer_keys, peer_u, peer_v, ln2_g, ln2_b):
    batch, seq, _ = x_prompt.shape
    dec_batch, dec_seq, _ = x_sample.shape
    past = cache_k.shape[2]
    assert dec_batch + 1 <= 8 and seq % TM == 0 and dec_seq % TM_IN == 0 and seq % GLA_BLK == 0

    cond8 = jnp.zeros((8, D_MODEL), F32).at[0].set(c_ctx).at[1:1 + dec_batch].set(c)
    mod = _modulation(cond8, w_mod, b_mod)
    cos_t, sin_t = _rope_tables(dec_seq)

    xp = _ln_in(x_prompt.reshape(batch * seq, D_MODEL), ln_in_g, ln_in_b)
    xs = _ln_in(x_sample.reshape(dec_batch * dec_seq, D_MODEL), ln_in_g, ln_in_b)
    zeros_state = jnp.zeros((batch, GLA_HEADS, GLA_DK, GLA_DV), F32)
    ks, vs, ss = [], [], []
    for l in range(DEPTH):
        lw = _layer_weights(l, w_in, w_gla_a2, b_gla_a, gla_norm_g, conv_w, attn_sink, w_branch, w_out,
                            ln1_g, ln1_b, w_pq, peer_keys, peer_u, peer_v, ln2_g, ln2_b)
        mod3 = mod[l].reshape(8, 1, 6 * D_MODEL)
        xp, p_ctx, s_f, s_b = _layer(xp, mod3, lw, batch, seq, False, zeros_state, zeros_state, None)
        ks.append(p_ctx[:, P_AK:P_AK + 256].reshape(batch, seq, ATT_KV_HEADS, ATT_HD))
        vs.append(p_ctx[:, P_AV:P_AV + 256].reshape(batch, seq, ATT_KV_HEADS, ATT_HD))
        ss.append(jnp.stack([s_f, s_b], axis=1))
        ctx = {"k": cache_k[:, l].reshape(dec_batch, past, ATT_KV_HEADS * ATT_HD),
               "v": cache_v[:, l].reshape(dec_batch, past, ATT_KV_HEADS * ATT_HD),
               "cos": cos_t, "sin": sin_t}
        xs, _, _, _ = _layer(xs, mod3, lw, dec_batch, dec_seq, True,
                             state_gla[:, l, 0], state_gla[:, l, 1], ctx)
    return (xp.reshape(batch, seq, D_MODEL), xs.reshape(dec_batch, dec_seq, D_MODEL),
            jnp.stack(ks, axis=1), jnp.stack(vs, axis=1), jnp.stack(ss, axis=1))
```

```python
import functools

import numpy as np
import jax
import jax.numpy as jnp
from jax import lax
from jax.experimental import pallas as pl
from jax.experimental.pallas import tpu as pltpu

F32 = jnp.float32
BF16 = jnp.bfloat16
HIGHEST = lax.Precision.HIGHEST

D_MODEL = 1024
DEPTH = 2
GRID_W = 64
LN_EPS = 1e-6
DN_ALPHA = float((2 * DEPTH) ** 0.25)

GLA_HEADS = 4
GLA_DK = 128
GLA_DV = 256
GLA_LR = 16
GLA_TAU = 16.0
CONV_K = 3
ATT_HD = 64
ATT_HEADS = 16
ATT_KV_HEADS = 4
ATT_GROUP = 4
ATT_BLOCK = 128
ROPE_THETA = 10000.0
PEER_HEADS = 8
PEER_NKEYS = 128
PEER_EXPERTS = PEER_NKEYS * PEER_NKEYS
PEER_TOPK = 16
PEER_DQ = 256

O_GQ, O_GK, O_GV, O_GG, O_GA = 0, 512, 1024, 2048, 3072
O_CH, O_CB, O_CC, O_AQ, O_AK, O_AV, O_MG = 3104, 4128, 5152, 6176, 7200, 7456, 7712
N_IN = 10784
P_GQ, P_GK, P_GV, P_GG = 0, 512, 1024, 2048
P_CH, P_CB, P_CC, P_AQ, P_MG = 3072, 4096, 5120, 6144, 7168
P_AK, P_AV, P_GA = 10240, 10496, 10752
N_P = 10880

LANE = 128
SUBLANE = 8
VMEM_LIMIT = 56 * 1024 * 1024

TM = 256
TM_IN = 512
TN_IN = 2176
GLA_BLK = 256
GLA_LEVELS = 8
GLA_MM_LEVELS = 2
PEER_TT = 128
PEER_GROUP = 8
PEER_RING = 4
PEER_AHEAD = 2
PEER_SLOTS = PEER_RING * PEER_GROUP


def _cparams(*sem):
    return pltpu.CompilerParams(dimension_semantics=sem, vmem_limit_bytes=VMEM_LIMIT)


def _layer_norm(x, g, b):
    mu = jnp.mean(x, axis=-1, keepdims=True)
    xc = x - mu
    var = jnp.mean(xc * xc, axis=-1, keepdims=True)
    return xc * lax.rsqrt(var + LN_EPS) * g + b


def _nt_dot(a, b):
    return lax.dot_general(a, b, (((1,), (1,)), ((), ())), preferred_element_type=F32)


def _mod_kernel(c_ref, w_ref, b_ref, o_ref):
    c = c_ref[...]
    s = c * jax.nn.sigmoid(c)
    o_ref[0] = jnp.dot(s, w_ref[0], precision=HIGHEST, preferred_element_type=F32) + b_ref[0]


def _modulation(cond8, w_mod, b_mod):
    tn = 1536
    return pl.pallas_call(
        _mod_kernel,
        grid=(DEPTH, 6 * D_MODEL // tn),
        in_specs=[pl.BlockSpec((8, D_MODEL), lambda l, j: (0, 0)),
                  pl.BlockSpec((1, D_MODEL, tn), lambda l, j: (l, 0, j)),
                  pl.BlockSpec((1, 1, tn), lambda l, j: (l, 0, j))],
        out_specs=pl.BlockSpec((1, 8, tn), lambda l, j: (l, 0, j)),
        out_shape=jax.ShapeDtypeStruct((DEPTH, 8, 6 * D_MODEL), F32),
        compiler_params=_cparams("arbitrary", "arbitrary"),
        name="ada_modulation",
    )(cond8, w_mod, b_mod.reshape(DEPTH, 1, 6 * D_MODEL))


def _mod_spec(piece, row_fn):
    return pl.BlockSpec((None, 1, D_MODEL), lambda *ids: (row_fn(*ids), 0, piece))


def _ln_kernel(x_ref, g_ref, b_ref, o_ref):
    o_ref[...] = _layer_norm(x_ref[...], g_ref[...], b_ref[...])


def _ln_in(x2d, g, b):
    nt = x2d.shape[0]
    return pl.pallas_call(
        _ln_kernel,
        grid=(nt // TM_IN,),
        in_specs=[pl.BlockSpec((TM_IN, D_MODEL), lambda i: (i, 0)),
                  pl.BlockSpec((1, D_MODEL), lambda i: (0, 0)),
                  pl.BlockSpec((1, D_MODEL), lambda i: (0, 0))],
        out_specs=pl.BlockSpec((TM_IN, D_MODEL), lambda i: (i, 0)),
        out_shape=jax.ShapeDtypeStruct((nt, D_MODEL), F32),
        compiler_params=_cparams("arbitrary"),
        name="ln_in",
    )(x2d, g.reshape(1, D_MODEL), b.reshape(1, D_MODEL))


def _inproj_kernel(x_ref, sh_ref, sc_ref, w_ref, o_ref):
    h = x_ref[...] * (1.0 + sc_ref[...]) + sh_ref[...]
    o_ref[...] = jnp.dot(h.astype(BF16), w_ref[...], preferred_element_type=F32)


def _inproj(x2d, mod3, w_in_p, seq_len, lat):
    nt = x2d.shape[0]
    tiles_per_seq = max(seq_len // TM_IN, 1)
    row = (lambda j, i: 1 + i // tiles_per_seq) if lat else (lambda j, i: 0)
    return pl.pallas_call(
        _inproj_kernel,
        grid=(N_P // TN_IN, nt // TM_IN),
        in_specs=[pl.BlockSpec((TM_IN, D_MODEL), lambda j, i: (i, 0)),
                  _mod_spec(0, row), _mod_spec(1, row),
                  pl.BlockSpec((D_MODEL, TN_IN), lambda j, i: (0, j))],
        out_specs=pl.BlockSpec((TM_IN, TN_IN), lambda j, i: (i, j)),
        out_shape=jax.ShapeDtypeStruct((nt, N_P), F32),
        compiler_params=_cparams("arbitrary", "arbitrary"),
        name="in_projection",
    )(x2d, mod3, mod3, w_in_p)


def _gla_tables(rev):
    n = GLA_BLK
    idx = np.arange(n)
    if rev:
        cum = (idx[None, :] >= idx[:, None]).astype(np.float32)
    else:
        cum = (idx[None, :] <= idx[:, None]).astype(np.float32)
    rows = [cum]
    for k in range(GLA_MM_LEVELS):
        h = 1 << k
        rows.append(cum[(idx & ~(2 * h - 1)) + (h if rev else h - 1)])
    mall = np.concatenate(rows, axis=0)
    t, s = idx[:, None], idx[None, :]
    x = t ^ s
    top = np.where(x > 0, np.floor(np.log2(np.maximum(x, 1))).astype(np.int64) + 1, 0)
    attend = (s >= t) if rev else (s <= t)
    lv = np.where(attend, top, -1).astype(np.int32)
    return jnp.asarray(mall, BF16), jnp.asarray(lv)


def _gla_kernel(rev, q_ref, k_ref, v_ref, ga_ref, wa_ref, ba_ref, mall_ref, lv_ref, s0_ref,
                o_ref, sout_ref, st_ref, t_ref):
    j = pl.program_id(1)
    n = GLA_BLK

    @pl.when(j == 0)
    def _():
        for h in range(GLA_HEADS):
            st_ref[h] = s0_ref[0, h].T

    z = jnp.dot(ga_ref[...], wa_ref[...], precision=HIGHEST, preferred_element_type=F32) + ba_ref[...]
    a = (jnp.minimum(z, 0.0) - jnp.log(1.0 + jnp.exp(-jnp.abs(z)))) * (1.0 / GLA_TAU)
    a_hi = a.astype(BF16)
    r1 = a - a_hi.astype(F32)
    a_mid = r1.astype(BF16)
    a_lo = (r1 - a_mid.astype(F32)).astype(BF16)
    mall = mall_ref[...]
    t_ref[...] = (jnp.dot(mall, a_hi, preferred_element_type=F32)
                  + jnp.dot(mall, a_mid, preferred_element_type=F32)
                  + jnp.dot(mall, a_lo, preferred_element_type=F32))

    lv = lv_ref[...]
    for h in range(GLA_HEADS):
        kc = slice(GLA_DK * h, GLA_DK * (h + 1))
        vc = slice(GLA_DV * h, GLA_DV * (h + 1))
        b = t_ref[0:n, kc]
        btot = t_ref[0:1, kc] if rev else t_ref[n - 1:n, kc]
        q = q_ref[:, kc] * (GLA_DK ** -0.5)
        k = k_ref[:, kc]
        v = v_ref[:, vc]
        att = jnp.where(lv == 0, _nt_dot(q.astype(BF16), k.astype(BF16)), 0.0)
        for lev in range(GLA_LEVELS):
            if lev < GLA_MM_LEVELS:
                pv = t_ref[(lev + 1) * n:(lev + 2) * n, kc]
            else:
                hh = 1 << lev
                pv = jnp.concatenate(
                    [jnp.broadcast_to(t_ref[g0 + (hh if rev else hh - 1):g0 + (hh if rev else hh - 1) + 1, kc],
                                      (2 * hh, GLA_DK)) for g0 in range(0, n, 2 * hh)], axis=0)
            ql = (q * jnp.exp(jnp.minimum(b - pv, 0.0))).astype(BF16)
            kl = (k * jnp.exp(jnp.minimum(pv - b, 0.0))).astype(BF16)
            att = jnp.where(lv == lev + 1, _nt_dot(ql, kl), att)
        st = st_ref[h]
        qi = (q * jnp.exp(b)).astype(BF16)
        o = (jnp.dot(att.astype(BF16), v.astype(BF16), preferred_element_type=F32)
             + _nt_dot(qi, st.astype(BF16)))
        o_ref[:, vc] = o
        kh = (k * jnp.exp(btot - b)).astype(BF16)
        st_ref[h] = st * jnp.exp(btot) + jnp.dot(v.T.astype(BF16), kh, preferred_element_type=F32)

    @pl.when(j == pl.num_programs(1) - 1)
    def _():
        for h in range(GLA_HEADS):
            sout_ref[0, h] = st_ref[h].T


def _gla(p2d, s0, wa_pad, ba, nseq, seq_len, rev):
    nt = p2d.shape[0]
    nblk = seq_len // GLA_BLK
    mall, lv = _gla_tables(rev)

    def rb(b, j):
        return b * nblk + (nblk - 1 - j if rev else j)

    return pl.pallas_call(
        functools.partial(_gla_kernel, rev),
        grid=(nseq, nblk),
        in_specs=[pl.BlockSpec((GLA_BLK, 512), lambda b, j: (rb(b, j), P_GQ // 512)),
                  pl.BlockSpec((GLA_BLK, 512), lambda b, j: (rb(b, j), P_GK // 512)),
                  pl.BlockSpec((GLA_BLK, 1024), lambda b, j: (rb(b, j), P_GV // 1024)),
                  pl.BlockSpec((GLA_BLK, LANE), lambda b, j: (rb(b, j), P_GA // LANE)),
                  pl.BlockSpec((LANE, 512), lambda b, j: (0, 0)),
                  pl.BlockSpec((1, 512), lambda b, j: (0, 0)),
                  pl.BlockSpec(mall.shape, lambda b, j: (0, 0)),
                  pl.BlockSpec(lv.shape, lambda b, j: (0, 0)),
                  pl.BlockSpec((1, GLA_HEADS, GLA_DK, GLA_DV), lambda b, j: (b, 0, 0, 0))],
        out_specs=[pl.BlockSpec((GLA_BLK, 1024), lambda b, j: (rb(b, j), 0)),
                   pl.BlockSpec((1, GLA_HEADS, GLA_DK, GLA_DV), lambda b, j: (b, 0, 0, 0))],
        out_shape=[jax.ShapeDtypeStruct((nt, 1024), F32),
                   jax.ShapeDtypeStruct((nseq, GLA_HEADS, GLA_DK, GLA_DV), F32)],
        scratch_shapes=[pltpu.VMEM((GLA_HEADS, GLA_DV, GLA_DK), F32),
                        pltpu.VMEM(((GLA_MM_LEVELS + 1) * GLA_BLK, 512), F32)],
        compiler_params=_cparams("arbitrary", "arbitrary"),
        name="gla_bwd" if rev else "gla_fwd",
    )(p2d, p2d, p2d, p2d, wa_pad, ba, mall, lv, s0)


def _softmax_sink_heads(q_rows, k_all, v_all, bias, sink_col):
    s = _nt_dot(q_rows.astype(BF16), k_all.astype(BF16))
    if bias is not None:
        s = s + bias
    m = jnp.maximum(jnp.max(s, axis=-1, keepdims=True), sink_col)
    p = jnp.exp(s - m)
    den = jnp.sum(p, axis=-1, keepdims=True) + jnp.exp(sink_col - m)
    o = jnp.dot(p.astype(BF16), v_all.astype(BF16), preferred_element_type=F32)
    return o / den


def _ctx_attn_kernel(sink_ref, q_ref, k_ref, v_ref, o_ref):
    t = q_ref.shape[0]
    rows = lax.broadcasted_iota(jnp.int32, (ATT_GROUP * t, 1), 0)
    for g in range(ATT_KV_HEADS):
        kg = k_ref[:, ATT_HD * g:ATT_HD * (g + 1)]
        vg = v_ref[:, ATT_HD * g:ATT_HD * (g + 1)]
        qs, sink_col = [], jnp.zeros((ATT_GROUP * t, 1), F32)
        for i in range(ATT_GROUP):
            hh = g * ATT_GROUP + i
            qs.append(q_ref[:, ATT_HD * hh:ATT_HD * (hh + 1)] * (ATT_HD ** -0.5))
            sink_col = jnp.where(rows // t == i, sink_ref[hh], sink_col)
        o = _softmax_sink_heads(jnp.concatenate(qs, axis=0), kg, vg, None, sink_col)
        for i in range(ATT_GROUP):
            hh = g * ATT_GROUP + i
            o_ref[:, ATT_HD * hh:ATT_HD * (hh + 1)] = o[i * t:(i + 1) * t]


def _ctx_attention(p2d, sink, nseq, seq_len):
    nt = p2d.shape[0]
    return pl.pallas_call(
        _ctx_attn_kernel,
        grid=(nseq,),
        in_specs=[pl.BlockSpec(memory_space=pltpu.SMEM),
                  pl.BlockSpec((seq_len, 1024), lambda b: (b, P_AQ // 1024)),
                  pl.BlockSpec((seq_len, 256), lambda b: (b, P_AK // 256)),
                  pl.BlockSpec((seq_len, 256), lambda b: (b, P_AV // 256))],
        out_specs=pl.BlockSpec((seq_len, 1024), lambda b: (b, 0)),
        out_shape=jax.ShapeDtypeStruct((nt, 1024), F32),
        compiler_params=_cparams("arbitrary"),
        name="context_attention",
    )(sink, p2d, p2d, p2d)


def _rope(x, cos_t, sin_t):
    even = lax.broadcasted_iota(jnp.int32, x.shape, 1) % 2 == 0
    swapped = jnp.where(even, pltpu.roll(x, LANE - 1, 1), pltpu.roll(x, 1, 1))
    return x * cos_t + swapped * sin_t


def _lat_attn_kernel(sink_ref, q_ref, kp_ref, kc_ref, kn_ref, vp_ref, vc_ref, vn_ref,
                     ck_ref, cv_ref, cq_ref, sq_ref, cp_ref, sp_ref, cn_ref, sn_ref, o_ref):
    n = pl.program_id(1)
    nb = pl.num_programs(1)
    t = ATT_BLOCK
    kblocks = []
    for kref, c_ref, s_ref in ((kp_ref, cp_ref, sp_ref), (kc_ref, cq_ref, sq_ref), (kn_ref, cn_ref, sn_ref)):
        kblocks.append(jnp.concatenate(
            [_rope(kref[:, LANE * c:LANE * (c + 1)], c_ref[...], s_ref[...]) for c in range(2)], axis=1))
    k_loc = jnp.concatenate(kblocks, axis=0)
    v_loc = jnp.concatenate([vp_ref[...], vc_ref[...], vn_ref[...]], axis=0)
    k_all = jnp.concatenate([k_loc, ck_ref[...]], axis=0)
    v_all = jnp.concatenate([v_loc, cv_ref[...]], axis=0)
    tc = ck_ref.shape[0]
    qi = lax.broadcasted_iota(jnp.int32, (t, 3 * t + tc), 0)
    kj = lax.broadcasted_iota(jnp.int32, (t, 3 * t + tc), 1)
    ninf = jnp.float32(-jnp.inf)
    prev_bias = jnp.where(n > 0, jnp.float32(0.0), ninf)
    next_bias = jnp.where(n < nb - 1, jnp.float32(0.0), ninf)
    bias1 = jnp.where(kj < t, jnp.where(kj >= qi, prev_bias, ninf),
                      jnp.where(kj < 2 * t, 0.0,
                                jnp.where(kj < 3 * t, jnp.where(kj - 2 * t <= qi, next_bias, ninf), 0.0)))
    bias = jnp.concatenate([bias1] * ATT_GROUP, axis=0)
    rows = lax.broadcasted_iota(jnp.int32, (ATT_GROUP * t, 1), 0)
    qr = [_rope(q_ref[:, LANE * c:LANE * (c + 1)], cq_ref[...], sq_ref[...]) * (ATT_HD ** -0.5)
          for c in range(ATT_HEADS * ATT_HD // LANE)]
    for g in range(ATT_KV_HEADS):
        qs, sink_col = [], jnp.zeros((ATT_GROUP * t, 1), F32)
        for i in range(ATT_GROUP):
            hh = g * ATT_GROUP + i
            slab = qr[hh // 2]
            qs.append(slab[:, ATT_HD * (hh % 2):ATT_HD * (hh % 2 + 1)])
            sink_col = jnp.where(rows // t == i, sink_ref[hh], sink_col)
        kg = k_all[:, ATT_HD * g:ATT_HD * (g + 1)]
        vg = v_all[:, ATT_HD * g:ATT_HD * (g + 1)]
        o = _softmax_sink_heads(jnp.concatenate(qs, axis=0), kg, vg, bias, sink_col)
        for i in range(ATT_GROUP):
            hh = g * ATT_GROUP + i
            o_ref[:, ATT_HD * hh:ATT_HD * (hh + 1)] = o[i * t:(i + 1) * t]


def _lat_attention(p2d, sink, cache_k, cache_v, cos_t, sin_t, nseq, seq_len):
    nt = p2d.shape[0]
    nb = seq_len // ATT_BLOCK
    tc = cache_k.shape[1]
    cur = lambda b, n: b * nb + n
    prv = lambda b, n: b * nb + jnp.maximum(n - 1, 0)
    nxt = lambda b, n: b * nb + jnp.minimum(n + 1, nb - 1)
    kv = lambda f, col: pl.BlockSpec((ATT_BLOCK, 256), lambda b, n: (f(b, n), col))
    tab = lambda f: pl.BlockSpec((ATT_BLOCK, LANE), lambda b, n: (f(0, n), 0))
    return pl.pallas_call(
        _lat_attn_kernel,
        grid=(nseq, nb),
        in_specs=[pl.BlockSpec(memory_space=pltpu.SMEM),
                  pl.BlockSpec((ATT_BLOCK, 1024), lambda b, n: (cur(b, n), P_AQ // 1024)),
                  kv(prv, P_AK // 256), kv(cur, P_AK // 256), kv(nxt, P_AK // 256),
                  kv(prv, P_AV // 256), kv(cur, P_AV // 256), kv(nxt, P_AV // 256),
                  pl.BlockSpec((None, tc, 256), lambda b, n: (b, 0, 0)),
                  pl.BlockSpec((None, tc, 256), lambda b, n: (b, 0, 0)),
                  tab(cur), tab(cur), tab(prv), tab(prv), tab(nxt), tab(nxt)],
        out_specs=pl.BlockSpec((ATT_BLOCK, 1024), lambda b, n: (cur(b, n), 0)),
        out_shape=jax.ShapeDtypeStruct((nt, 1024), F32),
        compiler_params=_cparams("arbitrary", "arbitrary"),
        name="latent_window_attention",
    )(sink, p2d, p2d, p2d, p2d, p2d, p2d, p2d, cache_k, cache_v,
      cos_t, sin_t, cos_t, sin_t, cos_t, sin_t)


def _rope_tables(seq_len):
    rows = seq_len // GRID_W
    row = jnp.repeat(jnp.arange(rows, dtype=F32), GRID_W)
    col = jnp.tile(jnp.arange(GRID_W, dtype=F32), rows)
    half = ATT_HD // 2
    inv = ROPE_THETA ** (-jnp.arange(0, half, 2, dtype=F32) / half)
    ang = jnp.concatenate([row[:, None] * inv, col[:, None] * inv], -1)
    cos, sin = jnp.cos(ang), jnp.sin(ang)
    cos_t = jnp.tile(jnp.repeat(cos, 2, axis=1), (1, LANE // ATT_HD))
    sign = jnp.tile(jnp.asarray([-1.0, 1.0], F32), ATT_HD // 2)
    sin_t = jnp.tile(jnp.repeat(sin, 2, axis=1) * sign, (1, LANE // ATT_HD))
    return cos_t, sin_t


def _merge_kernel(tiles_per_seq, of_ref, ob_ref, gg_ref, ch_ref, cb_ref, cc_ref,
                  chp_ref, ccp_ref, chn_ref, ccn_ref, yc_ref, mga_ref, mgb_ref, mgc_ref, x_ref, g1_ref,
                  gn_ref, cw_ref, wb_ref, wo_ref, lg_ref, lb_ref, o_ref):
    i = pl.program_id(0)
    tm = x_ref.shape[0]
    gn = gn_ref[...]
    ya = []
    for h in range(GLA_HEADS):
        vc = slice(GLA_DV * h, GLA_DV * (h + 1))
        o = of_ref[:, vc] + ob_ref[:, vc]
        o = o * lax.rsqrt(jnp.mean(o * o, axis=-1, keepdims=True) + LN_EPS) * gn
        gg = gg_ref[:, vc]
        ya.append(o * (gg * jax.nn.sigmoid(gg)))
    ya = jnp.concatenate(ya, axis=1)
    z = cc_ref[...] * ch_ref[...]
    has_prev = (i % tiles_per_seq) != 0
    has_next = (i % tiles_per_seq) != tiles_per_seq - 1
    z_before = jnp.where(has_prev, ccp_ref[SUBLANE - 1:SUBLANE, :] * chp_ref[SUBLANE - 1:SUBLANE, :], 0.0)
    z_after = jnp.where(has_next, ccn_ref[0:1, :] * chn_ref[0:1, :], 0.0)
    r = lax.broadcasted_iota(jnp.int32, z.shape, 0)
    zp = jnp.where(r == 0, z_before, pltpu.roll(z, 1, 0))
    zn = jnp.where(r == tm - 1, z_after, pltpu.roll(z, tm - 1, 0))
    yb = cb_ref[...] * (cw_ref[0:1, :] * zp + cw_ref[1:2, :] * z + cw_ref[2:3, :] * zn)
    merged = jnp.zeros((tm, D_MODEL), F32)
    for bi, (y, mg_ref) in enumerate(((ya, mga_ref), (yb, mgb_ref), (yc_ref[...], mgc_ref))):
        proj = jnp.dot(y.astype(BF16), wb_ref[bi], preferred_element_type=F32)
        merged = merged + jax.nn.sigmoid(mg_ref[...]) * proj
    mix = jnp.dot(merged.astype(BF16), wo_ref[...], preferred_element_type=F32)
    o_ref[...] = _layer_norm(DN_ALPHA * x_ref[...] + g1_ref[...] * mix, lg_ref[...], lb_ref[...])


def _merge(x2d, p2d, o_f, o_b, y_c, mod3, gn, conv_w, wb, wo, ln_g, ln_b, seq_len, lat):
    nt = x2d.shape[0]
    tps = seq_len // TM
    row = (lambda i: 1 + i // tps) if lat else (lambda i: 0)
    col = lambda c: pl.BlockSpec((TM, 1024), lambda i: (i, c // 1024))
    halo_rows = TM // SUBLANE
    nhalo = nt // SUBLANE
    prev = lambda c: pl.BlockSpec((SUBLANE, 1024), lambda i: (jnp.maximum(i * halo_rows - 1, 0), c // 1024))
    nxt = lambda c: pl.BlockSpec((SUBLANE, 1024),
                                 lambda i: (jnp.minimum((i + 1) * halo_rows, nhalo - 1), c // 1024))
    full = lambda shape: pl.BlockSpec(shape, lambda i: (0,) * len(shape))
    return pl.pallas_call(
        functools.partial(_merge_kernel, tps),
        grid=(nt // TM,),
        in_specs=[pl.BlockSpec((TM, 1024), lambda i: (i, 0)), pl.BlockSpec((TM, 1024), lambda i: (i, 0)),
                  col(P_GG), col(P_CH), col(P_CB), col(P_CC),
                  prev(P_CH), prev(P_CC), nxt(P_CH), nxt(P_CC),
                  pl.BlockSpec((TM, 1024), lambda i: (i, 0)),
                  col(P_MG), col(P_MG + 1024), col(P_MG + 2048),
                  pl.BlockSpec((TM, 1024), lambda i: (i, 0)),
                  _mod_spec(2, row),
                  full((1, GLA_DV)), full((SUBLANE, 1024)), full((3, 1024, 1024)), full((1024, 1024)),
                  full((1, 1024)), full((1, 1024))],
        out_specs=pl.BlockSpec((TM, 1024), lambda i: (i, 0)),
        out_shape=jax.ShapeDtypeStruct((nt, 1024), F32),
        compiler_params=_cparams("arbitrary"),
        name="merge_ln1",
    )(o_f, o_b, p2d, p2d, p2d, p2d, p2d, p2d, p2d, p2d, y_c, p2d, p2d, p2d, x2d, mod3,
      gn, conv_w, wb, wo, ln_g, ln_b)


def _sorting_network(n):
    pairs = []
    p = 1
    while p < n:
        k = p
        while k >= 1:
            for j in range(k % p, n - k, 2 * k):
                for i in range(min(k, n - j - k)):
                    if (i + j) // (2 * p) == (i + j + k) // (2 * p):
                        pairs.append((i + j, i + j + k))
            k //= 2
        p *= 2
    return pairs


def _topk_rows(s, kk):
    n, m = s.shape
    nslab = n // SUBLANE
    assert nslab == kk
    base = lax.broadcasted_iota(jnp.int32, (SUBLANE, m), 0).astype(F32)
    val = [s[SUBLANE * r:SUBLANE * (r + 1)] for r in range(nslab)]
    idx = [base + float(SUBLANE * r) for r in range(nslab)]
    for a, b in _sorting_network(nslab):
        keep = val[a] >= val[b]
        val[a], val[b] = jnp.where(keep, val[a], val[b]), jnp.where(keep, val[b], val[a])
        idx[a], idx[b] = jnp.where(keep, idx[a], idx[b]), jnp.where(keep, idx[b], idx[a])
    vals, idxs = [], []
    for r in range(kk):
        top = jnp.max(val[0], axis=0, keepdims=True)
        at = jnp.min(jnp.where(val[0] == top, idx[0], float(n)), axis=0, keepdims=True)
        vals.append(top)
        idxs.append(at)
        pop = idx[0] == at
        depth = kk - 1 - r
        for d in range(depth):
            val[d] = jnp.where(pop, val[d + 1], val[d])
            idx[d] = jnp.where(pop, idx[d + 1], idx[d])
    return jnp.concatenate(vals, axis=0), jnp.concatenate(idxs, axis=0)


def _staircase():
    return [(i, PEER_TOPK // (i + 1)) for i in range(PEER_TOPK)]


def _peer_topk_kernel(x_ref, sh_ref, sc_ref, w_ref, keys_ref, pos_ref, eidx_ref, gate_ref):
    h2 = x_ref[...] * (1.0 + sc_ref[...]) + sh_ref[...]
    pq = jnp.dot(h2.astype(BF16), w_ref[...], preferred_element_type=F32)
    half = PEER_DQ // 2
    tops = []
    for p in range(2):
        s = _nt_dot(keys_ref[p], pq[:, half * p:half * (p + 1)].astype(BF16))
        tops.append(_topk_rows(s, PEER_TOPK))
    (s1, i1), (s2, i2) = tops
    tm = s1.shape[1]
    kk = PEER_TOPK
    pos = pos_ref[...]
    npad = pos.shape[0] - sum(nj for _, nj in _staircase())
    cand = jnp.concatenate([s1[i:i + 1] + s2[0:nj] for i, nj in _staircase()]
                           + [jnp.full((npad, tm), -jnp.inf, F32)], axis=0)
    cidx = jnp.concatenate([i1[i:i + 1] * float(PEER_NKEYS) + i2[0:nj] for i, nj in _staircase()]
                           + [jnp.zeros((npad, tm), F32)], axis=0)
    top_s, top_e = [], []
    for _ in range(kk):
        m = jnp.max(cand, axis=0, keepdims=True)
        at = jnp.min(jnp.where(cand == m, pos, float(2 * kk * kk)), axis=0, keepdims=True)
        hit = pos == at
        top_s.append(m)
        top_e.append(jnp.sum(jnp.where(hit, cidx, 0.0), axis=0, keepdims=True))
        cand = jnp.where(hit, -jnp.inf, cand)
    top_s = jnp.concatenate(top_s, axis=0)
    e = jnp.exp(top_s - top_s[0:1])
    gate_ref[...] = e / jnp.sum(e, axis=0, keepdims=True)
    eidx_ref[...] = jnp.concatenate(top_e, axis=0).astype(jnp.int32)


def _peer_topk(x2d, mod3, w_pq, keys, seq_len, lat):
    nt = x2d.shape[0]
    tps = seq_len // TM
    row = (lambda i, h: 1 + i // tps) if lat else (lambda i, h: 0)
    flat = [i * PEER_TOPK + j for i, nj in _staircase() for j in range(nj)]
    nrows = -(-len(flat) // SUBLANE) * SUBLANE
    flat = flat + [PEER_TOPK * PEER_TOPK] * (nrows - len(flat))
    pos = jnp.asarray(np.repeat(np.asarray(flat, np.float32)[:, None], TM, axis=1))
    return pl.pallas_call(
        _peer_topk_kernel,
        grid=(nt // TM, PEER_HEADS),
        in_specs=[pl.BlockSpec((TM, 1024), lambda i, h: (i, 0)),
                  _mod_spec(3, row), _mod_spec(4, row),
                  pl.BlockSpec((1024, PEER_DQ), lambda i, h: (0, h)),
                  pl.BlockSpec((None, 2, PEER_NKEYS, PEER_DQ // 2), lambda i, h: (h, 0, 0, 0)),
                  pl.BlockSpec((nrows, TM), lambda i, h: (0, 0))],
        out_specs=[pl.BlockSpec((None, PEER_TOPK, TM), lambda i, h: (h, 0, i)),
                   pl.BlockSpec((None, PEER_TOPK, TM), lambda i, h: (h, 0, i))],
        out_shape=[jax.ShapeDtypeStruct((PEER_HEADS, PEER_TOPK, nt), jnp.int32),
                   jax.ShapeDtypeStruct((PEER_HEADS, PEER_TOPK, nt), F32)],
        compiler_params=_cparams("arbitrary", "arbitrary"),
        name="peer_topk",
    )(x2d, mod3, mod3, w_pq, keys, pos)


def _peer_expert_kernel(eidx_ref, x_ref, sh_ref, sc_ref, g2_ref, gate_ref, lg_ref, lb_ref, col_ref, exp_ref,
                        uv_hbm, o_ref, *scratch):
    npair = PEER_HEADS * PEER_TOPK
    ngroup = PEER_TT // PEER_GROUP
    nrow = npair * SUBLANE
    blocks, (sems, ffn_ref) = scratch[:PEER_RING], scratch[PEER_RING:]

    def ring(slot):
        return blocks[slot // PEER_GROUP].at[slot % PEER_GROUP]

    def slot_copy(slot):
        return pltpu.make_async_copy(uv_hbm.at[pl.ds(0, npair)], ring(slot), sems.at[slot])

    def issue_token(tok, slot):
        for r in range(npair):
            pltpu.make_async_copy(uv_hbm.at[eidx_ref[tok, r]], ring(slot).at[r],
                                  sems.at[slot]).start(priority=r % 2)

    @pl.when(pl.program_id(0) == 0)
    def _():
        for t in range(PEER_AHEAD * PEER_GROUP):
            issue_token(t, t)

    rowid = lax.broadcasted_iota(jnp.int32, (PEER_GROUP, 1), 0)
    own = (lax.broadcasted_iota(jnp.int32, (SUBLANE, nrow), 1) % SUBLANE
           == lax.broadcasted_iota(jnp.int32, (SUBLANE, nrow), 0))
    sc, sh = sc_ref[...], sh_ref[...]

    def ring_round(k, carry):
        for q in range(PEER_RING):
            g = k * PEER_RING + q
            base = q * PEER_GROUP
            ahead = ((q + PEER_AHEAD) % PEER_RING) * PEER_GROUP
            r0 = pl.multiple_of(g * PEER_GROUP, PEER_GROUP)
            for jj in range(PEER_GROUP):
                slot_copy(base + jj).wait()
            z8 = jnp.zeros((PEER_GROUP, nrow), F32)
            for jj in range(PEER_GROUP):
                issue_token((g + PEER_AHEAD) * PEER_GROUP + jj, ahead + jj)
                h = (x_ref[r0 + jj] * (1.0 + sc) + sh).astype(BF16)
                u = ring(base + jj)[:, 0].reshape(nrow, LANE).astype(BF16)
                y = jnp.where(own, _nt_dot(h, u), 0.0)
                z8 = jnp.where(rowid == jj, jnp.sum(y, axis=0, keepdims=True), z8)
            z_hi = z8.astype(BF16)
            z_lo = (z8 - z_hi.astype(F32)).astype(BF16)
            s8 = (jnp.dot(z_hi, col_ref[...], preferred_element_type=F32)
                  + jnp.dot(z_lo, col_ref[...], preferred_element_type=F32))
            act = 0.5 * s8 * (1.0 + lax.erf(s8 * (2.0 ** -0.5)))
            w8 = (gate_ref[pl.ds(r0, PEER_GROUP), :] * act).astype(BF16)
            wexp = jnp.dot(w8, exp_ref[...], preferred_element_type=F32)
            for jj in range(PEER_GROUP):
                wsel = jnp.where(own, wexp[jj:jj + 1, :], 0.0).astype(BF16)
                v = ring(base + jj)[:, 1].reshape(nrow, LANE).astype(BF16)
                ffn_ref[r0 + jj] = jnp.dot(wsel, v, preferred_element_type=F32)
        return carry

    lax.fori_loop(0, ngroup // PEER_RING, ring_round, 0)

    @pl.when(pl.program_id(0) == pl.num_programs(0) - 1)
    def _():
        for t in range(PEER_AHEAD * PEER_GROUP):
            slot_copy(t).wait()

    y = DN_ALPHA * x_ref[...] + g2_ref[...] * ffn_ref[...]
    mu = jnp.sum(jnp.sum(y, axis=2, keepdims=True), axis=1, keepdims=True) * (1.0 / D_MODEL)
    yc = y - mu
    var = jnp.sum(jnp.sum(yc * yc, axis=2, keepdims=True), axis=1, keepdims=True) * (1.0 / D_MODEL)
    o_ref[...] = yc * lax.rsqrt(var + LN_EPS) * lg_ref[...] + lb_ref[...]


def _peer_experts(x2d, eidx, gate, mod, uv, ln_g, ln_b, seq_len, lat):
    nt = x2d.shape[0]
    tps = seq_len // PEER_TT
    row = (lambda i: 1 + i // tps) if lat else (lambda i: 0)
    npair = PEER_HEADS * PEER_TOPK
    nrow = npair * SUBLANE
    slab = (SUBLANE, LANE)
    mod4 = mod.reshape(8, 6, *slab)
    mod_spec = lambda piece: pl.BlockSpec((None, None) + slab, lambda i: (row(i), piece, 0, 0))
    full = lambda shape: pl.BlockSpec(shape, lambda i: (0,) * len(shape))
    collapse = jnp.asarray(np.arange(nrow)[:, None] // SUBLANE == np.arange(npair)[None, :], BF16)
    ntile = nt // PEER_TT
    nahead = PEER_AHEAD * PEER_GROUP
    tiles = eidx.reshape(ntile, PEER_TT, npair)
    nxt = jnp.concatenate([tiles[1:, :nahead], tiles[-1:, :nahead]], axis=0)
    eidx_ext = jnp.concatenate([tiles, nxt], axis=1)
    assert (PEER_TT // PEER_GROUP) % PEER_RING == 0 and PEER_AHEAD < PEER_RING
    out = pl.pallas_call(
        _peer_expert_kernel,
        grid=(ntile,),
        in_specs=[pl.BlockSpec((None, PEER_TT + nahead, npair), lambda i: (i, 0, 0), memory_space=pltpu.SMEM),
                  pl.BlockSpec((PEER_TT,) + slab, lambda i: (i, 0, 0)),
                  mod_spec(3), mod_spec(4), mod_spec(5),
                  pl.BlockSpec((PEER_TT, npair), lambda i: (i, 0)),
                  full(slab), full(slab), full((nrow, npair)), full((npair, nrow)),
                  pl.BlockSpec(memory_space=pl.ANY)],
        out_specs=pl.BlockSpec((PEER_TT,) + slab, lambda i: (i, 0, 0)),
        out_shape=jax.ShapeDtypeStruct((nt,) + slab, F32),
        scratch_shapes=[pltpu.VMEM((PEER_GROUP, npair, 2) + slab, F32)] * PEER_RING
                       + [pltpu.SemaphoreType.DMA((PEER_SLOTS,)),
                        pltpu.VMEM((PEER_TT,) + slab, F32)],
        compiler_params=_cparams("arbitrary"),
        name="peer_experts_ln2",
    )(eidx_ext, x2d.reshape((nt,) + slab), mod4, mod4, mod4, gate, ln_g.reshape(slab), ln_b.reshape(slab),
      collapse, collapse.T, uv)
    return out.reshape(nt, D_MODEL)


def _layer(x2d, mod3, lw, nseq, seq_len, lat, s0_f, s0_b, ctx):
    p2d = _inproj(x2d, mod3, lw["w_in"], seq_len, lat)
    o_f, s_f = _gla(p2d, s0_f, lw["wa_f"], lw["ba_f"], nseq, seq_len, rev=False)
    o_b, s_b = _gla(p2d, s0_b, lw["wa_b"], lw["ba_b"], nseq, seq_len, rev=True)
    if lat:
        y_c = _lat_attention(p2d, lw["sink"], ctx["k"], ctx["v"], ctx["cos"], ctx["sin"], nseq, seq_len)
    else:
        y_c = _ctx_attention(p2d, lw["sink"], nseq, seq_len)
    x1 = _merge(x2d, p2d, o_f, o_b, y_c, mod3, lw["gn"], lw["conv_w"], lw["wb"], lw["wo"],
                lw["ln1_g"], lw["ln1_b"], seq_len, lat)
    eidx_t, gate_t = _peer_topk(x1, mod3, lw["w_pq"], lw["keys"], seq_len, lat)
    nt = x2d.shape[0]
    eidx = eidx_t.reshape(PEER_HEADS * PEER_TOPK, nt).T
    gate = gate_t.reshape(PEER_HEADS * PEER_TOPK, nt).T
    x2 = _peer_experts(x1, eidx, gate, mod3, lw["uv"], lw["ln2_g"], lw["ln2_b"], seq_len, lat)
    return x2, p2d, s_f, s_b


def _layer_weights(l, w_in, w_gla_a2, b_gla_a, gla_norm_g, conv_w, attn_sink, w_branch, w_out,
                   ln1_g, ln1_b, w_pq, peer_keys, peer_u, peer_v, ln2_g, ln2_b):
    w = w_in[l]
    w_in_p = jnp.concatenate(
        [w[:, O_GQ:O_GA], w[:, O_CH:O_AK], w[:, O_MG:N_IN], w[:, O_AK:O_MG], w[:, O_GA:O_CH],
         jnp.zeros((D_MODEL, N_P - N_IN), F32)], axis=1).astype(BF16)

    def wa_pad(d):
        return jnp.zeros((LANE, 512), F32).at[GLA_LR * d:GLA_LR * (d + 1)].set(w_gla_a2[l, d])

    return {
        "w_in": w_in_p,
        "wa_f": wa_pad(0), "wa_b": wa_pad(1),
        "ba_f": b_gla_a[l, 0].reshape(1, 512), "ba_b": b_gla_a[l, 1].reshape(1, 512),
        "gn": gla_norm_g[l].reshape(1, GLA_DV),
        "conv_w": jnp.zeros((SUBLANE, 1024), F32).at[0:CONV_K].set(conv_w[l]),
        "sink": attn_sink[l],
        "wb": w_branch[l].astype(BF16), "wo": w_out[l].astype(BF16),
        "ln1_g": ln1_g[l].reshape(1, 1024), "ln1_b": ln1_b[l].reshape(1, 1024),
        "w_pq": w_pq[l].astype(BF16),
        "keys": peer_keys[l].astype(BF16),
        "uv": jnp.stack([peer_u[l].reshape(PEER_EXPERTS, SUBLANE, LANE),
                         peer_v[l].reshape(PEER_EXPERTS, SUBLANE, LANE)], axis=1),
        "ln2_g": ln2_g[l].reshape(1, 1024), "ln2_b": ln2_b[l].reshape(1, 1024),
    }


def kernel(x_prompt, x_sample, cache_k, cache_v, state_gla, c, c_ctx, ln_in_g, ln_in_b, w_mod, b_mod, w_in, w_gla_a2, b_gla_a, gla_norm_g, conv_w, attn_sink, w_branch, w_out, ln1_g, ln1_b, w_pq, peer_keys, peer_u, peer_v, ln2_g, ln2_b):
    batch, seq, _ = x_prompt.shape
    dec_batch, dec_seq, _ = x_sample.shape
    past = cache_k.shape[2]
    assert dec_batch + 1 <= 8 and seq % TM == 0 and dec_seq % TM_IN == 0 and seq % GLA_BLK == 0

    cond8 = jnp.zeros((8, D_MODEL), F32).at[0].set(c_ctx).at[1:1 + dec_batch].set(c)
    mod = _modulation(cond8, w_mod, b_mod)
    cos_t, sin_t = _rope_tables(dec_seq)

    xp = _ln_in(x_prompt.reshape(batch * seq, D_MODEL), ln_in_g, ln_in_b)
    xs = _ln_in(x_sample.reshape(dec_batch * dec_seq, D_MODEL), ln_in_g, ln_in_b)
    zeros_state = jnp.zeros((batch, GLA_HEADS, GLA_DK, GLA_DV), F32)
    ks, vs, ss = [], [], []
    for l in range(DEPTH):
        lw = _layer_weights(l, w_in, w_gla_a2, b_gla_a, gla_norm_g, conv_w, attn_sink, w_branch, w_out,
                            ln1_g, ln1_b, w_pq, peer_keys, peer_u, peer_v, ln2_g, ln2_b)
        mod3 = mod[l].reshape(8, 1, 6 * D_MODEL)
        xp, p_ctx, s_f, s_b = _layer(xp, mod3, lw, batch, seq, False, zeros_state, zeros_state, None)
        ks.append(p_ctx[:, P_AK:P_AK + 256].reshape(batch, seq, ATT_KV_HEADS, ATT_HD))
        vs.append(p_ctx[:, P_AV:P_AV + 256].reshape(batch, seq, ATT_KV_HEADS, ATT_HD))
        ss.append(jnp.stack([s_f, s_b], axis=1))
        ctx = {"k": cache_k[:, l].reshape(dec_batch, past, ATT_KV_HEADS * ATT_HD),
               "v": cache_v[:, l].reshape(dec_batch, past, ATT_KV_HEADS * ATT_HD),
               "cos": cos_t, "sin": sin_t}
        xs, _, _, _ = _layer(xs, mod3, lw, dec_batch, dec_seq, True,
                             state_gla[:, l, 0], state_gla[:, l, 1], ctx)
    return (xp.reshape(batch, seq, D_MODEL), xs.reshape(dec_batch, dec_seq, D_MODEL),
            jnp.stack(ks, axis=1), jnp.stack(vs, axis=1), jnp.stack(ss, axis=1))
```

```python
import functools

import numpy as np
import jax
import jax.numpy as jnp
from jax import lax
from jax.experimental import pallas as pl
from jax.experimental.pallas import tpu as pltpu

F32 = jnp.float32
BF16 = jnp.bfloat16
HIGHEST = lax.Precision.HIGHEST

D_MODEL = 1024
DEPTH = 2
GRID_W = 64
LN_EPS = 1e-6
DN_ALPHA = float((2 * DEPTH) ** 0.25)

GLA_HEADS = 4
GLA_DK = 128
GLA_DV = 256
GLA_LR = 16
GLA_TAU = 16.0
CONV_K = 3
ATT_HD = 64
ATT_HEADS = 16
ATT_KV_HEADS = 4
ATT_GROUP = 4
ATT_BLOCK = 128
ROPE_THETA = 10000.0
PEER_HEADS = 8
PEER_NKEYS = 128
PEER_EXPERTS = PEER_NKEYS * PEER_NKEYS
PEER_TOPK = 16
PEER_DQ = 256

O_GQ, O_GK, O_GV, O_GG, O_GA = 0, 512, 1024, 2048, 3072
O_CH, O_CB, O_CC, O_AQ, O_AK, O_AV, O_MG = 3104, 4128, 5152, 6176, 7200, 7456, 7712
N_IN = 10784
P_GQ, P_GK, P_GV, P_GG = 0, 512, 1024, 2048
P_CH, P_CB, P_CC, P_AQ, P_MG = 3072, 4096, 5120, 6144, 7168
P_AK, P_AV, P_GA = 10240, 10496, 10752
N_P = 10880

LANE = 128
SUBLANE = 8
VMEM_LIMIT = 56 * 1024 * 1024

TM = 256
TM_IN = 1024
TN_IN = 2176
GLA_BLK = 256
GLA_LEVELS = 8
GLA_MM_LEVELS = 2
PEER_TOPK_HEADS = 2
PEER_TT = 128
PEER_GROUP = 8
PEER_RING = 4
PEER_AHEAD = 2
PEER_SLOTS = PEER_RING * PEER_GROUP


def _cparams(*sem):
    return pltpu.CompilerParams(dimension_semantics=sem, vmem_limit_bytes=VMEM_LIMIT)


def _layer_norm(x, g, b):
    mu = jnp.mean(x, axis=-1, keepdims=True)
    xc = x - mu
    var = jnp.mean(xc * xc, axis=-1, keepdims=True)
    return xc * lax.rsqrt(var + LN_EPS) * g + b


def _nt_dot(a, b):
    return lax.dot_general(a, b, (((1,), (1,)), ((), ())), preferred_element_type=F32)


def _mod_kernel(c_ref, w_ref, b_ref, o_ref):
    c = c_ref[...]
    s = c * jax.nn.sigmoid(c)
    o_ref[0] = jnp.dot(s, w_ref[0], precision=HIGHEST, preferred_element_type=F32) + b_ref[0]


def _modulation(cond8, w_mod, b_mod):
    tn = 1536
    return pl.pallas_call(
        _mod_kernel,
        grid=(DEPTH, 6 * D_MODEL // tn),
        in_specs=[pl.BlockSpec((8, D_MODEL), lambda l, j: (0, 0)),
                  pl.BlockSpec((1, D_MODEL, tn), lambda l, j: (l, 0, j)),
                  pl.BlockSpec((1, 1, tn), lambda l, j: (l, 0, j))],
        out_specs=pl.BlockSpec((1, 8, tn), lambda l, j: (l, 0, j)),
        out_shape=jax.ShapeDtypeStruct((DEPTH, 8, 6 * D_MODEL), F32),
        compiler_params=_cparams("arbitrary", "arbitrary"),
        name="ada_modulation",
    )(cond8, w_mod, b_mod.reshape(DEPTH, 1, 6 * D_MODEL))


def _mod_spec(piece, row_fn):
    return pl.BlockSpec((None, 1, D_MODEL), lambda *ids: (row_fn(*ids), 0, piece))


def _ln_kernel(x_ref, g_ref, b_ref, o_ref):
    o_ref[...] = _layer_norm(x_ref[...], g_ref[...], b_ref[...])


def _ln_in(x2d, g, b):
    nt = x2d.shape[0]
    return pl.pallas_call(
        _ln_kernel,
        grid=(nt // TM_IN,),
        in_specs=[pl.BlockSpec((TM_IN, D_MODEL), lambda i: (i, 0)),
                  pl.BlockSpec((1, D_MODEL), lambda i: (0, 0)),
                  pl.BlockSpec((1, D_MODEL), lambda i: (0, 0))],
        out_specs=pl.BlockSpec((TM_IN, D_MODEL), lambda i: (i, 0)),
        out_shape=jax.ShapeDtypeStruct((nt, D_MODEL), F32),
        compiler_params=_cparams("arbitrary"),
        name="ln_in",
    )(x2d, g.reshape(1, D_MODEL), b.reshape(1, D_MODEL))


def _inproj_kernel(x_ref, sh_ref, sc_ref, w_ref, o_ref):
    h = x_ref[...] * (1.0 + sc_ref[...]) + sh_ref[...]
    o_ref[...] = jnp.dot(h.astype(BF16), w_ref[...], preferred_element_type=F32)


def _inproj(x2d, mod3, w_in_p, seq_len, lat):
    nt = x2d.shape[0]
    tiles_per_seq = max(seq_len // TM_IN, 1)
    row = (lambda j, i: 1 + i // tiles_per_seq) if lat else (lambda j, i: 0)
    return pl.pallas_call(
        _inproj_kernel,
        grid=(N_P // TN_IN, nt // TM_IN),
        in_specs=[pl.BlockSpec((TM_IN, D_MODEL), lambda j, i: (i, 0)),
                  _mod_spec(0, row), _mod_spec(1, row),
                  pl.BlockSpec((D_MODEL, TN_IN), lambda j, i: (0, j))],
        out_specs=pl.BlockSpec((TM_IN, TN_IN), lambda j, i: (i, j)),
        out_shape=jax.ShapeDtypeStruct((nt, N_P), F32),
        compiler_params=_cparams("arbitrary", "arbitrary"),
        name="in_projection",
    )(x2d, mod3, mod3, w_in_p)


def _gla_tables(rev):
    n = GLA_BLK
    idx = np.arange(n)
    if rev:
        cum = (idx[None, :] >= idx[:, None]).astype(np.float32)
    else:
        cum = (idx[None, :] <= idx[:, None]).astype(np.float32)
    rows = [cum]
    for k in range(GLA_MM_LEVELS):
        h = 1 << k
        rows.append(cum[(idx & ~(2 * h - 1)) + (h if rev else h - 1)])
    mall = np.concatenate(rows, axis=0)
    t, s = idx[:, None], idx[None, :]
    x = t ^ s
    top = np.where(x > 0, np.floor(np.log2(np.maximum(x, 1))).astype(np.int64) + 1, 0)
    attend = (s >= t) if rev else (s <= t)
    lv = np.where(attend, top, -1).astype(np.int32)
    return jnp.asarray(mall, BF16), jnp.asarray(lv)


def _gla_kernel(rev, q_ref, k_ref, v_ref, ga_ref, wa_ref, ba_ref, mall_ref, lv_ref, s0_ref,
                o_ref, sout_ref, st_ref, t_ref):
    j = pl.program_id(1)
    n = GLA_BLK

    @pl.when(j == 0)
    def _():
        for h in range(GLA_HEADS):
            st_ref[h] = s0_ref[0, h].T

    z = jnp.dot(ga_ref[...], wa_ref[...], precision=HIGHEST, preferred_element_type=F32) + ba_ref[...]
    a = (jnp.minimum(z, 0.0) - jnp.log(1.0 + jnp.exp(-jnp.abs(z)))) * (1.0 / GLA_TAU)
    a_hi = a.astype(BF16)
    r1 = a - a_hi.astype(F32)
    a_mid = r1.astype(BF16)
    a_lo = (r1 - a_mid.astype(F32)).astype(BF16)
    mall = mall_ref[...]
    t_ref[...] = (jnp.dot(mall, a_hi, preferred_element_type=F32)
                  + jnp.dot(mall, a_mid, preferred_element_type=F32)
                  + jnp.dot(mall, a_lo, preferred_element_type=F32))

    lv = lv_ref[...]
    for h in range(GLA_HEADS):
        kc = slice(GLA_DK * h, GLA_DK * (h + 1))
        vc = slice(GLA_DV * h, GLA_DV * (h + 1))
        b = t_ref[0:n, kc]
        btot = t_ref[0:1, kc] if rev else t_ref[n - 1:n, kc]
        q = q_ref[:, kc] * (GLA_DK ** -0.5)
        k = k_ref[:, kc]
        v = v_ref[:, vc]
        att = jnp.where(lv == 0, _nt_dot(q.astype(BF16), k.astype(BF16)), 0.0)
        for lev in range(GLA_LEVELS):
            if lev < GLA_MM_LEVELS:
                pv = t_ref[(lev + 1) * n:(lev + 2) * n, kc]
            else:
                hh = 1 << lev
                pv = jnp.concatenate(
                    [jnp.broadcast_to(t_ref[g0 + (hh if rev else hh - 1):g0 + (hh if rev else hh - 1) + 1, kc],
                                      (2 * hh, GLA_DK)) for g0 in range(0, n, 2 * hh)], axis=0)
            ql = (q * jnp.exp(jnp.minimum(b - pv, 0.0))).astype(BF16)
            kl = (k * jnp.exp(jnp.minimum(pv - b, 0.0))).astype(BF16)
            att = jnp.where(lv == lev + 1, _nt_dot(ql, kl), att)
        st = st_ref[h]
        qi = (q * jnp.exp(b)).astype(BF16)
        o = (jnp.dot(att.astype(BF16), v.astype(BF16), preferred_element_type=F32)
             + _nt_dot(qi, st.astype(BF16)))
        o_ref[:, vc] = o
        kh = (k * jnp.exp(btot - b)).astype(BF16)
        st_ref[h] = st * jnp.exp(btot) + jnp.dot(v.T.astype(BF16), kh, preferred_element_type=F32)

    @pl.when(j == pl.num_programs(1) - 1)
    def _():
        for h in range(GLA_HEADS):
            sout_ref[0, h] = st_ref[h].T


def _gla(p2d, s0, wa_pad, ba, nseq, seq_len, rev):
    nt = p2d.shape[0]
    nblk = seq_len // GLA_BLK
    mall, lv = _gla_tables(rev)

    def rb(b, j):
        return b * nblk + (nblk - 1 - j if rev else j)

    return pl.pallas_call(
        functools.partial(_gla_kernel, rev),
        grid=(nseq, nblk),
        in_specs=[pl.BlockSpec((GLA_BLK, 512), lambda b, j: (rb(b, j), P_GQ // 512)),
                  pl.BlockSpec((GLA_BLK, 512), lambda b, j: (rb(b, j), P_GK // 512)),
                  pl.BlockSpec((GLA_BLK, 1024), lambda b, j: (rb(b, j), P_GV // 1024)),
                  pl.BlockSpec((GLA_BLK, LANE), lambda b, j: (rb(b, j), P_GA // LANE)),
                  pl.BlockSpec((LANE, 512), lambda b, j: (0, 0)),
                  pl.BlockSpec((1, 512), lambda b, j: (0, 0)),
                  pl.BlockSpec(mall.shape, lambda b, j: (0, 0)),
                  pl.BlockSpec(lv.shape, lambda b, j: (0, 0)),
                  pl.BlockSpec((1, GLA_HEADS, GLA_DK, GLA_DV), lambda b, j: (b, 0, 0, 0))],
        out_specs=[pl.BlockSpec((GLA_BLK, 1024), lambda b, j: (rb(b, j), 0)),
                   pl.BlockSpec((1, GLA_HEADS, GLA_DK, GLA_DV), lambda b, j: (b, 0, 0, 0))],
        out_shape=[jax.ShapeDtypeStruct((nt, 1024), F32),
                   jax.ShapeDtypeStruct((nseq, GLA_HEADS, GLA_DK, GLA_DV), F32)],
        scratch_shapes=[pltpu.VMEM((GLA_HEADS, GLA_DV, GLA_DK), F32),
                        pltpu.VMEM(((GLA_MM_LEVELS + 1) * GLA_BLK, 512), F32)],
        compiler_params=_cparams("arbitrary", "arbitrary"),
        name="gla_bwd" if rev else "gla_fwd",
    )(p2d, p2d, p2d, p2d, wa_pad, ba, mall, lv, s0)


def _softmax_sink_heads(q_rows, k_all, v_all, bias, sink_col):
    s = _nt_dot(q_rows.astype(BF16), k_all.astype(BF16))
    if bias is not None:
        s = s + bias
    m = jnp.maximum(jnp.max(s, axis=-1, keepdims=True), sink_col)
    p = jnp.exp(s - m)
    den = jnp.sum(p, axis=-1, keepdims=True) + jnp.exp(sink_col - m)
    o = jnp.dot(p.astype(BF16), v_all.astype(BF16), preferred_element_type=F32)
    return o / den


def _ctx_attn_kernel(sink_ref, q_ref, k_ref, v_ref, o_ref):
    t = q_ref.shape[0]
    rows = lax.broadcasted_iota(jnp.int32, (ATT_GROUP * t, 1), 0)
    for g in range(ATT_KV_HEADS):
        kg = k_ref[:, ATT_HD * g:ATT_HD * (g + 1)]
        vg = v_ref[:, ATT_HD * g:ATT_HD * (g + 1)]
        qs, sink_col = [], jnp.zeros((ATT_GROUP * t, 1), F32)
        for i in range(ATT_GROUP):
            hh = g * ATT_GROUP + i
            qs.append(q_ref[:, ATT_HD * hh:ATT_HD * (hh + 1)] * (ATT_HD ** -0.5))
            sink_col = jnp.where(rows // t == i, sink_ref[hh], sink_col)
        o = _softmax_sink_heads(jnp.concatenate(qs, axis=0), kg, vg, None, sink_col)
        for i in range(ATT_GROUP):
            hh = g * ATT_GROUP + i
            o_ref[:, ATT_HD * hh:ATT_HD * (hh + 1)] = o[i * t:(i + 1) * t]


def _ctx_attention(p2d, sink, nseq, seq_len):
    nt = p2d.shape[0]
    return pl.pallas_call(
        _ctx_attn_kernel,
        grid=(nseq,),
        in_specs=[pl.BlockSpec(memory_space=pltpu.SMEM),
                  pl.BlockSpec((seq_len, 1024), lambda b: (b, P_AQ // 1024)),
                  pl.BlockSpec((seq_len, 256), lambda b: (b, P_AK // 256)),
                  pl.BlockSpec((seq_len, 256), lambda b: (b, P_AV // 256))],
        out_specs=pl.BlockSpec((seq_len, 1024), lambda b: (b, 0)),
        out_shape=jax.ShapeDtypeStruct((nt, 1024), F32),
        compiler_params=_cparams("arbitrary"),
        name="context_attention",
    )(sink, p2d, p2d, p2d)


def _rope(x, cos_t, sin_t):
    even = lax.broadcasted_iota(jnp.int32, x.shape, 1) % 2 == 0
    swapped = jnp.where(even, pltpu.roll(x, LANE - 1, 1), pltpu.roll(x, 1, 1))
    return x * cos_t + swapped * sin_t


def _lat_attn_kernel(sink_ref, q_ref, kp_ref, kc_ref, kn_ref, vp_ref, vc_ref, vn_ref,
                     ck_ref, cv_ref, cq_ref, sq_ref, cp_ref, sp_ref, cn_ref, sn_ref, o_ref):
    n = pl.program_id(1)
    nb = pl.num_programs(1)
    t = ATT_BLOCK
    kblocks = []
    for kref, c_ref, s_ref in ((kp_ref, cp_ref, sp_ref), (kc_ref, cq_ref, sq_ref), (kn_ref, cn_ref, sn_ref)):
        kblocks.append(jnp.concatenate(
            [_rope(kref[:, LANE * c:LANE * (c + 1)], c_ref[...], s_ref[...]) for c in range(2)], axis=1))
    k_loc = jnp.concatenate(kblocks, axis=0)
    v_loc = jnp.concatenate([vp_ref[...], vc_ref[...], vn_ref[...]], axis=0)
    k_all = jnp.concatenate([k_loc, ck_ref[...]], axis=0)
    v_all = jnp.concatenate([v_loc, cv_ref[...]], axis=0)
    tc = ck_ref.shape[0]
    qi = lax.broadcasted_iota(jnp.int32, (t, 3 * t + tc), 0)
    kj = lax.broadcasted_iota(jnp.int32, (t, 3 * t + tc), 1)
    ninf = jnp.float32(-jnp.inf)
    prev_bias = jnp.where(n > 0, jnp.float32(0.0), ninf)
    next_bias = jnp.where(n < nb - 1, jnp.float32(0.0), ninf)
    bias1 = jnp.where(kj < t, jnp.where(kj >= qi, prev_bias, ninf),
                      jnp.where(kj < 2 * t, 0.0,
                                jnp.where(kj < 3 * t, jnp.where(kj - 2 * t <= qi, next_bias, ninf), 0.0)))
    bias = jnp.concatenate([bias1] * ATT_GROUP, axis=0)
    rows = lax.broadcasted_iota(jnp.int32, (ATT_GROUP * t, 1), 0)
    qr = [_rope(q_ref[:, LANE * c:LANE * (c + 1)], cq_ref[...], sq_ref[...]) * (ATT_HD ** -0.5)
          for c in range(ATT_HEADS * ATT_HD // LANE)]
    for g in range(ATT_KV_HEADS):
        qs, sink_col = [], jnp.zeros((ATT_GROUP * t, 1), F32)
        for i in range(ATT_GROUP):
            hh = g * ATT_GROUP + i
            slab = qr[hh // 2]
            qs.append(slab[:, ATT_HD * (hh % 2):ATT_HD * (hh % 2 + 1)])
            sink_col = jnp.where(rows // t == i, sink_ref[hh], sink_col)
        kg = k_all[:, ATT_HD * g:ATT_HD * (g + 1)]
        vg = v_all[:, ATT_HD * g:ATT_HD * (g + 1)]
        o = _softmax_sink_heads(jnp.concatenate(qs, axis=0), kg, vg, bias, sink_col)
        for i in range(ATT_GROUP):
            hh = g * ATT_GROUP + i
            o_ref[:, ATT_HD * hh:ATT_HD * (hh + 1)] = o[i * t:(i + 1) * t]


def _lat_attention(p2d, sink, cache_k, cache_v, cos_t, sin_t, nseq, seq_len):
    nt = p2d.shape[0]
    nb = seq_len // ATT_BLOCK
    tc = cache_k.shape[1]
    cur = lambda b, n: b * nb + n
    prv = lambda b, n: b * nb + jnp.maximum(n - 1, 0)
    nxt = lambda b, n: b * nb + jnp.minimum(n + 1, nb - 1)
    kv = lambda f, col: pl.BlockSpec((ATT_BLOCK, 256), lambda b, n: (f(b, n), col))
    tab = lambda f: pl.BlockSpec((ATT_BLOCK, LANE), lambda b, n: (f(0, n), 0))
    return pl.pallas_call(
        _lat_attn_kernel,
        grid=(nseq, nb),
        in_specs=[pl.BlockSpec(memory_space=pltpu.SMEM),
                  pl.BlockSpec((ATT_BLOCK, 1024), lambda b, n: (cur(b, n), P_AQ // 1024)),
                  kv(prv, P_AK // 256), kv(cur, P_AK // 256), kv(nxt, P_AK // 256),
                  kv(prv, P_AV // 256), kv(cur, P_AV // 256), kv(nxt, P_AV // 256),
                  pl.BlockSpec((None, tc, 256), lambda b, n: (b, 0, 0)),
                  pl.BlockSpec((None, tc, 256), lambda b, n: (b, 0, 0)),
                  tab(cur), tab(cur), tab(prv), tab(prv), tab(nxt), tab(nxt)],
        out_specs=pl.BlockSpec((ATT_BLOCK, 1024), lambda b, n: (cur(b, n), 0)),
        out_shape=jax.ShapeDtypeStruct((nt, 1024), F32),
        compiler_params=_cparams("arbitrary", "arbitrary"),
        name="latent_window_attention",
    )(sink, p2d, p2d, p2d, p2d, p2d, p2d, p2d, cache_k, cache_v,
      cos_t, sin_t, cos_t, sin_t, cos_t, sin_t)


def _rope_tables(seq_len):
    rows = seq_len // GRID_W
    row = jnp.repeat(jnp.arange(rows, dtype=F32), GRID_W)
    col = jnp.tile(jnp.arange(GRID_W, dtype=F32), rows)
    half = ATT_HD // 2
    inv = ROPE_THETA ** (-jnp.arange(0, half, 2, dtype=F32) / half)
    ang = jnp.concatenate([row[:, None] * inv, col[:, None] * inv], -1)
    cos, sin = jnp.cos(ang), jnp.sin(ang)
    cos_t = jnp.tile(jnp.repeat(cos, 2, axis=1), (1, LANE // ATT_HD))
    sign = jnp.tile(jnp.asarray([-1.0, 1.0], F32), ATT_HD // 2)
    sin_t = jnp.tile(jnp.repeat(sin, 2, axis=1) * sign, (1, LANE // ATT_HD))
    return cos_t, sin_t


def _merge_kernel(tiles_per_seq, of_ref, ob_ref, gg_ref, ch_ref, cb_ref, cc_ref,
                  chp_ref, ccp_ref, chn_ref, ccn_ref, yc_ref, mga_ref, mgb_ref, mgc_ref, x_ref, g1_ref,
                  gn_ref, cw_ref, wb_ref, wo_ref, lg_ref, lb_ref, o_ref):
    i = pl.program_id(0)
    tm = x_ref.shape[0]
    gn = gn_ref[...]
    ya = []
    for h in range(GLA_HEADS):
        vc = slice(GLA_DV * h, GLA_DV * (h + 1))
        o = of_ref[:, vc] + ob_ref[:, vc]
        o = o * lax.rsqrt(jnp.mean(o * o, axis=-1, keepdims=True) + LN_EPS) * gn
        gg = gg_ref[:, vc]
        ya.append(o * (gg * jax.nn.sigmoid(gg)))
    ya = jnp.concatenate(ya, axis=1)
    z = cc_ref[...] * ch_ref[...]
    has_prev = (i % tiles_per_seq) != 0
    has_next = (i % tiles_per_seq) != tiles_per_seq - 1
    z_before = jnp.where(has_prev, ccp_ref[SUBLANE - 1:SUBLANE, :] * chp_ref[SUBLANE - 1:SUBLANE, :], 0.0)
    z_after = jnp.where(has_next, ccn_ref[0:1, :] * chn_ref[0:1, :], 0.0)
    r = lax.broadcasted_iota(jnp.int32, z.shape, 0)
    zp = jnp.where(r == 0, z_before, pltpu.roll(z, 1, 0))
    zn = jnp.where(r == tm - 1, z_after, pltpu.roll(z, tm - 1, 0))
    yb = cb_ref[...] * (cw_ref[0:1, :] * zp + cw_ref[1:2, :] * z + cw_ref[2:3, :] * zn)
    merged = jnp.zeros((tm, D_MODEL), F32)
    for bi, (y, mg_ref) in enumerate(((ya, mga_ref), (yb, mgb_ref), (yc_ref[...], mgc_ref))):
        proj = jnp.dot(y.astype(BF16), wb_ref[bi], preferred_element_type=F32)
        merged = merged + jax.nn.sigmoid(mg_ref[...]) * proj
    mix = jnp.dot(merged.astype(BF16), wo_ref[...], preferred_element_type=F32)
    o_ref[...] = _layer_norm(DN_ALPHA * x_ref[...] + g1_ref[...] * mix, lg_ref[...], lb_ref[...])


def _merge(x2d, p2d, o_f, o_b, y_c, mod3, gn, conv_w, wb, wo, ln_g, ln_b, seq_len, lat):
    nt = x2d.shape[0]
    tps = seq_len // TM
    row = (lambda i: 1 + i // tps) if lat else (lambda i: 0)
    col = lambda c: pl.BlockSpec((TM, 1024), lambda i: (i, c // 1024))
    halo_rows = TM // SUBLANE
    nhalo = nt // SUBLANE
    prev = lambda c: pl.BlockSpec((SUBLANE, 1024), lambda i: (jnp.maximum(i * halo_rows - 1, 0), c // 1024))
    nxt = lambda c: pl.BlockSpec((SUBLANE, 1024),
                                 lambda i: (jnp.minimum((i + 1) * halo_rows, nhalo - 1), c // 1024))
    full = lambda shape: pl.BlockSpec(shape, lambda i: (0,) * len(shape))
    return pl.pallas_call(
        functools.partial(_merge_kernel, tps),
        grid=(nt // TM,),
        in_specs=[pl.BlockSpec((TM, 1024), lambda i: (i, 0)), pl.BlockSpec((TM, 1024), lambda i: (i, 0)),
                  col(P_GG), col(P_CH), col(P_CB), col(P_CC),
                  prev(P_CH), prev(P_CC), nxt(P_CH), nxt(P_CC),
                  pl.BlockSpec((TM, 1024), lambda i: (i, 0)),
                  col(P_MG), col(P_MG + 1024), col(P_MG + 2048),
                  pl.BlockSpec((TM, 1024), lambda i: (i, 0)),
                  _mod_spec(2, row),
                  full((1, GLA_DV)), full((SUBLANE, 1024)), full((3, 1024, 1024)), full((1024, 1024)),
                  full((1, 1024)), full((1, 1024))],
        out_specs=pl.BlockSpec((TM, 1024), lambda i: (i, 0)),
        out_shape=jax.ShapeDtypeStruct((nt, 1024), F32),
        compiler_params=_cparams("arbitrary"),
        name="merge_ln1",
    )(o_f, o_b, p2d, p2d, p2d, p2d, p2d, p2d, p2d, p2d, y_c, p2d, p2d, p2d, x2d, mod3,
      gn, conv_w, wb, wo, ln_g, ln_b)


def _sorting_network(n):
    pairs = []
    p = 1
    while p < n:
        k = p
        while k >= 1:
            for j in range(k % p, n - k, 2 * k):
                for i in range(min(k, n - j - k)):
                    if (i + j) // (2 * p) == (i + j + k) // (2 * p):
                        pairs.append((i + j, i + j + k))
            k //= 2
        p *= 2
    return pairs


def _topk_rows(s, kk):
    n, m = s.shape
    nslab = n // SUBLANE
    assert nslab == kk
    base = lax.broadcasted_iota(jnp.int32, (SUBLANE, m), 0).astype(F32)
    val = [s[SUBLANE * r:SUBLANE * (r + 1)] for r in range(nslab)]
    idx = [base + float(SUBLANE * r) for r in range(nslab)]
    for a, b in _sorting_network(nslab):
        keep = val[a] >= val[b]
        val[a], val[b] = jnp.where(keep, val[a], val[b]), jnp.where(keep, val[b], val[a])
        idx[a], idx[b] = jnp.where(keep, idx[a], idx[b]), jnp.where(keep, idx[b], idx[a])
    vals, idxs = [], []
    for r in range(kk):
        top = jnp.max(val[0], axis=0, keepdims=True)
        at = jnp.min(jnp.where(val[0] == top, idx[0], float(n)), axis=0, keepdims=True)
        vals.append(top)
        idxs.append(at)
        pop = idx[0] == at
        depth = kk - 1 - r
        for d in range(depth):
            val[d] = jnp.where(pop, val[d + 1], val[d])
            idx[d] = jnp.where(pop, idx[d + 1], idx[d])
    return jnp.concatenate(vals, axis=0), jnp.concatenate(idxs, axis=0)


def _staircase():
    return [(i, PEER_TOPK // (i + 1)) for i in range(PEER_TOPK)]


def _peer_topk_kernel(x_ref, sh_ref, sc_ref, w_ref, keys_ref, pos_ref, eidx_ref, gate_ref):
    h2 = (x_ref[...] * (1.0 + sc_ref[...]) + sh_ref[...]).astype(BF16)
    for hh in range(PEER_TOPK_HEADS):
        pq = jnp.dot(h2, w_ref[:, PEER_DQ * hh:PEER_DQ * (hh + 1)], preferred_element_type=F32)
        eidx_ref[hh], gate_ref[hh] = _peer_topk_head(pq, keys_ref[hh], pos_ref[...])


def _peer_topk_head(pq, keys, pos):
    half = PEER_DQ // 2
    tops = []
    for p in range(2):
        s = _nt_dot(keys[p], pq[:, half * p:half * (p + 1)].astype(BF16))
        tops.append(_topk_rows(s, PEER_TOPK))
    (s1, i1), (s2, i2) = tops
    tm = s1.shape[1]
    kk = PEER_TOPK
    npad = pos.shape[0] - sum(nj for _, nj in _staircase())
    cand = jnp.concatenate([s1[i:i + 1] + s2[0:nj] for i, nj in _staircase()]
                           + [jnp.full((npad, tm), -jnp.inf, F32)], axis=0)
    cidx = jnp.concatenate([i1[i:i + 1] * float(PEER_NKEYS) + i2[0:nj] for i, nj in _staircase()]
                           + [jnp.zeros((npad, tm), F32)], axis=0)
    top_s, top_e = [], []
    for _ in range(kk):
        m = jnp.max(cand, axis=0, keepdims=True)
        at = jnp.min(jnp.where(cand == m, pos, float(2 * kk * kk)), axis=0, keepdims=True)
        hit = pos == at
        top_s.append(m)
        top_e.append(jnp.sum(jnp.where(hit, cidx, 0.0), axis=0, keepdims=True))
        cand = jnp.where(hit, -jnp.inf, cand)
    top_s = jnp.concatenate(top_s, axis=0)
    e = jnp.exp(top_s - top_s[0:1])
    return jnp.concatenate(top_e, axis=0).astype(jnp.int32), e / jnp.sum(e, axis=0, keepdims=True)


def _peer_topk(x2d, mod3, w_pq, keys, seq_len, lat):
    nt = x2d.shape[0]
    tps = seq_len // TM
    row = (lambda i, h: 1 + i // tps) if lat else (lambda i, h: 0)
    flat = [i * PEER_TOPK + j for i, nj in _staircase() for j in range(nj)]
    nrows = -(-len(flat) // SUBLANE) * SUBLANE
    flat = flat + [PEER_TOPK * PEER_TOPK] * (nrows - len(flat))
    pos = jnp.asarray(np.repeat(np.asarray(flat, np.float32)[:, None], TM, axis=1))
    return pl.pallas_call(
        _peer_topk_kernel,
        grid=(nt // TM, PEER_HEADS // PEER_TOPK_HEADS),
        in_specs=[pl.BlockSpec((TM, 1024), lambda i, h: (i, 0)),
                  _mod_spec(3, row), _mod_spec(4, row),
                  pl.BlockSpec((1024, PEER_TOPK_HEADS * PEER_DQ), lambda i, h: (0, h)),
                  pl.BlockSpec((PEER_TOPK_HEADS, 2, PEER_NKEYS, PEER_DQ // 2), lambda i, h: (h, 0, 0, 0)),
                  pl.BlockSpec((nrows, TM), lambda i, h: (0, 0))],
        out_specs=[pl.BlockSpec((PEER_TOPK_HEADS, PEER_TOPK, TM), lambda i, h: (h, 0, i)),
                   pl.BlockSpec((PEER_TOPK_HEADS, PEER_TOPK, TM), lambda i, h: (h, 0, i))],
        out_shape=[jax.ShapeDtypeStruct((PEER_HEADS, PEER_TOPK, nt), jnp.int32),
                   jax.ShapeDtypeStruct((PEER_HEADS, PEER_TOPK, nt), F32)],
        compiler_params=_cparams("arbitrary", "arbitrary"),
        name="peer_topk",
    )(x2d, mod3, mod3, w_pq, keys, pos)


def _peer_expert_kernel(eidx_ref, x_ref, sh_ref, sc_ref, g2_ref, gate_ref, lg_ref, lb_ref, col_ref, exp_ref,
                        uv_hbm, o_ref, *scratch):
    npair = PEER_HEADS * PEER_TOPK
    ngroup = PEER_TT // PEER_GROUP
    nrow = npair * SUBLANE
    blocks, (sems, ffn_ref) = scratch[:PEER_RING], scratch[PEER_RING:]

    def ring(slot):
        return blocks[slot // PEER_GROUP].at[slot % PEER_GROUP]

    def slot_copy(slot):
        return pltpu.make_async_copy(uv_hbm.at[pl.ds(0, npair)], ring(slot), sems.at[slot])

    def issue_token(tok, slot):
        for r in range(npair):
            pltpu.make_async_copy(uv_hbm.at[eidx_ref[tok, r]], ring(slot).at[r],
                                  sems.at[slot]).start(priority=r % 2)

    @pl.when(pl.program_id(0) == 0)
    def _():
        for t in range(PEER_AHEAD * PEER_GROUP):
            issue_token(t, t)

    rowid = lax.broadcasted_iota(jnp.int32, (PEER_GROUP, 1), 0)
    own = (lax.broadcasted_iota(jnp.int32, (SUBLANE, nrow), 1) % SUBLANE
           == lax.broadcasted_iota(jnp.int32, (SUBLANE, nrow), 0))
    sc, sh = sc_ref[...], sh_ref[...]

    def ring_round(k, carry):
        for q in range(PEER_RING):
            g = k * PEER_RING + q
            base = q * PEER_GROUP
            ahead = ((q + PEER_AHEAD) % PEER_RING) * PEER_GROUP
            r0 = pl.multiple_of(g * PEER_GROUP, PEER_GROUP)
            for jj in range(PEER_GROUP):
                slot_copy(base + jj).wait()
            z8 = jnp.zeros((PEER_GROUP, nrow), F32)
            for jj in range(PEER_GROUP):
                issue_token((g + PEER_AHEAD) * PEER_GROUP + jj, ahead + jj)
                h = (x_ref[r0 + jj] * (1.0 + sc) + sh).astype(BF16)
                u = ring(base + jj)[:, 0].reshape(nrow, LANE).astype(BF16)
                y = jnp.where(own, _nt_dot(h, u), 0.0)
                z8 = jnp.where(rowid == jj, jnp.sum(y, axis=0, keepdims=True), z8)
            z_hi = z8.astype(BF16)
            z_lo = (z8 - z_hi.astype(F32)).astype(BF16)
            s8 = (jnp.dot(z_hi, col_ref[...], preferred_element_type=F32)
                  + jnp.dot(z_lo, col_ref[...], preferred_element_type=F32))
            act = 0.5 * s8 * (1.0 + lax.erf(s8 * (2.0 ** -0.5)))
            w8 = (gate_ref[pl.ds(r0, PEER_GROUP), :] * act).astype(BF16)
            wexp = jnp.dot(w8, exp_ref[...], preferred_element_type=F32)
            for jj in range(PEER_GROUP):
                wsel = jnp.where(own, wexp[jj:jj + 1, :], 0.0).astype(BF16)
                v = ring(base + jj)[:, 1].reshape(nrow, LANE).astype(BF16)
                ffn_ref[r0 + jj] = jnp.dot(wsel, v, preferred_element_type=F32)
        return carry

    lax.fori_loop(0, ngroup // PEER_RING, ring_round, 0)

    @pl.when(pl.program_id(0) == pl.num_programs(0) - 1)
    def _():
        for t in range(PEER_AHEAD * PEER_GROUP):
            slot_copy(t).wait()

    y = DN_ALPHA * x_ref[...] + g2_ref[...] * ffn_ref[...]
    mu = jnp.sum(jnp.sum(y, axis=2, keepdims=True), axis=1, keepdims=True) * (1.0 / D_MODEL)
    yc = y - mu
    var = jnp.sum(jnp.sum(yc * yc, axis=2, keepdims=True), axis=1, keepdims=True) * (1.0 / D_MODEL)
    o_ref[...] = yc * lax.rsqrt(var + LN_EPS) * lg_ref[...] + lb_ref[...]


def _peer_experts(x2d, eidx, gate, mod, uv, layer, ln_g, ln_b, seq_len, lat):
    nt = x2d.shape[0]
    tps = seq_len // PEER_TT
    row = (lambda i: 1 + i // tps) if lat else (lambda i: 0)
    npair = PEER_HEADS * PEER_TOPK
    nrow = npair * SUBLANE
    slab = (SUBLANE, LANE)
    mod4 = mod.reshape(8, 6, *slab)
    mod_spec = lambda piece: pl.BlockSpec((None, None) + slab, lambda i: (row(i), piece, 0, 0))
    full = lambda shape: pl.BlockSpec(shape, lambda i: (0,) * len(shape))
    collapse = jnp.asarray(np.arange(nrow)[:, None] // SUBLANE == np.arange(npair)[None, :], BF16)
    ntile = nt // PEER_TT
    nahead = PEER_AHEAD * PEER_GROUP
    tiles = eidx.reshape(ntile, PEER_TT, npair) + layer * PEER_EXPERTS
    nxt = jnp.concatenate([tiles[1:, :nahead], tiles[-1:, :nahead]], axis=0)
    eidx_ext = jnp.concatenate([tiles, nxt], axis=1)
    assert (PEER_TT // PEER_GROUP) % PEER_RING == 0 and PEER_AHEAD < PEER_RING
    out = pl.pallas_call(
        _peer_expert_kernel,
        grid=(ntile,),
        in_specs=[pl.BlockSpec((None, PEER_TT + nahead, npair), lambda i: (i, 0, 0), memory_space=pltpu.SMEM),
                  pl.BlockSpec((PEER_TT,) + slab, lambda i: (i, 0, 0)),
                  mod_spec(3), mod_spec(4), mod_spec(5),
                  pl.BlockSpec((PEER_TT, npair), lambda i: (i, 0)),
                  full(slab), full(slab), full((nrow, npair)), full((npair, nrow)),
                  pl.BlockSpec(memory_space=pl.ANY)],
        out_specs=pl.BlockSpec((PEER_TT,) + slab, lambda i: (i, 0, 0)),
        out_shape=jax.ShapeDtypeStruct((nt,) + slab, F32),
        scratch_shapes=[pltpu.VMEM((PEER_GROUP, npair, 2) + slab, F32)] * PEER_RING
                       + [pltpu.SemaphoreType.DMA((PEER_SLOTS,)),
                        pltpu.VMEM((PEER_TT,) + slab, F32)],
        compiler_params=_cparams("arbitrary"),
        name="peer_experts_ln2",
    )(eidx_ext, x2d.reshape((nt,) + slab), mod4, mod4, mod4, gate, ln_g.reshape(slab), ln_b.reshape(slab),
      collapse, collapse.T, uv)
    return out.reshape(nt, D_MODEL)


def _layer(x2d, mod3, lw, nseq, seq_len, lat, s0_f, s0_b, ctx):
    p2d = _inproj(x2d, mod3, lw["w_in"], seq_len, lat)
    o_f, s_f = _gla(p2d, s0_f, lw["wa_f"], lw["ba_f"], nseq, seq_len, rev=False)
    o_b, s_b = _gla(p2d, s0_b, lw["wa_b"], lw["ba_b"], nseq, seq_len, rev=True)
    if lat:
        y_c = _lat_attention(p2d, lw["sink"], ctx["k"], ctx["v"], ctx["cos"], ctx["sin"], nseq, seq_len)
    else:
        y_c = _ctx_attention(p2d, lw["sink"], nseq, seq_len)
    x1 = _merge(x2d, p2d, o_f, o_b, y_c, mod3, lw["gn"], lw["conv_w"], lw["wb"], lw["wo"],
                lw["ln1_g"], lw["ln1_b"], seq_len, lat)
    eidx_t, gate_t = _peer_topk(x1, mod3, lw["w_pq"], lw["keys"], seq_len, lat)
    nt = x2d.shape[0]
    eidx = eidx_t.reshape(PEER_HEADS * PEER_TOPK, nt).T
    gate = gate_t.reshape(PEER_HEADS * PEER_TOPK, nt).T
    x2 = _peer_experts(x1, eidx, gate, mod3, lw["uv"], lw["layer"], lw["ln2_g"], lw["ln2_b"], seq_len, lat)
    return x2, p2d, s_f, s_b


def _layer_weights(l, w_in, w_gla_a2, b_gla_a, gla_norm_g, conv_w, attn_sink, w_branch, w_out,
                   ln1_g, ln1_b, w_pq, peer_keys, uv, ln2_g, ln2_b):
    w = w_in[l]
    w_in_p = jnp.concatenate(
        [w[:, O_GQ:O_GA], w[:, O_CH:O_AK], w[:, O_MG:N_IN], w[:, O_AK:O_MG], w[:, O_GA:O_CH],
         jnp.zeros((D_MODEL, N_P - N_IN), F32)], axis=1).astype(BF16)

    def wa_pad(d):
        return jnp.zeros((LANE, 512), F32).at[GLA_LR * d:GLA_LR * (d + 1)].set(w_gla_a2[l, d])

    return {
        "w_in": w_in_p,
        "wa_f": wa_pad(0), "wa_b": wa_pad(1),
        "ba_f": b_gla_a[l, 0].reshape(1, 512), "ba_b": b_gla_a[l, 1].reshape(1, 512),
        "gn": gla_norm_g[l].reshape(1, GLA_DV),
        "conv_w": jnp.zeros((SUBLANE, 1024), F32).at[0:CONV_K].set(conv_w[l]),
        "sink": attn_sink[l],
        "wb": w_branch[l].astype(BF16), "wo": w_out[l].astype(BF16),
        "ln1_g": ln1_g[l].reshape(1, 1024), "ln1_b": ln1_b[l].reshape(1, 1024),
        "w_pq": w_pq[l].astype(BF16),
        "keys": peer_keys[l].astype(BF16),
        "uv": uv, "layer": l,
        "ln2_g": ln2_g[l].reshape(1, 1024), "ln2_b": ln2_b[l].reshape(1, 1024),
    }


def kernel(x_prompt, x_sample, cache_k, cache_v, state_gla, c, c_ctx, ln_in_g, ln_in_b, w_mod, b_mod, w_in, w_gla_a2, b_gla_a, gla_norm_g, conv_w, attn_sink, w_branch, w_out, ln1_g, ln1_b, w_pq, peer_keys, peer_u, peer_v, ln2_g, ln2_b):
    batch, seq, _ = x_prompt.shape
    dec_batch, dec_seq, _ = x_sample.shape
    past = cache_k.shape[2]
    assert dec_batch + 1 <= 8 and seq % TM == 0 and dec_seq % TM_IN == 0 and seq % GLA_BLK == 0

    cond8 = jnp.zeros((8, D_MODEL), F32).at[0].set(c_ctx).at[1:1 + dec_batch].set(c)
    mod = _modulation(cond8, w_mod, b_mod)
    cos_t, sin_t = _rope_tables(dec_seq)

    xp = _ln_in(x_prompt.reshape(batch * seq, D_MODEL), ln_in_g, ln_in_b)
    xs = _ln_in(x_sample.reshape(dec_batch * dec_seq, D_MODEL), ln_in_g, ln_in_b)
    zeros_state = jnp.zeros((batch, GLA_HEADS, GLA_DK, GLA_DV), F32)
    ks, vs, ss = [], [], []
    slabs = (DEPTH, PEER_EXPERTS, SUBLANE, LANE)
    uv = jnp.stack([peer_u.reshape(slabs), peer_v.reshape(slabs)], axis=2)
    uv = uv.reshape((DEPTH * PEER_EXPERTS, 2, SUBLANE, LANE))
    for l in range(DEPTH):
        lw = _layer_weights(l, w_in, w_gla_a2, b_gla_a, gla_norm_g, conv_w, attn_sink, w_branch, w_out,
                            ln1_g, ln1_b, w_pq, peer_keys, uv, ln2_g, ln2_b)
        mod3 = mod[l].reshape(8, 1, 6 * D_MODEL)
        xp, p_ctx, s_f, s_b = _layer(xp, mod3, lw, batch, seq, False, zeros_state, zeros_state, None)
        ks.append(p_ctx[:, P_AK:P_AK + 256].reshape(batch, seq, ATT_KV_HEADS, ATT_HD))
        vs.append(p_ctx[:, P_AV:P_AV + 256].reshape(batch, seq, ATT_KV_HEADS, ATT_HD))
        ss.append(jnp.stack([s_f, s_b], axis=1))
        ctx = {"k": cache_k[:, l].reshape(dec_batch, past, ATT_KV_HEADS * ATT_HD),
               "v": cache_v[:, l].reshape(dec_batch, past, ATT_KV_HEADS * ATT_HD),
               "cos": cos_t, "sin": sin_t}
        xs, _, _, _ = _layer(xs, mod3, lw, dec_batch, dec_seq, True,
                             state_gla[:, l, 0], state_gla[:, l, 1], ctx)
    return (xp.reshape(batch, seq, D_MODEL), xs.reshape(dec_batch, dec_seq, D_MODEL),
            jnp.stack(ks, axis=1), jnp.stack(vs, axis=1), jnp.stack(ss, axis=1))
```

```python
import functools

import numpy as np
import jax
import jax.numpy as jnp
from jax import lax
from jax.experimental import pallas as pl
from jax.experimental.pallas import tpu as pltpu

F32 = jnp.float32
BF16 = jnp.bfloat16
HIGHEST = lax.Precision.HIGHEST

D_MODEL = 1024
DEPTH = 2
GRID_W = 64
LN_EPS = 1e-6
DN_ALPHA = float((2 * DEPTH) ** 0.25)

GLA_HEADS = 4
GLA_DK = 128
GLA_DV = 256
GLA_LR = 16
GLA_TAU = 16.0
CONV_K = 3
ATT_HD = 64
ATT_HEADS = 16
ATT_KV_HEADS = 4
ATT_GROUP = 4
ATT_BLOCK = 128
ROPE_THETA = 10000.0
PEER_HEADS = 8
PEER_NKEYS = 128
PEER_EXPERTS = PEER_NKEYS * PEER_NKEYS
PEER_TOPK = 16
PEER_DQ = 256

O_GQ, O_GK, O_GV, O_GG, O_GA = 0, 512, 1024, 2048, 3072
O_CH, O_CB, O_CC, O_AQ, O_AK, O_AV, O_MG = 3104, 4128, 5152, 6176, 7200, 7456, 7712
N_IN = 10784
P_GQ, P_GK, P_GV, P_GG = 0, 512, 1024, 2048
P_CH, P_CB, P_CC, P_AQ, P_MG = 3072, 4096, 5120, 6144, 7168
P_AK, P_AV, P_GA = 10240, 10496, 10752
N_P = 10880

LANE = 128
SUBLANE = 8
VMEM_LIMIT = 56 * 1024 * 1024

TM = 256
TM_IN = 1024
TN_IN = 2176
GLA_BLK = 256
GLA_LEVELS = 8
GLA_MM_LEVELS = 2
PEER_TOPK_HEADS = 2
PEER_TT = 128
PEER_GROUP = 8
PEER_RING = 4
PEER_AHEAD = 2
PEER_SLOTS = PEER_RING * PEER_GROUP


def _cparams(*sem):
    return pltpu.CompilerParams(dimension_semantics=sem, vmem_limit_bytes=VMEM_LIMIT)


def _layer_norm(x, g, b):
    mu = jnp.mean(x, axis=-1, keepdims=True)
    xc = x - mu
    var = jnp.mean(xc * xc, axis=-1, keepdims=True)
    return xc * lax.rsqrt(var + LN_EPS) * g + b


def _nt_dot(a, b):
    return lax.dot_general(a, b, (((1,), (1,)), ((), ())), preferred_element_type=F32)


def _mod_kernel(c_ref, w_ref, b_ref, o_ref):
    c = c_ref[...]
    s = c * jax.nn.sigmoid(c)
    o_ref[0] = jnp.dot(s, w_ref[0], precision=HIGHEST, preferred_element_type=F32) + b_ref[0]


def _modulation(cond8, w_mod, b_mod):
    tn = 1536
    return pl.pallas_call(
        _mod_kernel,
        grid=(DEPTH, 6 * D_MODEL // tn),
        in_specs=[pl.BlockSpec((8, D_MODEL), lambda l, j: (0, 0)),
                  pl.BlockSpec((1, D_MODEL, tn), lambda l, j: (l, 0, j)),
                  pl.BlockSpec((1, 1, tn), lambda l, j: (l, 0, j))],
        out_specs=pl.BlockSpec((1, 8, tn), lambda l, j: (l, 0, j)),
        out_shape=jax.ShapeDtypeStruct((DEPTH, 8, 6 * D_MODEL), F32),
        compiler_params=_cparams("arbitrary", "arbitrary"),
        name="ada_modulation",
    )(cond8, w_mod, b_mod.reshape(DEPTH, 1, 6 * D_MODEL))


def _mod_spec(piece, row_fn):
    return pl.BlockSpec((None, 1, D_MODEL), lambda *ids: (row_fn(*ids), 0, piece))


def _ln_kernel(x_ref, g_ref, b_ref, o_ref):
    o_ref[...] = _layer_norm(x_ref[...], g_ref[...], b_ref[...])


def _ln_in(x2d, g, b):
    nt = x2d.shape[0]
    return pl.pallas_call(
        _ln_kernel,
        grid=(nt // TM_IN,),
        in_specs=[pl.BlockSpec((TM_IN, D_MODEL), lambda i: (i, 0)),
                  pl.BlockSpec((1, D_MODEL), lambda i: (0, 0)),
                  pl.BlockSpec((1, D_MODEL), lambda i: (0, 0))],
        out_specs=pl.BlockSpec((TM_IN, D_MODEL), lambda i: (i, 0)),
        out_shape=jax.ShapeDtypeStruct((nt, D_MODEL), F32),
        compiler_params=_cparams("arbitrary"),
        name="ln_in",
    )(x2d, g.reshape(1, D_MODEL), b.reshape(1, D_MODEL))


def _inproj_kernel(x_ref, sh_ref, sc_ref, w_ref, o_ref):
    h = x_ref[...] * (1.0 + sc_ref[...]) + sh_ref[...]
    o_ref[...] = jnp.dot(h.astype(BF16), w_ref[...], preferred_element_type=F32)


def _inproj(x2d, mod3, w_in_p, seq_len, lat):
    nt = x2d.shape[0]
    tiles_per_seq = max(seq_len // TM_IN, 1)
    row = (lambda j, i: 1 + i // tiles_per_seq) if lat else (lambda j, i: 0)
    return pl.pallas_call(
        _inproj_kernel,
        grid=(N_P // TN_IN, nt // TM_IN),
        in_specs=[pl.BlockSpec((TM_IN, D_MODEL), lambda j, i: (i, 0)),
                  _mod_spec(0, row), _mod_spec(1, row),
                  pl.BlockSpec((D_MODEL, TN_IN), lambda j, i: (0, j))],
        out_specs=pl.BlockSpec((TM_IN, TN_IN), lambda j, i: (i, j)),
        out_shape=jax.ShapeDtypeStruct((nt, N_P), F32),
        compiler_params=_cparams("arbitrary", "arbitrary"),
        name="in_projection",
    )(x2d, mod3, mod3, w_in_p)


def _gla_tables(rev):
    n = GLA_BLK
    idx = np.arange(n)
    if rev:
        cum = (idx[None, :] >= idx[:, None]).astype(np.float32)
    else:
        cum = (idx[None, :] <= idx[:, None]).astype(np.float32)
    rows = [cum]
    for k in range(GLA_MM_LEVELS):
        h = 1 << k
        rows.append(cum[(idx & ~(2 * h - 1)) + (h if rev else h - 1)])
    mall = np.concatenate(rows, axis=0)
    t, s = idx[:, None], idx[None, :]
    x = t ^ s
    top = np.where(x > 0, np.floor(np.log2(np.maximum(x, 1))).astype(np.int64) + 1, 0)
    attend = (s >= t) if rev else (s <= t)
    lv = np.where(attend, top, -1).astype(np.int32)
    return jnp.asarray(mall, BF16), jnp.asarray(lv)


def _gla_kernel(rev, q_ref, k_ref, v_ref, ga_ref, wa_ref, ba_ref, mall_ref, lv_ref, s0_ref,
                o_ref, sout_ref, st_ref, t_ref):
    j = pl.program_id(1)
    n = GLA_BLK

    @pl.when(j == 0)
    def _():
        for h in range(GLA_HEADS):
            st_ref[h] = s0_ref[0, h].T

    z = jnp.dot(ga_ref[...], wa_ref[...], precision=HIGHEST, preferred_element_type=F32) + ba_ref[...]
    a = (jnp.minimum(z, 0.0) - jnp.log(1.0 + jnp.exp(-jnp.abs(z)))) * (1.0 / GLA_TAU)
    a_hi = a.astype(BF16)
    r1 = a - a_hi.astype(F32)
    a_mid = r1.astype(BF16)
    a_lo = (r1 - a_mid.astype(F32)).astype(BF16)
    mall = mall_ref[...]
    t_ref[...] = (jnp.dot(mall, a_hi, preferred_element_type=F32)
                  + jnp.dot(mall, a_mid, preferred_element_type=F32)
                  + jnp.dot(mall, a_lo, preferred_element_type=F32))

    lv = lv_ref[...]
    for h in range(GLA_HEADS):
        kc = slice(GLA_DK * h, GLA_DK * (h + 1))
        vc = slice(GLA_DV * h, GLA_DV * (h + 1))
        b = t_ref[0:n, kc]
        btot = t_ref[0:1, kc] if rev else t_ref[n - 1:n, kc]
        q = q_ref[:, kc] * (GLA_DK ** -0.5)
        k = k_ref[:, kc]
        v = v_ref[:, vc]
        att = jnp.where(lv == 0, _nt_dot(q.astype(BF16), k.astype(BF16)), 0.0)
        for lev in range(GLA_LEVELS):
            if lev < GLA_MM_LEVELS:
                pv = t_ref[(lev + 1) * n:(lev + 2) * n, kc]
            else:
                hh = 1 << lev
                pv = jnp.concatenate(
                    [jnp.broadcast_to(t_ref[g0 + (hh if rev else hh - 1):g0 + (hh if rev else hh - 1) + 1, kc],
                                      (2 * hh, GLA_DK)) for g0 in range(0, n, 2 * hh)], axis=0)
            ql = (q * jnp.exp(jnp.minimum(b - pv, 0.0))).astype(BF16)
            kl = (k * jnp.exp(jnp.minimum(pv - b, 0.0))).astype(BF16)
            att = jnp.where(lv == lev + 1, _nt_dot(ql, kl), att)
        st = st_ref[h]
        qi = (q * jnp.exp(b)).astype(BF16)
        o = (jnp.dot(att.astype(BF16), v.astype(BF16), preferred_element_type=F32)
             + _nt_dot(qi, st.astype(BF16)))
        o_ref[:, vc] = o
        kh = (k * jnp.exp(btot - b)).astype(BF16)
        st_ref[h] = st * jnp.exp(btot) + jnp.dot(v.T.astype(BF16), kh, preferred_element_type=F32)

    @pl.when(j == pl.num_programs(1) - 1)
    def _():
        for h in range(GLA_HEADS):
            sout_ref[0, h] = st_ref[h].T


def _gla(p2d, s0, wa_pad, ba, nseq, seq_len, rev):
    nt = p2d.shape[0]
    nblk = seq_len // GLA_BLK
    mall, lv = _gla_tables(rev)

    def rb(b, j):
        return b * nblk + (nblk - 1 - j if rev else j)

    return pl.pallas_call(
        functools.partial(_gla_kernel, rev),
        grid=(nseq, nblk),
        in_specs=[pl.BlockSpec((GLA_BLK, 512), lambda b, j: (rb(b, j), P_GQ // 512)),
                  pl.BlockSpec((GLA_BLK, 512), lambda b, j: (rb(b, j), P_GK // 512)),
                  pl.BlockSpec((GLA_BLK, 1024), lambda b, j: (rb(b, j), P_GV // 1024)),
                  pl.BlockSpec((GLA_BLK, LANE), lambda b, j: (rb(b, j), P_GA // LANE)),
                  pl.BlockSpec((LANE, 512), lambda b, j: (0, 0)),
                  pl.BlockSpec((1, 512), lambda b, j: (0, 0)),
                  pl.BlockSpec(mall.shape, lambda b, j: (0, 0)),
                  pl.BlockSpec(lv.shape, lambda b, j: (0, 0)),
                  pl.BlockSpec((1, GLA_HEADS, GLA_DK, GLA_DV), lambda b, j: (b, 0, 0, 0))],
        out_specs=[pl.BlockSpec((GLA_BLK, 1024), lambda b, j: (rb(b, j), 0)),
                   pl.BlockSpec((1, GLA_HEADS, GLA_DK, GLA_DV), lambda b, j: (b, 0, 0, 0))],
        out_shape=[jax.ShapeDtypeStruct((nt, 1024), F32),
                   jax.ShapeDtypeStruct((nseq, GLA_HEADS, GLA_DK, GLA_DV), F32)],
        scratch_shapes=[pltpu.VMEM((GLA_HEADS, GLA_DV, GLA_DK), F32),
                        pltpu.VMEM(((GLA_MM_LEVELS + 1) * GLA_BLK, 512), F32)],
        compiler_params=_cparams("arbitrary", "arbitrary"),
        name="gla_bwd" if rev else "gla_fwd",
    )(p2d, p2d, p2d, p2d, wa_pad, ba, mall, lv, s0)


def _softmax_sink_heads(q_rows, k_all, v_all, bias, sink_col):
    s = _nt_dot(q_rows.astype(BF16), k_all.astype(BF16))
    if bias is not None:
        s = s + bias
    m = jnp.maximum(jnp.max(s, axis=-1, keepdims=True), sink_col)
    p = jnp.exp(s - m)
    den = jnp.sum(p, axis=-1, keepdims=True) + jnp.exp(sink_col - m)
    o = jnp.dot(p.astype(BF16), v_all.astype(BF16), preferred_element_type=F32)
    return o / den


def _ctx_attn_kernel(sink_ref, q_ref, k_ref, v_ref, o_ref):
    t = q_ref.shape[0]
    rows = lax.broadcasted_iota(jnp.int32, (ATT_GROUP * t, 1), 0)
    for g in range(ATT_KV_HEADS):
        kg = k_ref[:, ATT_HD * g:ATT_HD * (g + 1)]
        vg = v_ref[:, ATT_HD * g:ATT_HD * (g + 1)]
        qs, sink_col = [], jnp.zeros((ATT_GROUP * t, 1), F32)
        for i in range(ATT_GROUP):
            hh = g * ATT_GROUP + i
            qs.append(q_ref[:, ATT_HD * hh:ATT_HD * (hh + 1)] * (ATT_HD ** -0.5))
            sink_col = jnp.where(rows // t == i, sink_ref[hh], sink_col)
        o = _softmax_sink_heads(jnp.concatenate(qs, axis=0), kg, vg, None, sink_col)
        for i in range(ATT_GROUP):
            hh = g * ATT_GROUP + i
            o_ref[:, ATT_HD * hh:ATT_HD * (hh + 1)] = o[i * t:(i + 1) * t]


def _ctx_attention(p2d, sink, nseq, seq_len):
    nt = p2d.shape[0]
    return pl.pallas_call(
        _ctx_attn_kernel,
        grid=(nseq,),
        in_specs=[pl.BlockSpec(memory_space=pltpu.SMEM),
                  pl.BlockSpec((seq_len, 1024), lambda b: (b, P_AQ // 1024)),
                  pl.BlockSpec((seq_len, 256), lambda b: (b, P_AK // 256)),
                  pl.BlockSpec((seq_len, 256), lambda b: (b, P_AV // 256))],
        out_specs=pl.BlockSpec((seq_len, 1024), lambda b: (b, 0)),
        out_shape=jax.ShapeDtypeStruct((nt, 1024), F32),
        compiler_params=_cparams("arbitrary"),
        name="context_attention",
    )(sink, p2d, p2d, p2d)


def _rope(x, cos_t, sin_t):
    even = lax.broadcasted_iota(jnp.int32, x.shape, 1) % 2 == 0
    swapped = jnp.where(even, pltpu.roll(x, LANE - 1, 1), pltpu.roll(x, 1, 1))
    return x * cos_t + swapped * sin_t


def _lat_attn_kernel(sink_ref, q_ref, kp_ref, kc_ref, kn_ref, vp_ref, vc_ref, vn_ref,
                     ck_ref, cv_ref, cq_ref, sq_ref, cp_ref, sp_ref, cn_ref, sn_ref, o_ref):
    n = pl.program_id(1)
    nb = pl.num_programs(1)
    t = ATT_BLOCK
    kblocks = []
    for kref, c_ref, s_ref in ((kp_ref, cp_ref, sp_ref), (kc_ref, cq_ref, sq_ref), (kn_ref, cn_ref, sn_ref)):
        kblocks.append(jnp.concatenate(
            [_rope(kref[:, LANE * c:LANE * (c + 1)], c_ref[...], s_ref[...]) for c in range(2)], axis=1))
    k_loc = jnp.concatenate(kblocks, axis=0)
    v_loc = jnp.concatenate([vp_ref[...], vc_ref[...], vn_ref[...]], axis=0)
    k_all = jnp.concatenate([k_loc, ck_ref[...]], axis=0)
    v_all = jnp.concatenate([v_loc, cv_ref[...]], axis=0)
    tc = ck_ref.shape[0]
    qi = lax.broadcasted_iota(jnp.int32, (t, 3 * t + tc), 0)
    kj = lax.broadcasted_iota(jnp.int32, (t, 3 * t + tc), 1)
    ninf = jnp.float32(-jnp.inf)
    prev_bias = jnp.where(n > 0, jnp.float32(0.0), ninf)
    next_bias = jnp.where(n < nb - 1, jnp.float32(0.0), ninf)
    bias1 = jnp.where(kj < t, jnp.where(kj >= qi, prev_bias, ninf),
                      jnp.where(kj < 2 * t, 0.0,
                                jnp.where(kj < 3 * t, jnp.where(kj - 2 * t <= qi, next_bias, ninf), 0.0)))
    bias = jnp.concatenate([bias1] * ATT_GROUP, axis=0)
    rows = lax.broadcasted_iota(jnp.int32, (ATT_GROUP * t, 1), 0)
    qr = [_rope(q_ref[:, LANE * c:LANE * (c + 1)], cq_ref[...], sq_ref[...]) * (ATT_HD ** -0.5)
          for c in range(ATT_HEADS * ATT_HD // LANE)]
    for g in range(ATT_KV_HEADS):
        qs, sink_col = [], jnp.zeros((ATT_GROUP * t, 1), F32)
        for i in range(ATT_GROUP):
            hh = g * ATT_GROUP + i
            slab = qr[hh // 2]
            qs.append(slab[:, ATT_HD * (hh % 2):ATT_HD * (hh % 2 + 1)])
            sink_col = jnp.where(rows // t == i, sink_ref[hh], sink_col)
        kg = k_all[:, ATT_HD * g:ATT_HD * (g + 1)]
        vg = v_all[:, ATT_HD * g:ATT_HD * (g + 1)]
        o = _softmax_sink_heads(jnp.concatenate(qs, axis=0), kg, vg, bias, sink_col)
        for i in range(ATT_GROUP):
            hh = g * ATT_GROUP + i
            o_ref[:, ATT_HD * hh:ATT_HD * (hh + 1)] = o[i * t:(i + 1) * t]


def _lat_attention(p2d, sink, cache_k, cache_v, cos_t, sin_t, nseq, seq_len):
    nt = p2d.shape[0]
    nb = seq_len // ATT_BLOCK
    tc = cache_k.shape[1]
    cur = lambda b, n: b * nb + n
    prv = lambda b, n: b * nb + jnp.maximum(n - 1, 0)
    nxt = lambda b, n: b * nb + jnp.minimum(n + 1, nb - 1)
    kv = lambda f, col: pl.BlockSpec((ATT_BLOCK, 256), lambda b, n: (f(b, n), col))
    tab = lambda f: pl.BlockSpec((ATT_BLOCK, LANE), lambda b, n: (f(0, n), 0))
    return pl.pallas_call(
        _lat_attn_kernel,
        grid=(nseq, nb),
        in_specs=[pl.BlockSpec(memory_space=pltpu.SMEM),
                  pl.BlockSpec((ATT_BLOCK, 1024), lambda b, n: (cur(b, n), P_AQ // 1024)),
                  kv(prv, P_AK // 256), kv(cur, P_AK // 256), kv(nxt, P_AK // 256),
                  kv(prv, P_AV // 256), kv(cur, P_AV // 256), kv(nxt, P_AV // 256),
                  pl.BlockSpec((None, tc, 256), lambda b, n: (b, 0, 0)),
                  pl.BlockSpec((None, tc, 256), lambda b, n: (b, 0, 0)),
                  tab(cur), tab(cur), tab(prv), tab(prv), tab(nxt), tab(nxt)],
        out_specs=pl.BlockSpec((ATT_BLOCK, 1024), lambda b, n: (cur(b, n), 0)),
        out_shape=jax.ShapeDtypeStruct((nt, 1024), F32),
        compiler_params=_cparams("arbitrary", "arbitrary"),
        name="latent_window_attention",
    )(sink, p2d, p2d, p2d, p2d, p2d, p2d, p2d, cache_k, cache_v,
      cos_t, sin_t, cos_t, sin_t, cos_t, sin_t)


def _rope_tables(seq_len):
    rows = seq_len // GRID_W
    row = jnp.repeat(jnp.arange(rows, dtype=F32), GRID_W)
    col = jnp.tile(jnp.arange(GRID_W, dtype=F32), rows)
    half = ATT_HD // 2
    inv = ROPE_THETA ** (-jnp.arange(0, half, 2, dtype=F32) / half)
    ang = jnp.concatenate([row[:, None] * inv, col[:, None] * inv], -1)
    cos, sin = jnp.cos(ang), jnp.sin(ang)
    cos_t = jnp.tile(jnp.repeat(cos, 2, axis=1), (1, LANE // ATT_HD))
    sign = jnp.tile(jnp.asarray([-1.0, 1.0], F32), ATT_HD // 2)
    sin_t = jnp.tile(jnp.repeat(sin, 2, axis=1) * sign, (1, LANE // ATT_HD))
    return cos_t, sin_t


def _merge_kernel(tiles_per_seq, of_ref, ob_ref, gg_ref, ch_ref, cb_ref, cc_ref,
                  chp_ref, ccp_ref, chn_ref, ccn_ref, yc_ref, mga_ref, mgb_ref, mgc_ref, x_ref, g1_ref,
                  gn_ref, cw_ref, wb_ref, wo_ref, lg_ref, lb_ref, o_ref):
    i = pl.program_id(0)
    tm = x_ref.shape[0]
    gn = gn_ref[...]
    ya = []
    for h in range(GLA_HEADS):
        vc = slice(GLA_DV * h, GLA_DV * (h + 1))
        o = of_ref[:, vc] + ob_ref[:, vc]
        o = o * lax.rsqrt(jnp.mean(o * o, axis=-1, keepdims=True) + LN_EPS) * gn
        gg = gg_ref[:, vc]
        ya.append(o * (gg * jax.nn.sigmoid(gg)))
    ya = jnp.concatenate(ya, axis=1)
    z = cc_ref[...] * ch_ref[...]
    has_prev = (i % tiles_per_seq) != 0
    has_next = (i % tiles_per_seq) != tiles_per_seq - 1
    z_before = jnp.where(has_prev, ccp_ref[SUBLANE - 1:SUBLANE, :] * chp_ref[SUBLANE - 1:SUBLANE, :], 0.0)
    z_after = jnp.where(has_next, ccn_ref[0:1, :] * chn_ref[0:1, :], 0.0)
    r = lax.broadcasted_iota(jnp.int32, z.shape, 0)
    zp = jnp.where(r == 0, z_before, pltpu.roll(z, 1, 0))
    zn = jnp.where(r == tm - 1, z_after, pltpu.roll(z, tm - 1, 0))
    yb = cb_ref[...] * (cw_ref[0:1, :] * zp + cw_ref[1:2, :] * z + cw_ref[2:3, :] * zn)
    merged = jnp.zeros((tm, D_MODEL), F32)
    for bi, (y, mg_ref) in enumerate(((ya, mga_ref), (yb, mgb_ref), (yc_ref[...], mgc_ref))):
        proj = jnp.dot(y.astype(BF16), wb_ref[bi], preferred_element_type=F32)
        merged = merged + jax.nn.sigmoid(mg_ref[...]) * proj
    mix = jnp.dot(merged.astype(BF16), wo_ref[...], preferred_element_type=F32)
    o_ref[...] = _layer_norm(DN_ALPHA * x_ref[...] + g1_ref[...] * mix, lg_ref[...], lb_ref[...])


def _merge(x2d, p2d, o_f, o_b, y_c, mod3, gn, conv_w, wb, wo, ln_g, ln_b, seq_len, lat):
    nt = x2d.shape[0]
    tps = seq_len // TM
    row = (lambda i: 1 + i // tps) if lat else (lambda i: 0)
    col = lambda c: pl.BlockSpec((TM, 1024), lambda i: (i, c // 1024))
    halo_rows = TM // SUBLANE
    nhalo = nt // SUBLANE
    prev = lambda c: pl.BlockSpec((SUBLANE, 1024), lambda i: (jnp.maximum(i * halo_rows - 1, 0), c // 1024))
    nxt = lambda c: pl.BlockSpec((SUBLANE, 1024),
                                 lambda i: (jnp.minimum((i + 1) * halo_rows, nhalo - 1), c // 1024))
    full = lambda shape: pl.BlockSpec(shape, lambda i: (0,) * len(shape))
    return pl.pallas_call(
        functools.partial(_merge_kernel, tps),
        grid=(nt // TM,),
        in_specs=[pl.BlockSpec((TM, 1024), lambda i: (i, 0)), pl.BlockSpec((TM, 1024), lambda i: (i, 0)),
                  col(P_GG), col(P_CH), col(P_CB), col(P_CC),
                  prev(P_CH), prev(P_CC), nxt(P_CH), nxt(P_CC),
                  pl.BlockSpec((TM, 1024), lambda i: (i, 0)),
                  col(P_MG), col(P_MG + 1024), col(P_MG + 2048),
                  pl.BlockSpec((TM, 1024), lambda i: (i, 0)),
                  _mod_spec(2, row),
                  full((1, GLA_DV)), full((SUBLANE, 1024)), full((3, 1024, 1024)), full((1024, 1024)),
                  full((1, 1024)), full((1, 1024))],
        out_specs=pl.BlockSpec((TM, 1024), lambda i: (i, 0)),
        out_shape=jax.ShapeDtypeStruct((nt, 1024), F32),
        compiler_params=_cparams("arbitrary"),
        name="merge_ln1",
    )(o_f, o_b, p2d, p2d, p2d, p2d, p2d, p2d, p2d, p2d, y_c, p2d, p2d, p2d, x2d, mod3,
      gn, conv_w, wb, wo, ln_g, ln_b)


def _sorting_network(n):
    pairs = []
    p = 1
    while p < n:
        k = p
        while k >= 1:
            for j in range(k % p, n - k, 2 * k):
                for i in range(min(k, n - j - k)):
                    if (i + j) // (2 * p) == (i + j + k) // (2 * p):
                        pairs.append((i + j, i + j + k))
            k //= 2
        p *= 2
    return pairs


def _topk_rows(s, kk):
    n, m = s.shape
    nslab = n // SUBLANE
    assert nslab == kk
    base = lax.broadcasted_iota(jnp.int32, (SUBLANE, m), 0).astype(F32)
    val = [s[SUBLANE * r:SUBLANE * (r + 1)] for r in range(nslab)]
    idx = [base + float(SUBLANE * r) for r in range(nslab)]
    for a, b in _sorting_network(nslab):
        keep = val[a] >= val[b]
        val[a], val[b] = jnp.where(keep, val[a], val[b]), jnp.where(keep, val[b], val[a])
        idx[a], idx[b] = jnp.where(keep, idx[a], idx[b]), jnp.where(keep, idx[b], idx[a])
    vals, idxs = [], []
    for r in range(kk):
        top = jnp.max(val[0], axis=0, keepdims=True)
        at = jnp.min(jnp.where(val[0] == top, idx[0], float(n)), axis=0, keepdims=True)
        vals.append(top)
        idxs.append(at)
        pop = idx[0] == at
        depth = kk - 1 - r
        for d in range(depth):
            val[d] = jnp.where(pop, val[d + 1], val[d])
            idx[d] = jnp.where(pop, idx[d + 1], idx[d])
    return jnp.concatenate(vals, axis=0), jnp.concatenate(idxs, axis=0)


def _staircase():
    return [(i, PEER_TOPK // (i + 1)) for i in range(PEER_TOPK)]


def _peer_topk_kernel(x_ref, sh_ref, sc_ref, w_ref, keys_ref, pos_ref, eidx_ref, gate_ref):
    h2 = (x_ref[...] * (1.0 + sc_ref[...]) + sh_ref[...]).astype(BF16)
    for hh in range(PEER_TOPK_HEADS):
        pq = jnp.dot(h2, w_ref[:, PEER_DQ * hh:PEER_DQ * (hh + 1)], preferred_element_type=F32)
        eidx_ref[hh], gate_ref[hh] = _peer_topk_head(pq, keys_ref[hh], pos_ref[...])


def _peer_topk_head(pq, keys, pos):
    half = PEER_DQ // 2
    tops = []
    for p in range(2):
        s = _nt_dot(keys[p], pq[:, half * p:half * (p + 1)].astype(BF16))
        tops.append(_topk_rows(s, PEER_TOPK))
    (s1, i1), (s2, i2) = tops
    tm = s1.shape[1]
    kk = PEER_TOPK
    npad = pos.shape[0] - sum(nj for _, nj in _staircase())
    cand = jnp.concatenate([s1[i:i + 1] + s2[0:nj] for i, nj in _staircase()]
                           + [jnp.full((npad, tm), -jnp.inf, F32)], axis=0)
    cidx = jnp.concatenate([i1[i:i + 1] * float(PEER_NKEYS) + i2[0:nj] for i, nj in _staircase()]
                           + [jnp.zeros((npad, tm), F32)], axis=0)
    top_s, top_e = [], []
    for _ in range(kk):
        m = jnp.max(cand, axis=0, keepdims=True)
        at = jnp.min(jnp.where(cand == m, pos, float(2 * kk * kk)), axis=0, keepdims=True)
        hit = pos == at
        top_s.append(m)
        top_e.append(jnp.sum(jnp.where(hit, cidx, 0.0), axis=0, keepdims=True))
        cand = jnp.where(hit, -jnp.inf, cand)
    top_s = jnp.concatenate(top_s, axis=0)
    e = jnp.exp(top_s - top_s[0:1])
    return jnp.concatenate(top_e, axis=0).astype(jnp.int32), e / jnp.sum(e, axis=0, keepdims=True)


def _topk_pos():
    flat = [i * PEER_TOPK + j for i, nj in _staircase() for j in range(nj)]
    nrows = -(-len(flat) // SUBLANE) * SUBLANE
    flat = flat + [PEER_TOPK * PEER_TOPK] * (nrows - len(flat))
    return jnp.asarray(np.repeat(np.asarray(flat, np.float32)[:, None], TM, axis=1))


def _token_major(t, nt):
    return t.reshape(PEER_HEADS * PEER_TOPK, nt).T


def _peer_topk(x2d, mod3, w_pq, keys, seq_len, lat):
    nt = x2d.shape[0]
    tps = seq_len // TM
    row = (lambda i, h: 1 + i // tps) if lat else (lambda i, h: 0)
    pos = _topk_pos()
    nrows = pos.shape[0]
    return pl.pallas_call(
        _peer_topk_kernel,
        grid=(nt // TM, PEER_HEADS // PEER_TOPK_HEADS),
        in_specs=[pl.BlockSpec((TM, 1024), lambda i, h: (i, 0)),
                  _mod_spec(3, row), _mod_spec(4, row),
                  pl.BlockSpec((1024, PEER_TOPK_HEADS * PEER_DQ), lambda i, h: (0, h)),
                  pl.BlockSpec((PEER_TOPK_HEADS, 2, PEER_NKEYS, PEER_DQ // 2), lambda i, h: (h, 0, 0, 0)),
                  pl.BlockSpec((nrows, TM), lambda i, h: (0, 0))],
        out_specs=[pl.BlockSpec((PEER_TOPK_HEADS, PEER_TOPK, TM), lambda i, h: (h, 0, i)),
                   pl.BlockSpec((PEER_TOPK_HEADS, PEER_TOPK, TM), lambda i, h: (h, 0, i))],
        out_shape=[jax.ShapeDtypeStruct((PEER_HEADS, PEER_TOPK, nt), jnp.int32),
                   jax.ShapeDtypeStruct((PEER_HEADS, PEER_TOPK, nt), F32)],
        compiler_params=_cparams("arbitrary", "arbitrary"),
        name="peer_topk",
    )(x2d, mod3, mod3, w_pq, keys, pos)


def _peer_expert_kernel(*refs, side):
    npair = PEER_HEADS * PEER_TOPK
    ngroup = PEER_TT // PEER_GROUP
    nrow = npair * SUBLANE
    (eidx_ref, x_ref, sh_ref, sc_ref, g2_ref, gate_ref, lg_ref, lb_ref, col_ref, exp_ref, uv_hbm), refs = refs[:11], refs[11:]
    if side is not None:
        (xb_ref, shb_ref, scb_ref, wpq_ref, keys_ref, pos_ref), refs = refs[:6], refs[6:]
        (o_ref, eidxb_ref, gateb_ref), refs = refs[:3], refs[3:]
        h2b_ref, refs = refs[-1], refs[:-1]
        h2b_ref[...] = (xb_ref[...] * (1.0 + scb_ref[...]) + shb_ref[...]).astype(BF16)
    else:
        o_ref, refs = refs[0], refs[1:]
    blocks, (sems, ffn_ref) = refs[:PEER_RING], refs[PEER_RING:]

    def ring(slot):
        return blocks[slot // PEER_GROUP].at[slot % PEER_GROUP]

    def slot_copy(slot):
        return pltpu.make_async_copy(uv_hbm.at[pl.ds(0, npair)], ring(slot), sems.at[slot])

    def issue_token(tok, slot):
        for r in range(npair):
            pltpu.make_async_copy(uv_hbm.at[eidx_ref[tok, r]], ring(slot).at[r],
                                  sems.at[slot]).start(priority=r % 2)

    @pl.when(pl.program_id(0) == 0)
    def _():
        for t in range(PEER_AHEAD * PEER_GROUP):
            issue_token(t, t)

    rowid = lax.broadcasted_iota(jnp.int32, (PEER_GROUP, 1), 0)
    own = (lax.broadcasted_iota(jnp.int32, (SUBLANE, nrow), 1) % SUBLANE
           == lax.broadcasted_iota(jnp.int32, (SUBLANE, nrow), 0))
    sc, sh = sc_ref[...], sh_ref[...]

    def ring_round(k, carry):
        for q in range(PEER_RING):
            g = k * PEER_RING + q
            base = q * PEER_GROUP
            ahead = ((q + PEER_AHEAD) % PEER_RING) * PEER_GROUP
            r0 = pl.multiple_of(g * PEER_GROUP, PEER_GROUP)
            for jj in range(PEER_GROUP):
                slot_copy(base + jj).wait()
            z8 = jnp.zeros((PEER_GROUP, nrow), F32)
            for jj in range(PEER_GROUP):
                issue_token((g + PEER_AHEAD) * PEER_GROUP + jj, ahead + jj)
                h = (x_ref[r0 + jj] * (1.0 + sc) + sh).astype(BF16)
                u = ring(base + jj)[:, 0].reshape(nrow, LANE).astype(BF16)
                y = jnp.where(own, _nt_dot(h, u), 0.0)
                z8 = jnp.where(rowid == jj, jnp.sum(y, axis=0, keepdims=True), z8)
            z_hi = z8.astype(BF16)
            z_lo = (z8 - z_hi.astype(F32)).astype(BF16)
            s8 = (jnp.dot(z_hi, col_ref[...], preferred_element_type=F32)
                  + jnp.dot(z_lo, col_ref[...], preferred_element_type=F32))
            act = 0.5 * s8 * (1.0 + lax.erf(s8 * (2.0 ** -0.5)))
            w8 = (gate_ref[pl.ds(r0, PEER_GROUP), :] * act).astype(BF16)
            wexp = jnp.dot(w8, exp_ref[...], preferred_element_type=F32)
            for jj in range(PEER_GROUP):
                wsel = jnp.where(own, wexp[jj:jj + 1, :], 0.0).astype(BF16)
                v = ring(base + jj)[:, 1].reshape(nrow, LANE).astype(BF16)
                ffn_ref[r0 + jj] = jnp.dot(wsel, v, preferred_element_type=F32)
        if side is not None:
            per_round, every = side

            def side_heads():
                for hh in range(per_round):
                    h = (k // every) * per_round + hh
                    pq = jnp.dot(h2b_ref[...], wpq_ref[h], preferred_element_type=F32)
                    eidxb_ref[h], gateb_ref[h] = _peer_topk_head(pq, keys_ref[h], pos_ref[...])

            if every == 1:
                side_heads()
            else:
                pl.when(k % every == 0)(side_heads)
        return carry

    lax.fori_loop(0, ngroup // PEER_RING, ring_round, 0)

    @pl.when(pl.program_id(0) == pl.num_programs(0) - 1)
    def _():
        for t in range(PEER_AHEAD * PEER_GROUP):
            slot_copy(t).wait()

    y = DN_ALPHA * x_ref[...] + g2_ref[...] * ffn_ref[...]
    mu = jnp.sum(jnp.sum(y, axis=2, keepdims=True), axis=1, keepdims=True) * (1.0 / D_MODEL)
    yc = y - mu
    var = jnp.sum(jnp.sum(yc * yc, axis=2, keepdims=True), axis=1, keepdims=True) * (1.0 / D_MODEL)
    o_ref[...] = yc * lax.rsqrt(var + LN_EPS) * lg_ref[...] + lb_ref[...]


def _peer_experts(x2d, eidx, gate, mod, uv, layer, ln_g, ln_b, seq_len, lat, side=None):
    nt = x2d.shape[0]
    tps = seq_len // PEER_TT
    row = (lambda i: 1 + i // tps) if lat else (lambda i: 0)
    npair = PEER_HEADS * PEER_TOPK
    nrow = npair * SUBLANE
    slab = (SUBLANE, LANE)
    mod4 = mod.reshape(8, 6, *slab)
    mod_spec = lambda piece: pl.BlockSpec((None, None) + slab, lambda i: (row(i), piece, 0, 0))
    full = lambda shape: pl.BlockSpec(shape, lambda i: (0,) * len(shape))
    collapse = jnp.asarray(np.arange(nrow)[:, None] // SUBLANE == np.arange(npair)[None, :], BF16)
    ntile = nt // PEER_TT
    nahead = PEER_AHEAD * PEER_GROUP
    tiles = eidx.reshape(ntile, PEER_TT, npair) + layer * PEER_EXPERTS
    nxt = jnp.concatenate([tiles[1:, :nahead], tiles[-1:, :nahead]], axis=0)
    eidx_ext = jnp.concatenate([tiles, nxt], axis=1)
    assert (PEER_TT // PEER_GROUP) % PEER_RING == 0 and PEER_AHEAD < PEER_RING
    in_specs = [pl.BlockSpec((None, PEER_TT + nahead, npair), lambda i: (i, 0, 0), memory_space=pltpu.SMEM),
                pl.BlockSpec((PEER_TT,) + slab, lambda i: (i, 0, 0)),
                mod_spec(3), mod_spec(4), mod_spec(5),
                pl.BlockSpec((PEER_TT, npair), lambda i: (i, 0)),
                full(slab), full(slab), full((nrow, npair)), full((npair, nrow)),
                pl.BlockSpec(memory_space=pl.ANY)]
    args = [eidx_ext, x2d.reshape((nt,) + slab), mod4, mod4, mod4, gate, ln_g.reshape(slab), ln_b.reshape(slab),
            collapse, collapse.T, uv]
    out_specs = [pl.BlockSpec((PEER_TT,) + slab, lambda i: (i, 0, 0))]
    out_shape = [jax.ShapeDtypeStruct((nt,) + slab, F32)]
    scratch = ([pltpu.VMEM((PEER_GROUP, npair, 2) + slab, F32)] * PEER_RING
               + [pltpu.SemaphoreType.DMA((PEER_SLOTS,)), pltpu.VMEM((PEER_TT,) + slab, F32)])
    side_cfg = None
    if side is not None:
        xb, mod3_b, w_pq, keys, seq_b, lat_b = side
        ntb = xb.shape[0]
        nround = PEER_TT // PEER_GROUP // PEER_RING
        heads_per_step = PEER_HEADS * (ntb // TM) // ntile
        steps_per_tile = PEER_HEADS // heads_per_step
        per_round = max(heads_per_step // nround, 1)
        side_cfg = (per_round, nround * per_round // heads_per_step)
        assert heads_per_step * ntile == PEER_HEADS * (ntb // TM) and side_cfg[1] * heads_per_step == nround * per_round
        tile_b = lambda i: i // steps_per_tile
        part_b = lambda i: i % steps_per_tile
        tps_b = seq_b // TM
        row_b = (lambda i: 1 + tile_b(i) // tps_b) if lat_b else (lambda i: 0)
        pos = _topk_pos()
        w_heads = w_pq.reshape(D_MODEL, PEER_HEADS, PEER_DQ).transpose(1, 0, 2)
        in_specs += [pl.BlockSpec((TM, D_MODEL), lambda i: (tile_b(i), 0)),
                     _mod_spec(3, row_b), _mod_spec(4, row_b),
                     pl.BlockSpec((heads_per_step, D_MODEL, PEER_DQ), lambda i: (part_b(i), 0, 0)),
                     pl.BlockSpec((heads_per_step, 2, PEER_NKEYS, PEER_DQ // 2), lambda i: (part_b(i), 0, 0, 0)),
                     full(pos.shape)]
        args += [xb, mod3_b, mod3_b, w_heads, keys, pos]
        out_specs += [pl.BlockSpec((heads_per_step, PEER_TOPK, TM), lambda i: (part_b(i), 0, tile_b(i)))] * 2
        out_shape += [jax.ShapeDtypeStruct((PEER_HEADS, PEER_TOPK, ntb), jnp.int32),
                      jax.ShapeDtypeStruct((PEER_HEADS, PEER_TOPK, ntb), F32)]
        scratch += [pltpu.VMEM((TM, D_MODEL), BF16)]
    outs = pl.pallas_call(
        functools.partial(_peer_expert_kernel, side=side_cfg),
        grid=(ntile,),
        in_specs=in_specs,
        out_specs=out_specs,
        out_shape=out_shape,
        scratch_shapes=scratch,
        compiler_params=_cparams("arbitrary"),
        name="peer_experts_ln2" if side is None else "peer_experts_ln2_topk",
    )(*args)
    x2 = outs[0].reshape(nt, D_MODEL)
    if side is None:
        return x2, None
    return x2, (_token_major(outs[1], ntb), _token_major(outs[2], ntb))


def _mixer_half(x2d, mod3, lw, nseq, seq_len, lat, s0_f, s0_b, ctx):
    p2d = _inproj(x2d, mod3, lw["w_in"], seq_len, lat)
    o_f, s_f = _gla(p2d, s0_f, lw["wa_f"], lw["ba_f"], nseq, seq_len, rev=False)
    o_b, s_b = _gla(p2d, s0_b, lw["wa_b"], lw["ba_b"], nseq, seq_len, rev=True)
    if lat:
        y_c = _lat_attention(p2d, lw["sink"], ctx["k"], ctx["v"], ctx["cos"], ctx["sin"], nseq, seq_len)
    else:
        y_c = _ctx_attention(p2d, lw["sink"], nseq, seq_len)
    x1 = _merge(x2d, p2d, o_f, o_b, y_c, mod3, lw["gn"], lw["conv_w"], lw["wb"], lw["wo"],
                lw["ln1_g"], lw["ln1_b"], seq_len, lat)
    return x1, p2d, s_f, s_b


def _layer_weights(l, w_in, w_gla_a2, b_gla_a, gla_norm_g, conv_w, attn_sink, w_branch, w_out,
                   ln1_g, ln1_b, w_pq, peer_keys, uv, ln2_g, ln2_b):
    w = w_in[l]
    w_in_p = jnp.concatenate(
        [w[:, O_GQ:O_GA], w[:, O_CH:O_AK], w[:, O_MG:N_IN], w[:, O_AK:O_MG], w[:, O_GA:O_CH],
         jnp.zeros((D_MODEL, N_P - N_IN), F32)], axis=1).astype(BF16)

    def wa_pad(d):
        return jnp.zeros((LANE, 512), F32).at[GLA_LR * d:GLA_LR * (d + 1)].set(w_gla_a2[l, d])

    return {
        "w_in": w_in_p,
        "wa_f": wa_pad(0), "wa_b": wa_pad(1),
        "ba_f": b_gla_a[l, 0].reshape(1, 512), "ba_b": b_gla_a[l, 1].reshape(1, 512),
        "gn": gla_norm_g[l].reshape(1, GLA_DV),
        "conv_w": jnp.zeros((SUBLANE, 1024), F32).at[0:CONV_K].set(conv_w[l]),
        "sink": attn_sink[l],
        "wb": w_branch[l].astype(BF16), "wo": w_out[l].astype(BF16),
        "ln1_g": ln1_g[l].reshape(1, 1024), "ln1_b": ln1_b[l].reshape(1, 1024),
        "w_pq": w_pq[l].astype(BF16),
        "keys": peer_keys[l].astype(BF16),
        "uv": uv, "layer": l,
        "ln2_g": ln2_g[l].reshape(1, 1024), "ln2_b": ln2_b[l].reshape(1, 1024),
    }


def kernel(x_prompt, x_sample, cache_k, cache_v, state_gla, c, c_ctx, ln_in_g, ln_in_b, w_mod, b_mod, w_in, w_gla_a2, b_gla_a, gla_norm_g, conv_w, attn_sink, w_branch, w_out, ln1_g, ln1_b, w_pq, peer_keys, peer_u, peer_v, ln2_g, ln2_b):
    batch, seq, _ = x_prompt.shape
    dec_batch, dec_seq, _ = x_sample.shape
    past = cache_k.shape[2]
    assert dec_batch + 1 <= 8 and seq % TM == 0 and dec_seq % TM_IN == 0 and seq % GLA_BLK == 0

    cond8 = jnp.zeros((8, D_MODEL), F32).at[0].set(c_ctx).at[1:1 + dec_batch].set(c)
    mod = _modulation(cond8, w_mod, b_mod)
    cos_t, sin_t = _rope_tables(dec_seq)

    xp = _ln_in(x_prompt.reshape(batch * seq, D_MODEL), ln_in_g, ln_in_b)
    xs = _ln_in(x_sample.reshape(dec_batch * dec_seq, D_MODEL), ln_in_g, ln_in_b)
    zeros_state = jnp.zeros((batch, GLA_HEADS, GLA_DK, GLA_DV), F32)
    ks, vs, ss = [], [], []
    pending = None
    slabs = (DEPTH, PEER_EXPERTS, SUBLANE, LANE)
    uv = jnp.stack([peer_u.reshape(slabs), peer_v.reshape(slabs)], axis=2)
    uv = uv.reshape((DEPTH * PEER_EXPERTS, 2, SUBLANE, LANE))
    for l in range(DEPTH):
        lw = _layer_weights(l, w_in, w_gla_a2, b_gla_a, gla_norm_g, conv_w, attn_sink, w_branch, w_out,
                            ln1_g, ln1_b, w_pq, peer_keys, uv, ln2_g, ln2_b)
        mod3 = mod[l].reshape(8, 1, 6 * D_MODEL)
        x1c, p_ctx, s_f, s_b = _mixer_half(xp, mod3, lw, batch, seq, False, zeros_state, zeros_state, None)
        ks.append(p_ctx[:, P_AK:P_AK + 256].reshape(batch, seq, ATT_KV_HEADS, ATT_HD))
        vs.append(p_ctx[:, P_AV:P_AV + 256].reshape(batch, seq, ATT_KV_HEADS, ATT_HD))
        ss.append(jnp.stack([s_f, s_b], axis=1))
        if pending is None:
            e_t, g_t = _peer_topk(x1c, mod3, lw["w_pq"], lw["keys"], seq, False)
            ec, gc = _token_major(e_t, batch * seq), _token_major(g_t, batch * seq)
        else:
            xs, (ec, gc) = _peer_experts(*pending, side=(x1c, mod3, lw["w_pq"], lw["keys"], seq, False))
        ctx = {"k": cache_k[:, l].reshape(dec_batch, past, ATT_KV_HEADS * ATT_HD),
               "v": cache_v[:, l].reshape(dec_batch, past, ATT_KV_HEADS * ATT_HD),
               "cos": cos_t, "sin": sin_t}
        x1l, _, _, _ = _mixer_half(xs, mod3, lw, dec_batch, dec_seq, True,
                                   state_gla[:, l, 0], state_gla[:, l, 1], ctx)
        xp, (el, gl) = _peer_experts(x1c, ec, gc, mod3, lw["uv"], l, lw["ln2_g"], lw["ln2_b"], seq, False,
                                     side=(x1l, mod3, lw["w_pq"], lw["keys"], dec_seq, True))
        pending = (x1l, el, gl, mod3, lw["uv"], l, lw["ln2_g"], lw["ln2_b"], dec_seq, True)
    xs, _ = _peer_experts(*pending)
    return (xp.reshape(batch, seq, D_MODEL), xs.reshape(dec_batch, dec_seq, D_MODEL),
            jnp.stack(ks, axis=1), jnp.stack(vs, axis=1), jnp.stack(ss, axis=1))
```

```python
import functools

import numpy as np
import jax
import jax.numpy as jnp
from jax import lax
from jax.experimental import pallas as pl
from jax.experimental.pallas import tpu as pltpu

F32 = jnp.float32
BF16 = jnp.bfloat16
HIGHEST = lax.Precision.HIGHEST

D_MODEL = 1024
DEPTH = 2
GRID_W = 64
LN_EPS = 1e-6
DN_ALPHA = float((2 * DEPTH) ** 0.25)

GLA_HEADS = 4
GLA_DK = 128
GLA_DV = 256
GLA_LR = 16
GLA_TAU = 16.0
CONV_K = 3
ATT_HD = 64
ATT_HEADS = 16
ATT_KV_HEADS = 4
ATT_GROUP = 4
ATT_BLOCK = 128
ROPE_THETA = 10000.0
PEER_HEADS = 8
PEER_NKEYS = 128
PEER_EXPERTS = PEER_NKEYS * PEER_NKEYS
PEER_TOPK = 16
PEER_DQ = 256

O_GQ, O_GK, O_GV, O_GG, O_GA = 0, 512, 1024, 2048, 3072
O_CH, O_CB, O_CC, O_AQ, O_AK, O_AV, O_MG = 3104, 4128, 5152, 6176, 7200, 7456, 7712
N_IN = 10784
P_GQ, P_GK, P_GV, P_GG = 0, 512, 1024, 2048
P_CH, P_CB, P_CC, P_AQ, P_MG = 3072, 4096, 5120, 6144, 7168
P_AK, P_AV, P_GA = 10240, 10496, 10752
N_P = 10880

LANE = 128
SUBLANE = 8
VMEM_LIMIT = 56 * 1024 * 1024

TM = 256
TM_IN = 1024
TN_IN = 2176
GLA_BLK = 256
GLA_LEVELS = 8
GLA_MM_LEVELS = 2
PEER_TOPK_HEADS = 2
PEER_TT = 128
PEER_GROUP = 8
PEER_RING = 4
PEER_AHEAD = 2
PEER_SLOTS = PEER_RING * PEER_GROUP


def _cparams(*sem):
    return pltpu.CompilerParams(dimension_semantics=sem, vmem_limit_bytes=VMEM_LIMIT)


def _layer_norm(x, g, b):
    mu = jnp.mean(x, axis=-1, keepdims=True)
    xc = x - mu
    var = jnp.mean(xc * xc, axis=-1, keepdims=True)
    return xc * lax.rsqrt(var + LN_EPS) * g + b


def _nt_dot(a, b):
    return lax.dot_general(a, b, (((1,), (1,)), ((), ())), preferred_element_type=F32)


def _mod_kernel(c_ref, w_ref, b_ref, o_ref):
    c = c_ref[...]
    s = c * jax.nn.sigmoid(c)
    o_ref[0] = jnp.dot(s, w_ref[0], precision=HIGHEST, preferred_element_type=F32) + b_ref[0]


def _modulation(cond8, w_mod, b_mod):
    tn = 1536
    return pl.pallas_call(
        _mod_kernel,
        grid=(DEPTH, 6 * D_MODEL // tn),
        in_specs=[pl.BlockSpec((8, D_MODEL), lambda l, j: (0, 0)),
                  pl.BlockSpec((1, D_MODEL, tn), lambda l, j: (l, 0, j)),
                  pl.BlockSpec((1, 1, tn), lambda l, j: (l, 0, j))],
        out_specs=pl.BlockSpec((1, 8, tn), lambda l, j: (l, 0, j)),
        out_shape=jax.ShapeDtypeStruct((DEPTH, 8, 6 * D_MODEL), F32),
        compiler_params=_cparams("arbitrary", "arbitrary"),
        name="ada_modulation",
    )(cond8, w_mod, b_mod.reshape(DEPTH, 1, 6 * D_MODEL))


def _mod_spec(piece, row_fn):
    return pl.BlockSpec((None, 1, D_MODEL), lambda *ids: (row_fn(*ids), 0, piece))


def _ln_kernel(x_ref, g_ref, b_ref, o_ref):
    o_ref[...] = _layer_norm(x_ref[...], g_ref[...], b_ref[...])


def _ln_in(x2d, g, b):
    nt = x2d.shape[0]
    return pl.pallas_call(
        _ln_kernel,
        grid=(nt // TM_IN,),
        in_specs=[pl.BlockSpec((TM_IN, D_MODEL), lambda i: (i, 0)),
                  pl.BlockSpec((1, D_MODEL), lambda i: (0, 0)),
                  pl.BlockSpec((1, D_MODEL), lambda i: (0, 0))],
        out_specs=pl.BlockSpec((TM_IN, D_MODEL), lambda i: (i, 0)),
        out_shape=jax.ShapeDtypeStruct((nt, D_MODEL), F32),
        compiler_params=_cparams("arbitrary"),
        name="ln_in",
    )(x2d, g.reshape(1, D_MODEL), b.reshape(1, D_MODEL))


def _inproj_kernel(x_ref, sh_ref, sc_ref, w_ref, o_ref):
    h = x_ref[...] * (1.0 + sc_ref[...]) + sh_ref[...]
    o_ref[...] = jnp.dot(h.astype(BF16), w_ref[...], preferred_element_type=F32)


def _inproj(x2d, mod3, w_in_p, seq_len, lat):
    nt = x2d.shape[0]
    tiles_per_seq = max(seq_len // TM_IN, 1)
    row = (lambda j, i: 1 + i // tiles_per_seq) if lat else (lambda j, i: 0)
    return pl.pallas_call(
        _inproj_kernel,
        grid=(N_P // TN_IN, nt // TM_IN),
        in_specs=[pl.BlockSpec((TM_IN, D_MODEL), lambda j, i: (i, 0)),
                  _mod_spec(0, row), _mod_spec(1, row),
                  pl.BlockSpec((D_MODEL, TN_IN), lambda j, i: (0, j))],
        out_specs=pl.BlockSpec((TM_IN, TN_IN), lambda j, i: (i, j)),
        out_shape=jax.ShapeDtypeStruct((nt, N_P), F32),
        compiler_params=_cparams("arbitrary", "arbitrary"),
        name="in_projection",
    )(x2d, mod3, mod3, w_in_p)


def _gla_tables(rev):
    n = GLA_BLK
    idx = np.arange(n)
    if rev:
        cum = (idx[None, :] >= idx[:, None]).astype(np.float32)
    else:
        cum = (idx[None, :] <= idx[:, None]).astype(np.float32)
    rows = [cum]
    for k in range(GLA_MM_LEVELS):
        h = 1 << k
        rows.append(cum[(idx & ~(2 * h - 1)) + (h if rev else h - 1)])
    mall = np.concatenate(rows, axis=0)
    t, s = idx[:, None], idx[None, :]
    x = t ^ s
    top = np.where(x > 0, np.floor(np.log2(np.maximum(x, 1))).astype(np.int64) + 1, 0)
    attend = (s >= t) if rev else (s <= t)
    lv = np.where(attend, top, -1).astype(np.int32)
    return jnp.asarray(mall, BF16), jnp.asarray(lv)


def _gla_kernel(rev, q_ref, k_ref, v_ref, ga_ref, wa_ref, ba_ref, mall_ref, lv_ref, s0_ref,
                o_ref, sout_ref, st_ref, t_ref):
    j = pl.program_id(1)
    n = GLA_BLK

    @pl.when(j == 0)
    def _():
        for h in range(GLA_HEADS):
            st_ref[h] = s0_ref[0, h].T

    z = jnp.dot(ga_ref[...], wa_ref[...], precision=HIGHEST, preferred_element_type=F32) + ba_ref[...]
    a = (jnp.minimum(z, 0.0) - jnp.log(1.0 + jnp.exp(-jnp.abs(z)))) * (1.0 / GLA_TAU)
    a_hi = a.astype(BF16)
    r1 = a - a_hi.astype(F32)
    a_mid = r1.astype(BF16)
    a_lo = (r1 - a_mid.astype(F32)).astype(BF16)
    mall = mall_ref[...]
    t_ref[...] = (jnp.dot(mall, a_hi, preferred_element_type=F32)
                  + jnp.dot(mall, a_mid, preferred_element_type=F32)
                  + jnp.dot(mall, a_lo, preferred_element_type=F32))

    lv = lv_ref[...]
    for h in range(GLA_HEADS):
        kc = slice(GLA_DK * h, GLA_DK * (h + 1))
        vc = slice(GLA_DV * h, GLA_DV * (h + 1))
        b = t_ref[0:n, kc]
        btot = t_ref[0:1, kc] if rev else t_ref[n - 1:n, kc]
        q = q_ref[:, kc] * (GLA_DK ** -0.5)
        k = k_ref[:, kc]
        v = v_ref[:, vc]
        att = jnp.where(lv == 0, _nt_dot(q.astype(BF16), k.astype(BF16)), 0.0)
        for lev in range(GLA_LEVELS):
            if lev < GLA_MM_LEVELS:
                pv = t_ref[(lev + 1) * n:(lev + 2) * n, kc]
            else:
                hh = 1 << lev
                pv = jnp.concatenate(
                    [jnp.broadcast_to(t_ref[g0 + (hh if rev else hh - 1):g0 + (hh if rev else hh - 1) + 1, kc],
                                      (2 * hh, GLA_DK)) for g0 in range(0, n, 2 * hh)], axis=0)
            ql = (q * jnp.exp(jnp.minimum(b - pv, 0.0))).astype(BF16)
            kl = (k * jnp.exp(jnp.minimum(pv - b, 0.0))).astype(BF16)
            att = jnp.where(lv == lev + 1, _nt_dot(ql, kl), att)
        st = st_ref[h]
        qi = (q * jnp.exp(b)).astype(BF16)
        o = (jnp.dot(att.astype(BF16), v.astype(BF16), preferred_element_type=F32)
             + _nt_dot(qi, st.astype(BF16)))
        o_ref[:, vc] = o
        kh = (k * jnp.exp(btot - b)).astype(BF16)
        st_ref[h] = st * jnp.exp(btot) + jnp.dot(v.T.astype(BF16), kh, preferred_element_type=F32)

    @pl.when(j == pl.num_programs(1) - 1)
    def _():
        for h in range(GLA_HEADS):
            sout_ref[0, h] = st_ref[h].T


def _gla(p2d, s0, wa_pad, ba, nseq, seq_len, rev):
    nt = p2d.shape[0]
    nblk = seq_len // GLA_BLK
    mall, lv = _gla_tables(rev)

    def rb(b, j):
        return b * nblk + (nblk - 1 - j if rev else j)

    return pl.pallas_call(
        functools.partial(_gla_kernel, rev),
        grid=(nseq, nblk),
        in_specs=[pl.BlockSpec((GLA_BLK, 512), lambda b, j: (rb(b, j), P_GQ // 512)),
                  pl.BlockSpec((GLA_BLK, 512), lambda b, j: (rb(b, j), P_GK // 512)),
                  pl.BlockSpec((GLA_BLK, 1024), lambda b, j: (rb(b, j), P_GV // 1024)),
                  pl.BlockSpec((GLA_BLK, LANE), lambda b, j: (rb(b, j), P_GA // LANE)),
                  pl.BlockSpec((LANE, 512), lambda b, j: (0, 0)),
                  pl.BlockSpec((1, 512), lambda b, j: (0, 0)),
                  pl.BlockSpec(mall.shape, lambda b, j: (0, 0)),
                  pl.BlockSpec(lv.shape, lambda b, j: (0, 0)),
                  pl.BlockSpec((1, GLA_HEADS, GLA_DK, GLA_DV), lambda b, j: (b, 0, 0, 0))],
        out_specs=[pl.BlockSpec((GLA_BLK, 1024), lambda b, j: (rb(b, j), 0)),
                   pl.BlockSpec((1, GLA_HEADS, GLA_DK, GLA_DV), lambda b, j: (b, 0, 0, 0))],
        out_shape=[jax.ShapeDtypeStruct((nt, 1024), F32),
                   jax.ShapeDtypeStruct((nseq, GLA_HEADS, GLA_DK, GLA_DV), F32)],
        scratch_shapes=[pltpu.VMEM((GLA_HEADS, GLA_DV, GLA_DK), F32),
                        pltpu.VMEM(((GLA_MM_LEVELS + 1) * GLA_BLK, 512), F32)],
        compiler_params=_cparams("arbitrary", "arbitrary"),
        name="gla_bwd" if rev else "gla_fwd",
    )(p2d, p2d, p2d, p2d, wa_pad, ba, mall, lv, s0)


def _softmax_sink_heads(q_rows, k_all, v_all, bias, sink_col):
    s = _nt_dot(q_rows.astype(BF16), k_all.astype(BF16))
    if bias is not None:
        s = s + bias
    m = jnp.maximum(jnp.max(s, axis=-1, keepdims=True), sink_col)
    p = jnp.exp(s - m)
    den = jnp.sum(p, axis=-1, keepdims=True) + jnp.exp(sink_col - m)
    o = jnp.dot(p.astype(BF16), v_all.astype(BF16), preferred_element_type=F32)
    return o / den


def _ctx_attn_kernel(sink_ref, q_ref, k_ref, v_ref, o_ref):
    t = q_ref.shape[0]
    rows = lax.broadcasted_iota(jnp.int32, (ATT_GROUP * t, 1), 0)
    for g in range(ATT_KV_HEADS):
        kg = k_ref[:, ATT_HD * g:ATT_HD * (g + 1)]
        vg = v_ref[:, ATT_HD * g:ATT_HD * (g + 1)]
        qs, sink_col = [], jnp.zeros((ATT_GROUP * t, 1), F32)
        for i in range(ATT_GROUP):
            hh = g * ATT_GROUP + i
            qs.append(q_ref[:, ATT_HD * hh:ATT_HD * (hh + 1)] * (ATT_HD ** -0.5))
            sink_col = jnp.where(rows // t == i, sink_ref[hh], sink_col)
        o = _softmax_sink_heads(jnp.concatenate(qs, axis=0), kg, vg, None, sink_col)
        for i in range(ATT_GROUP):
            hh = g * ATT_GROUP + i
            o_ref[:, ATT_HD * hh:ATT_HD * (hh + 1)] = o[i * t:(i + 1) * t]


def _ctx_attention(p2d, sink, nseq, seq_len):
    nt = p2d.shape[0]
    return pl.pallas_call(
        _ctx_attn_kernel,
        grid=(nseq,),
        in_specs=[pl.BlockSpec(memory_space=pltpu.SMEM),
                  pl.BlockSpec((seq_len, 1024), lambda b: (b, P_AQ // 1024)),
                  pl.BlockSpec((seq_len, 256), lambda b: (b, P_AK // 256)),
                  pl.BlockSpec((seq_len, 256), lambda b: (b, P_AV // 256))],
        out_specs=pl.BlockSpec((seq_len, 1024), lambda b: (b, 0)),
        out_shape=jax.ShapeDtypeStruct((nt, 1024), F32),
        compiler_params=_cparams("arbitrary"),
        name="context_attention",
    )(sink, p2d, p2d, p2d)


def _rope(x, cos_t, sin_t):
    even = lax.broadcasted_iota(jnp.int32, x.shape, 1) % 2 == 0
    swapped = jnp.where(even, pltpu.roll(x, LANE - 1, 1), pltpu.roll(x, 1, 1))
    return x * cos_t + swapped * sin_t


def _lat_attn_kernel(sink_ref, q_ref, kp_ref, kc_ref, kn_ref, vp_ref, vc_ref, vn_ref,
                     ck_ref, cv_ref, cq_ref, sq_ref, cp_ref, sp_ref, cn_ref, sn_ref, o_ref):
    n = pl.program_id(1)
    nb = pl.num_programs(1)
    t = ATT_BLOCK
    kblocks = []
    for kref, c_ref, s_ref in ((kp_ref, cp_ref, sp_ref), (kc_ref, cq_ref, sq_ref), (kn_ref, cn_ref, sn_ref)):
        kblocks.append(jnp.concatenate(
            [_rope(kref[:, LANE * c:LANE * (c + 1)], c_ref[...], s_ref[...]) for c in range(2)], axis=1))
    k_loc = jnp.concatenate(kblocks, axis=0)
    v_loc = jnp.concatenate([vp_ref[...], vc_ref[...], vn_ref[...]], axis=0)
    k_all = jnp.concatenate([k_loc, ck_ref[...]], axis=0)
    v_all = jnp.concatenate([v_loc, cv_ref[...]], axis=0)
    tc = ck_ref.shape[0]
    qi = lax.broadcasted_iota(jnp.int32, (t, 3 * t + tc), 0)
    kj = lax.broadcasted_iota(jnp.int32, (t, 3 * t + tc), 1)
    ninf = jnp.float32(-jnp.inf)
    prev_bias = jnp.where(n > 0, jnp.float32(0.0), ninf)
    next_bias = jnp.where(n < nb - 1, jnp.float32(0.0), ninf)
    bias1 = jnp.where(kj < t, jnp.where(kj >= qi, prev_bias, ninf),
                      jnp.where(kj < 2 * t, 0.0,
                                jnp.where(kj < 3 * t, jnp.where(kj - 2 * t <= qi, next_bias, ninf), 0.0)))
    bias = jnp.concatenate([bias1] * ATT_GROUP, axis=0)
    rows = lax.broadcasted_iota(jnp.int32, (ATT_GROUP * t, 1), 0)
    qr = [_rope(q_ref[:, LANE * c:LANE * (c + 1)], cq_ref[...], sq_ref[...]) * (ATT_HD ** -0.5)
          for c in range(ATT_HEADS * ATT_HD // LANE)]
    for g in range(ATT_KV_HEADS):
        qs, sink_col = [], jnp.zeros((ATT_GROUP * t, 1), F32)
        for i in range(ATT_GROUP):
            hh = g * ATT_GROUP + i
            slab = qr[hh // 2]
            qs.append(slab[:, ATT_HD * (hh % 2):ATT_HD * (hh % 2 + 1)])
            sink_col = jnp.where(rows // t == i, sink_ref[hh], sink_col)
        kg = k_all[:, ATT_HD * g:ATT_HD * (g + 1)]
        vg = v_all[:, ATT_HD * g:ATT_HD * (g + 1)]
        o = _softmax_sink_heads(jnp.concatenate(qs, axis=0), kg, vg, bias, sink_col)
        for i in range(ATT_GROUP):
            hh = g * ATT_GROUP + i
            o_ref[:, ATT_HD * hh:ATT_HD * (hh + 1)] = o[i * t:(i + 1) * t]


def _lat_attention(p2d, sink, cache_k, cache_v, cos_t, sin_t, nseq, seq_len):
    nt = p2d.shape[0]
    nb = seq_len // ATT_BLOCK
    tc = cache_k.shape[1]
    cur = lambda b, n: b * nb + n
    prv = lambda b, n: b * nb + jnp.maximum(n - 1, 0)
    nxt = lambda b, n: b * nb + jnp.minimum(n + 1, nb - 1)
    kv = lambda f, col: pl.BlockSpec((ATT_BLOCK, 256), lambda b, n: (f(b, n), col))
    tab = lambda f: pl.BlockSpec((ATT_BLOCK, LANE), lambda b, n: (f(0, n), 0))
    return pl.pallas_call(
        _lat_attn_kernel,
        grid=(nseq, nb),
        in_specs=[pl.BlockSpec(memory_space=pltpu.SMEM),
                  pl.BlockSpec((ATT_BLOCK, 1024), lambda b, n: (cur(b, n), P_AQ // 1024)),
                  kv(prv, P_AK // 256), kv(cur, P_AK // 256), kv(nxt, P_AK // 256),
                  kv(prv, P_AV // 256), kv(cur, P_AV // 256), kv(nxt, P_AV // 256),
                  pl.BlockSpec((None, tc, 256), lambda b, n: (b, 0, 0)),
                  pl.BlockSpec((None, tc, 256), lambda b, n: (b, 0, 0)),
                  tab(cur), tab(cur), tab(prv), tab(prv), tab(nxt), tab(nxt)],
        out_specs=pl.BlockSpec((ATT_BLOCK, 1024), lambda b, n: (cur(b, n), 0)),
        out_shape=jax.ShapeDtypeStruct((nt, 1024), F32),
        compiler_params=_cparams("arbitrary", "arbitrary"),
        name="latent_window_attention",
    )(sink, p2d, p2d, p2d, p2d, p2d, p2d, p2d, cache_k, cache_v,
      cos_t, sin_t, cos_t, sin_t, cos_t, sin_t)


def _rope_tables(seq_len):
    rows = seq_len // GRID_W
    row = jnp.repeat(jnp.arange(rows, dtype=F32), GRID_W)
    col = jnp.tile(jnp.arange(GRID_W, dtype=F32), rows)
    half = ATT_HD // 2
    inv = ROPE_THETA ** (-jnp.arange(0, half, 2, dtype=F32) / half)
    ang = jnp.concatenate([row[:, None] * inv, col[:, None] * inv], -1)
    cos, sin = jnp.cos(ang), jnp.sin(ang)
    cos_t = jnp.tile(jnp.repeat(cos, 2, axis=1), (1, LANE // ATT_HD))
    sign = jnp.tile(jnp.asarray([-1.0, 1.0], F32), ATT_HD // 2)
    sin_t = jnp.tile(jnp.repeat(sin, 2, axis=1) * sign, (1, LANE // ATT_HD))
    return cos_t, sin_t


def _merge_kernel(tiles_per_seq, of_ref, ob_ref, gg_ref, ch_ref, cb_ref, cc_ref,
                  chp_ref, ccp_ref, chn_ref, ccn_ref, yc_ref, mga_ref, mgb_ref, mgc_ref, x_ref, g1_ref,
                  gn_ref, cw_ref, wb_ref, wo_ref, lg_ref, lb_ref, o_ref):
    i = pl.program_id(0)
    tm = x_ref.shape[0]
    gn = gn_ref[...]
    ya = []
    for h in range(GLA_HEADS):
        vc = slice(GLA_DV * h, GLA_DV * (h + 1))
        o = of_ref[:, vc] + ob_ref[:, vc]
        o = o * lax.rsqrt(jnp.mean(o * o, axis=-1, keepdims=True) + LN_EPS) * gn
        gg = gg_ref[:, vc]
        ya.append(o * (gg * jax.nn.sigmoid(gg)))
    ya = jnp.concatenate(ya, axis=1)
    z = cc_ref[...] * ch_ref[...]
    has_prev = (i % tiles_per_seq) != 0
    has_next = (i % tiles_per_seq) != tiles_per_seq - 1
    z_before = jnp.where(has_prev, ccp_ref[SUBLANE - 1:SUBLANE, :] * chp_ref[SUBLANE - 1:SUBLANE, :], 0.0)
    z_after = jnp.where(has_next, ccn_ref[0:1, :] * chn_ref[0:1, :], 0.0)
    r = lax.broadcasted_iota(jnp.int32, z.shape, 0)
    zp = jnp.where(r == 0, z_before, pltpu.roll(z, 1, 0))
    zn = jnp.where(r == tm - 1, z_after, pltpu.roll(z, tm - 1, 0))
    yb = cb_ref[...] * (cw_ref[0:1, :] * zp + cw_ref[1:2, :] * z + cw_ref[2:3, :] * zn)
    merged = jnp.zeros((tm, D_MODEL), F32)
    for bi, (y, mg_ref) in enumerate(((ya, mga_ref), (yb, mgb_ref), (yc_ref[...], mgc_ref))):
        proj = jnp.dot(y.astype(BF16), wb_ref[bi], preferred_element_type=F32)
        merged = merged + jax.nn.sigmoid(mg_ref[...]) * proj
    mix = jnp.dot(merged.astype(BF16), wo_ref[...], preferred_element_type=F32)
    o_ref[...] = _layer_norm(DN_ALPHA * x_ref[...] + g1_ref[...] * mix, lg_ref[...], lb_ref[...])


def _merge(x2d, p2d, o_f, o_b, y_c, mod3, gn, conv_w, wb, wo, ln_g, ln_b, seq_len, lat):
    nt = x2d.shape[0]
    tps = seq_len // TM
    row = (lambda i: 1 + i // tps) if lat else (lambda i: 0)
    col = lambda c: pl.BlockSpec((TM, 1024), lambda i: (i, c // 1024))
    halo_rows = TM // SUBLANE
    nhalo = nt // SUBLANE
    prev = lambda c: pl.BlockSpec((SUBLANE, 1024), lambda i: (jnp.maximum(i * halo_rows - 1, 0), c // 1024))
    nxt = lambda c: pl.BlockSpec((SUBLANE, 1024),
                                 lambda i: (jnp.minimum((i + 1) * halo_rows, nhalo - 1), c // 1024))
    full = lambda shape: pl.BlockSpec(shape, lambda i: (0,) * len(shape))
    return pl.pallas_call(
        functools.partial(_merge_kernel, tps),
        grid=(nt // TM,),
        in_specs=[pl.BlockSpec((TM, 1024), lambda i: (i, 0)), pl.BlockSpec((TM, 1024), lambda i: (i, 0)),
                  col(P_GG), col(P_CH), col(P_CB), col(P_CC),
                  prev(P_CH), prev(P_CC), nxt(P_CH), nxt(P_CC),
                  pl.BlockSpec((TM, 1024), lambda i: (i, 0)),
                  col(P_MG), col(P_MG + 1024), col(P_MG + 2048),
                  pl.BlockSpec((TM, 1024), lambda i: (i, 0)),
                  _mod_spec(2, row),
                  full((1, GLA_DV)), full((SUBLANE, 1024)), full((3, 1024, 1024)), full((1024, 1024)),
                  full((1, 1024)), full((1, 1024))],
        out_specs=pl.BlockSpec((TM, 1024), lambda i: (i, 0)),
        out_shape=jax.ShapeDtypeStruct((nt, 1024), F32),
        compiler_params=_cparams("arbitrary"),
        name="merge_ln1",
    )(o_f, o_b, p2d, p2d, p2d, p2d, p2d, p2d, p2d, p2d, y_c, p2d, p2d, p2d, x2d, mod3,
      gn, conv_w, wb, wo, ln_g, ln_b)


def _sorting_network(n):
    pairs = []
    p = 1
    while p < n:
        k = p
        while k >= 1:
            for j in range(k % p, n - k, 2 * k):
                for i in range(min(k, n - j - k)):
                    if (i + j) // (2 * p) == (i + j + k) // (2 * p):
                        pairs.append((i + j, i + j + k))
            k //= 2
        p *= 2
    return pairs


def _topk_rows_steps(s, kk, out):
    n, m = s.shape
    nslab = n // SUBLANE
    assert nslab == kk
    base = lax.broadcasted_iota(jnp.int32, (SUBLANE, m), 0).astype(F32)
    val = [s[SUBLANE * r:SUBLANE * (r + 1)] for r in range(nslab)]
    idx = [base + float(SUBLANE * r) for r in range(nslab)]
    for ci, (a, b) in enumerate(_sorting_network(nslab)):
        keep = val[a] >= val[b]
        val[a], val[b] = jnp.where(keep, val[a], val[b]), jnp.where(keep, val[b], val[a])
        idx[a], idx[b] = jnp.where(keep, idx[a], idx[b]), jnp.where(keep, idx[b], idx[a])
        if ci % 4 == 3:
            yield
    vals, idxs = [], []
    for r in range(kk):
        top = jnp.max(val[0], axis=0, keepdims=True)
        at = jnp.min(jnp.where(val[0] == top, idx[0], float(n)), axis=0, keepdims=True)
        vals.append(top)
        idxs.append(at)
        pop = idx[0] == at
        depth = kk - 1 - r
        for d in range(depth):
            val[d] = jnp.where(pop, val[d + 1], val[d])
            idx[d] = jnp.where(pop, idx[d + 1], idx[d])
        yield
    out.append((jnp.concatenate(vals, axis=0), jnp.concatenate(idxs, axis=0)))


def _staircase():
    return [(i, PEER_TOPK // (i + 1)) for i in range(PEER_TOPK)]


def _peer_topk_kernel(x_ref, sh_ref, sc_ref, w_ref, keys_ref, pos_ref, eidx_ref, gate_ref):
    h2 = (x_ref[...] * (1.0 + sc_ref[...]) + sh_ref[...]).astype(BF16)
    for hh in range(PEER_TOPK_HEADS):
        def store(e, g, hh=hh):
            eidx_ref[hh], gate_ref[hh] = e, g

        pq_fn = lambda hh=hh: jnp.dot(h2, w_ref[:, PEER_DQ * hh:PEER_DQ * (hh + 1)], preferred_element_type=F32)
        for _ in _peer_topk_head_steps(pq_fn, keys_ref[hh], pos_ref[...], store):
            pass


def _peer_topk_head_steps(pq_fn, keys, pos, store):
    half = PEER_DQ // 2
    pq = pq_fn()
    yield
    tops = []
    for p in range(2):
        s = _nt_dot(keys[p], pq[:, half * p:half * (p + 1)].astype(BF16))
        yield
        yield from _topk_rows_steps(s, PEER_TOPK, tops)
    (s1, i1), (s2, i2) = tops
    tm = s1.shape[1]
    kk = PEER_TOPK
    npad = pos.shape[0] - sum(nj for _, nj in _staircase())
    cand = jnp.concatenate([s1[i:i + 1] + s2[0:nj] for i, nj in _staircase()]
                           + [jnp.full((npad, tm), -jnp.inf, F32)], axis=0)
    cidx = jnp.concatenate([i1[i:i + 1] * float(PEER_NKEYS) + i2[0:nj] for i, nj in _staircase()]
                           + [jnp.zeros((npad, tm), F32)], axis=0)
    yield
    top_s, top_e = [], []
    for _ in range(kk):
        m = jnp.max(cand, axis=0, keepdims=True)
        at = jnp.min(jnp.where(cand == m, pos, float(2 * kk * kk)), axis=0, keepdims=True)
        hit = pos == at
        top_s.append(m)
        top_e.append(jnp.sum(jnp.where(hit, cidx, 0.0), axis=0, keepdims=True))
        cand = jnp.where(hit, -jnp.inf, cand)
        yield
    top_s = jnp.concatenate(top_s, axis=0)
    e = jnp.exp(top_s - top_s[0:1])
    store(jnp.concatenate(top_e, axis=0).astype(jnp.int32), e / jnp.sum(e, axis=0, keepdims=True))


TOPK_HEAD_STEPS = 1 + 2 * (1 + 16 + PEER_TOPK) + 1 + PEER_TOPK


def _topk_pos(tm):
    flat = [i * PEER_TOPK + j for i, nj in _staircase() for j in range(nj)]
    nrows = -(-len(flat) // SUBLANE) * SUBLANE
    flat = flat + [PEER_TOPK * PEER_TOPK] * (nrows - len(flat))
    return jnp.asarray(np.repeat(np.asarray(flat, np.float32)[:, None], tm, axis=1))


def _token_major(t, nt):
    return t.reshape(PEER_HEADS * PEER_TOPK, nt).T


def _peer_topk(x2d, mod3, w_pq, keys, seq_len, lat):
    nt = x2d.shape[0]
    tps = seq_len // TM
    row = (lambda i, h: 1 + i // tps) if lat else (lambda i, h: 0)
    pos = _topk_pos(TM)
    nrows = pos.shape[0]
    return pl.pallas_call(
        _peer_topk_kernel,
        grid=(nt // TM, PEER_HEADS // PEER_TOPK_HEADS),
        in_specs=[pl.BlockSpec((TM, 1024), lambda i, h: (i, 0)),
                  _mod_spec(3, row), _mod_spec(4, row),
                  pl.BlockSpec((1024, PEER_TOPK_HEADS * PEER_DQ), lambda i, h: (0, h)),
                  pl.BlockSpec((PEER_TOPK_HEADS, 2, PEER_NKEYS, PEER_DQ // 2), lambda i, h: (h, 0, 0, 0)),
                  pl.BlockSpec((nrows, TM), lambda i, h: (0, 0))],
        out_specs=[pl.BlockSpec((PEER_TOPK_HEADS, PEER_TOPK, TM), lambda i, h: (h, 0, i)),
                   pl.BlockSpec((PEER_TOPK_HEADS, PEER_TOPK, TM), lambda i, h: (h, 0, i))],
        out_shape=[jax.ShapeDtypeStruct((PEER_HEADS, PEER_TOPK, nt), jnp.int32),
                   jax.ShapeDtypeStruct((PEER_HEADS, PEER_TOPK, nt), F32)],
        compiler_params=_cparams("arbitrary", "arbitrary"),
        name="peer_topk",
    )(x2d, mod3, mod3, w_pq, keys, pos)


def _peer_expert_kernel(*refs, side):
    npair = PEER_HEADS * PEER_TOPK
    ngroup = PEER_TT // PEER_GROUP
    nrow = npair * SUBLANE
    (eidx_ref, x_ref, sh_ref, sc_ref, g2_ref, gate_ref, lg_ref, lb_ref, col_ref, exp_ref, uv_hbm), refs = refs[:11], refs[11:]
    if side is not None:
        (xb_ref, shb_ref, scb_ref, wpq_ref, keys_ref, pos_ref), refs = refs[:6], refs[6:]
        (o_ref, eidxb_ref, gateb_ref), refs = refs[:3], refs[3:]
        h2b_ref, refs = refs[-1], refs[:-1]
        h2b_ref[...] = (xb_ref[...] * (1.0 + scb_ref[...]) + shb_ref[...]).astype(BF16)
    else:
        o_ref, refs = refs[0], refs[1:]
    blocks, (sems, ffn_ref) = refs[:PEER_RING], refs[PEER_RING:]

    def ring(slot):
        return blocks[slot // PEER_GROUP].at[slot % PEER_GROUP]

    def slot_copy(slot):
        return pltpu.make_async_copy(uv_hbm.at[pl.ds(0, npair)], ring(slot), sems.at[slot])

    def issue_token(tok, slot):
        for r in range(npair):
            pltpu.make_async_copy(uv_hbm.at[eidx_ref[tok, r]], ring(slot).at[r],
                                  sems.at[slot]).start(priority=r % 2)

    @pl.when(pl.program_id(0) == 0)
    def _():
        for t in range(PEER_AHEAD * PEER_GROUP):
            issue_token(t, t)

    rowid = lax.broadcasted_iota(jnp.int32, (PEER_GROUP, 1), 0)
    own = (lax.broadcasted_iota(jnp.int32, (SUBLANE, nrow), 1) % SUBLANE
           == lax.broadcasted_iota(jnp.int32, (SUBLANE, nrow), 0))
    sc, sh = sc_ref[...], sh_ref[...]

    def side_steps(k):
        per_round = side
        for hh in range(per_round):
            h = k * per_round + hh

            def store(e, g, h=h):
                eidxb_ref[h], gateb_ref[h] = e, g

            pq_fn = lambda h=h: jnp.dot(h2b_ref[...], wpq_ref[h], preferred_element_type=F32)
            yield from _peer_topk_head_steps(pq_fn, keys_ref[h], pos_ref[...], store)

    def ring_round(k, carry):
        steps = side_steps(k) if side is not None else iter(())
        per_issue = 0 if side is None else -(-TOPK_HEAD_STEPS * side // (PEER_RING * PEER_GROUP))
        for q in range(PEER_RING):
            g = k * PEER_RING + q
            base = q * PEER_GROUP
            ahead = ((q + PEER_AHEAD) % PEER_RING) * PEER_GROUP
            r0 = pl.multiple_of(g * PEER_GROUP, PEER_GROUP)
            for jj in range(PEER_GROUP):
                slot_copy(base + jj).wait()
            z8 = jnp.zeros((PEER_GROUP, nrow), F32)
            for jj in range(PEER_GROUP):
                issue_token((g + PEER_AHEAD) * PEER_GROUP + jj, ahead + jj)
                for _ in range(per_issue):
                    next(steps, None)
                h = (x_ref[r0 + jj] * (1.0 + sc) + sh).astype(BF16)
                u = ring(base + jj)[:, 0].reshape(nrow, LANE).astype(BF16)
                y = jnp.where(own, _nt_dot(h, u), 0.0)
                z8 = jnp.where(rowid == jj, jnp.sum(y, axis=0, keepdims=True), z8)
            z_hi = z8.astype(BF16)
            z_lo = (z8 - z_hi.astype(F32)).astype(BF16)
            s8 = (jnp.dot(z_hi, col_ref[...], preferred_element_type=F32)
                  + jnp.dot(z_lo, col_ref[...], preferred_element_type=F32))
            act = 0.5 * s8 * (1.0 + lax.erf(s8 * (2.0 ** -0.5)))
            w8 = (gate_ref[pl.ds(r0, PEER_GROUP), :] * act).astype(BF16)
            wexp = jnp.dot(w8, exp_ref[...], preferred_element_type=F32)
            for jj in range(PEER_GROUP):
                wsel = jnp.where(own, wexp[jj:jj + 1, :], 0.0).astype(BF16)
                v = ring(base + jj)[:, 1].reshape(nrow, LANE).astype(BF16)
                ffn_ref[r0 + jj] = jnp.dot(wsel, v, preferred_element_type=F32)
        for _ in steps:
            pass
        return carry

    lax.fori_loop(0, ngroup // PEER_RING, ring_round, 0)

    @pl.when(pl.program_id(0) == pl.num_programs(0) - 1)
    def _():
        for t in range(PEER_AHEAD * PEER_GROUP):
            slot_copy(t).wait()

    y = DN_ALPHA * x_ref[...] + g2_ref[...] * ffn_ref[...]
    mu = jnp.sum(jnp.sum(y, axis=2, keepdims=True), axis=1, keepdims=True) * (1.0 / D_MODEL)
    yc = y - mu
    var = jnp.sum(jnp.sum(yc * yc, axis=2, keepdims=True), axis=1, keepdims=True) * (1.0 / D_MODEL)
    o_ref[...] = yc * lax.rsqrt(var + LN_EPS) * lg_ref[...] + lb_ref[...]


def _peer_experts(x2d, eidx, gate, mod, uv, layer, ln_g, ln_b, seq_len, lat, side=None):
    nt = x2d.shape[0]
    tps = seq_len // PEER_TT
    row = (lambda i: 1 + i // tps) if lat else (lambda i: 0)
    npair = PEER_HEADS * PEER_TOPK
    nrow = npair * SUBLANE
    slab = (SUBLANE, LANE)
    mod4 = mod.reshape(8, 6, *slab)
    mod_spec = lambda piece: pl.BlockSpec((None, None) + slab, lambda i: (row(i), piece, 0, 0))
    full = lambda shape: pl.BlockSpec(shape, lambda i: (0,) * len(shape))
    collapse = jnp.asarray(np.arange(nrow)[:, None] // SUBLANE == np.arange(npair)[None, :], BF16)
    ntile = nt // PEER_TT
    nahead = PEER_AHEAD * PEER_GROUP
    tiles = eidx.reshape(ntile, PEER_TT, npair) + layer * PEER_EXPERTS
    nxt = jnp.concatenate([tiles[1:, :nahead], tiles[-1:, :nahead]], axis=0)
    eidx_ext = jnp.concatenate([tiles, nxt], axis=1)
    assert (PEER_TT // PEER_GROUP) % PEER_RING == 0 and PEER_AHEAD < PEER_RING
    in_specs = [pl.BlockSpec((None, PEER_TT + nahead, npair), lambda i: (i, 0, 0), memory_space=pltpu.SMEM),
                pl.BlockSpec((PEER_TT,) + slab, lambda i: (i, 0, 0)),
                mod_spec(3), mod_spec(4), mod_spec(5),
                pl.BlockSpec((PEER_TT, npair), lambda i: (i, 0)),
                full(slab), full(slab), full((nrow, npair)), full((npair, nrow)),
                pl.BlockSpec(memory_space=pl.ANY)]
    args = [eidx_ext, x2d.reshape((nt,) + slab), mod4, mod4, mod4, gate, ln_g.reshape(slab), ln_b.reshape(slab),
            collapse, collapse.T, uv]
    out_specs = [pl.BlockSpec((PEER_TT,) + slab, lambda i: (i, 0, 0))]
    out_shape = [jax.ShapeDtypeStruct((nt,) + slab, F32)]
    scratch = ([pltpu.VMEM((PEER_GROUP, npair, 2) + slab, F32)] * PEER_RING
               + [pltpu.SemaphoreType.DMA((PEER_SLOTS,)), pltpu.VMEM((PEER_TT,) + slab, F32)])
    side_cfg = None
    if side is not None:
        xb, mod3_b, w_pq, keys, seq_b, lat_b = side
        ntb = xb.shape[0]
        nround = PEER_TT // PEER_GROUP // PEER_RING
        tmb = next(t for t in (TM, TM // 2) if (PEER_HEADS * ntb) % (t * ntile * nround) == 0)
        heads_per_step = PEER_HEADS * (ntb // tmb) // ntile
        steps_per_tile = PEER_HEADS // heads_per_step
        side_cfg = heads_per_step // nround
        assert heads_per_step * steps_per_tile == PEER_HEADS and side_cfg * nround == heads_per_step
        tile_b = lambda i: i // steps_per_tile
        part_b = lambda i: i % steps_per_tile
        tps_b = seq_b // tmb
        row_b = (lambda i: 1 + tile_b(i) // tps_b) if lat_b else (lambda i: 0)
        pos = _topk_pos(tmb)
        w_heads = w_pq.reshape(D_MODEL, PEER_HEADS, PEER_DQ).transpose(1, 0, 2)
        in_specs += [pl.BlockSpec((tmb, D_MODEL), lambda i: (tile_b(i), 0)),
                     _mod_spec(3, row_b), _mod_spec(4, row_b),
                     pl.BlockSpec((heads_per_step, D_MODEL, PEER_DQ), lambda i: (part_b(i), 0, 0)),
                     pl.BlockSpec((heads_per_step, 2, PEER_NKEYS, PEER_DQ // 2), lambda i: (part_b(i), 0, 0, 0)),
                     full(pos.shape)]
        args += [xb, mod3_b, mod3_b, w_heads, keys, pos]
        out_specs += [pl.BlockSpec((heads_per_step, PEER_TOPK, tmb), lambda i: (part_b(i), 0, tile_b(i)))] * 2
        out_shape += [jax.ShapeDtypeStruct((PEER_HEADS, PEER_TOPK, ntb), jnp.int32),
                      jax.ShapeDtypeStruct((PEER_HEADS, PEER_TOPK, ntb), F32)]
        scratch += [pltpu.VMEM((tmb, D_MODEL), BF16)]
    outs = pl.pallas_call(
        functools.partial(_peer_expert_kernel, side=side_cfg),
        grid=(ntile,),
        in_specs=in_specs,
        out_specs=out_specs,
        out_shape=out_shape,
        scratch_shapes=scratch,
        compiler_params=_cparams("arbitrary"),
        name="peer_experts_ln2" if side is None else "peer_experts_ln2_topk",
    )(*args)
    x2 = outs[0].reshape(nt, D_MODEL)
    if side is None:
        return x2, None
    return x2, (_token_major(outs[1], ntb), _token_major(outs[2], ntb))


def _mixer_half(x2d, mod3, lw, nseq, seq_len, lat, s0_f, s0_b, ctx):
    p2d = _inproj(x2d, mod3, lw["w_in"], seq_len, lat)
    o_f, s_f = _gla(p2d, s0_f, lw["wa_f"], lw["ba_f"], nseq, seq_len, rev=False)
    o_b, s_b = _gla(p2d, s0_b, lw["wa_b"], lw["ba_b"], nseq, seq_len, rev=True)
    if lat:
        y_c = _lat_attention(p2d, lw["sink"], ctx["k"], ctx["v"], ctx["cos"], ctx["sin"], nseq, seq_len)
    else:
        y_c = _ctx_attention(p2d, lw["sink"], nseq, seq_len)
    x1 = _merge(x2d, p2d, o_f, o_b, y_c, mod3, lw["gn"], lw["conv_w"], lw["wb"], lw["wo"],
                lw["ln1_g"], lw["ln1_b"], seq_len, lat)
    return x1, p2d, s_f, s_b


def _layer_weights(l, w_in, w_gla_a2, b_gla_a, gla_norm_g, conv_w, attn_sink, w_branch, w_out,
                   ln1_g, ln1_b, w_pq, peer_keys, uv, ln2_g, ln2_b):
    w = w_in[l]
    w_in_p = jnp.concatenate(
        [w[:, O_GQ:O_GA], w[:, O_CH:O_AK], w[:, O_MG:N_IN], w[:, O_AK:O_MG], w[:, O_GA:O_CH],
         jnp.zeros((D_MODEL, N_P - N_IN), F32)], axis=1).astype(BF16)

    def wa_pad(d):
        return jnp.zeros((LANE, 512), F32).at[GLA_LR * d:GLA_LR * (d + 1)].set(w_gla_a2[l, d])

    return {
        "w_in": w_in_p,
        "wa_f": wa_pad(0), "wa_b": wa_pad(1),
        "ba_f": b_gla_a[l, 0].reshape(1, 512), "ba_b": b_gla_a[l, 1].reshape(1, 512),
        "gn": gla_norm_g[l].reshape(1, GLA_DV),
        "conv_w": jnp.zeros((SUBLANE, 1024), F32).at[0:CONV_K].set(conv_w[l]),
        "sink": attn_sink[l],
        "wb": w_branch[l].astype(BF16), "wo": w_out[l].astype(BF16),
        "ln1_g": ln1_g[l].reshape(1, 1024), "ln1_b": ln1_b[l].reshape(1, 1024),
        "w_pq": w_pq[l].astype(BF16),
        "keys": peer_keys[l].astype(BF16),
        "uv": uv, "layer": l,
        "ln2_g": ln2_g[l].reshape(1, 1024), "ln2_b": ln2_b[l].reshape(1, 1024),
    }


def kernel(x_prompt, x_sample, cache_k, cache_v, state_gla, c, c_ctx, ln_in_g, ln_in_b, w_mod, b_mod, w_in, w_gla_a2, b_gla_a, gla_norm_g, conv_w, attn_sink, w_branch, w_out, ln1_g, ln1_b, w_pq, peer_keys, peer_u, peer_v, ln2_g, ln2_b):
    batch, seq, _ = x_prompt.shape
    dec_batch, dec_seq, _ = x_sample.shape
    past = cache_k.shape[2]
    assert dec_batch + 1 <= 8 and seq % TM == 0 and dec_seq % TM_IN == 0 and seq % GLA_BLK == 0

    cond8 = jnp.zeros((8, D_MODEL), F32).at[0].set(c_ctx).at[1:1 + dec_batch].set(c)
    mod = _modulation(cond8, w_mod, b_mod)
    cos_t, sin_t = _rope_tables(dec_seq)

    xp = _ln_in(x_prompt.reshape(batch * seq, D_MODEL), ln_in_g, ln_in_b)
    xs = _ln_in(x_sample.reshape(dec_batch * dec_seq, D_MODEL), ln_in_g, ln_in_b)
    zeros_state = jnp.zeros((batch, GLA_HEADS, GLA_DK, GLA_DV), F32)
    ks, vs, ss = [], [], []
    pending = None
    slabs = (DEPTH, PEER_EXPERTS, SUBLANE, LANE)
    uv = jnp.stack([peer_u.reshape(slabs), peer_v.reshape(slabs)], axis=2)
    uv = uv.reshape((DEPTH * PEER_EXPERTS, 2, SUBLANE, LANE))
    for l in range(DEPTH):
        lw = _layer_weights(l, w_in, w_gla_a2, b_gla_a, gla_norm_g, conv_w, attn_sink, w_branch, w_out,
                            ln1_g, ln1_b, w_pq, peer_keys, uv, ln2_g, ln2_b)
        mod3 = mod[l].reshape(8, 1, 6 * D_MODEL)
        x1c, p_ctx, s_f, s_b = _mixer_half(xp, mod3, lw, batch, seq, False, zeros_state, zeros_state, None)
        ks.append(p_ctx[:, P_AK:P_AK + 256].reshape(batch, seq, ATT_KV_HEADS, ATT_HD))
        vs.append(p_ctx[:, P_AV:P_AV + 256].reshape(batch, seq, ATT_KV_HEADS, ATT_HD))
        ss.append(jnp.stack([s_f, s_b], axis=1))
        if pending is None:
            e_t, g_t = _peer_topk(x1c, mod3, lw["w_pq"], lw["keys"], seq, False)
            ec, gc = _token_major(e_t, batch * seq), _token_major(g_t, batch * seq)
        else:
            xs, (ec, gc) = _peer_experts(*pending, side=(x1c, mod3, lw["w_pq"], lw["keys"], seq, False))
        ctx = {"k": cache_k[:, l].reshape(dec_batch, past, ATT_KV_HEADS * ATT_HD),
               "v": cache_v[:, l].reshape(dec_batch, past, ATT_KV_HEADS * ATT_HD),
               "cos": cos_t, "sin": sin_t}
        x1l, _, _, _ = _mixer_half(xs, mod3, lw, dec_batch, dec_seq, True,
                                   state_gla[:, l, 0], state_gla[:, l, 1], ctx)
        xp, (el, gl) = _peer_experts(x1c, ec, gc, mod3, lw["uv"], l, lw["ln2_g"], lw["ln2_b"], seq, False,
                                     side=(x1l, mod3, lw["w_pq"], lw["keys"], dec_seq, True))
        pending = (x1l, el, gl, mod3, lw["uv"], l, lw["ln2_g"], lw["ln2_b"], dec_seq, True)
    xs, _ = _peer_experts(*pending)
    return (xp.reshape(batch, seq, D_MODEL), xs.reshape(dec_batch, dec_seq, D_MODEL),
            jnp.stack(ks, axis=1), jnp.stack(vs, axis=1), jnp.stack(ss, axis=1))
```

```python
import functools

import numpy as np
import jax
import jax.numpy as jnp
from jax import lax
from jax.experimental import pallas as pl
from jax.experimental.pallas import tpu as pltpu

F32 = jnp.float32
BF16 = jnp.bfloat16
HIGHEST = lax.Precision.HIGHEST

D_MODEL = 1024
DEPTH = 2
GRID_W = 64
LN_EPS = 1e-6
DN_ALPHA = float((2 * DEPTH) ** 0.25)

GLA_HEADS = 4
GLA_DK = 128
GLA_DV = 256
GLA_LR = 16
GLA_TAU = 16.0
CONV_K = 3
ATT_HD = 64
ATT_HEADS = 16
ATT_KV_HEADS = 4
ATT_GROUP = 4
ATT_BLOCK = 128
ROPE_THETA = 10000.0
PEER_HEADS = 8
PEER_NKEYS = 128
PEER_EXPERTS = PEER_NKEYS * PEER_NKEYS
PEER_TOPK = 16
PEER_DQ = 256

O_GQ, O_GK, O_GV, O_GG, O_GA = 0, 512, 1024, 2048, 3072
O_CH, O_CB, O_CC, O_AQ, O_AK, O_AV, O_MG = 3104, 4128, 5152, 6176, 7200, 7456, 7712
N_IN = 10784
P_GQ, P_GK, P_GV, P_GG = 0, 512, 1024, 2048
P_CH, P_CB, P_CC, P_AQ, P_MG = 3072, 4096, 5120, 6144, 7168
P_AK, P_AV, P_GA = 10240, 10496, 10752
N_P = 10880

LANE = 128
SUBLANE = 8
VMEM_LIMIT = 56 * 1024 * 1024

TM = 256
TM_IN = 1024
TN_IN = 2176
GLA_BLK = 256
GLA_LEVELS = 8
GLA_MM_LEVELS = 2
PEER_TOPK_HEADS = 2
PEER_TT = 128
PEER_GROUP = 8
PEER_RING = 4
PEER_AHEAD = 2
PEER_SLOTS = PEER_RING * PEER_GROUP


def _cparams(*sem):
    return pltpu.CompilerParams(dimension_semantics=sem, vmem_limit_bytes=VMEM_LIMIT)


def _layer_norm(x, g, b):
    mu = jnp.mean(x, axis=-1, keepdims=True)
    xc = x - mu
    var = jnp.mean(xc * xc, axis=-1, keepdims=True)
    return xc * lax.rsqrt(var + LN_EPS) * g + b


def _nt_dot(a, b):
    return lax.dot_general(a, b, (((1,), (1,)), ((), ())), preferred_element_type=F32)


def _mod_kernel(c_ref, w_ref, b_ref, o_ref):
    c = c_ref[...]
    s = c * jax.nn.sigmoid(c)
    o_ref[0] = jnp.dot(s, w_ref[0], precision=HIGHEST, preferred_element_type=F32) + b_ref[0]


def _modulation(cond8, w_mod, b_mod):
    tn = 1536
    return pl.pallas_call(
        _mod_kernel,
        grid=(DEPTH, 6 * D_MODEL // tn),
        in_specs=[pl.BlockSpec((8, D_MODEL), lambda l, j: (0, 0)),
                  pl.BlockSpec((1, D_MODEL, tn), lambda l, j: (l, 0, j)),
                  pl.BlockSpec((1, 1, tn), lambda l, j: (l, 0, j))],
        out_specs=pl.BlockSpec((1, 8, tn), lambda l, j: (l, 0, j)),
        out_shape=jax.ShapeDtypeStruct((DEPTH, 8, 6 * D_MODEL), F32),
        compiler_params=_cparams("arbitrary", "arbitrary"),
        name="ada_modulation",
    )(cond8, w_mod, b_mod.reshape(DEPTH, 1, 6 * D_MODEL))


def _mod_spec(piece, row_fn):
    return pl.BlockSpec((None, 1, D_MODEL), lambda *ids: (row_fn(*ids), 0, piece))


def _ln_kernel(x_ref, g_ref, b_ref, o_ref):
    o_ref[...] = _layer_norm(x_ref[...], g_ref[...], b_ref[...])


def _ln_in(x2d, g, b):
    nt = x2d.shape[0]
    return pl.pallas_call(
        _ln_kernel,
        grid=(nt // TM_IN,),
        in_specs=[pl.BlockSpec((TM_IN, D_MODEL), lambda i: (i, 0)),
                  pl.BlockSpec((1, D_MODEL), lambda i: (0, 0)),
                  pl.BlockSpec((1, D_MODEL), lambda i: (0, 0))],
        out_specs=pl.BlockSpec((TM_IN, D_MODEL), lambda i: (i, 0)),
        out_shape=jax.ShapeDtypeStruct((nt, D_MODEL), F32),
        compiler_params=_cparams("arbitrary"),
        name="ln_in",
    )(x2d, g.reshape(1, D_MODEL), b.reshape(1, D_MODEL))


def _inproj_kernel(x_ref, sh_ref, sc_ref, w_ref, o_ref):
    h = x_ref[...] * (1.0 + sc_ref[...]) + sh_ref[...]
    o_ref[...] = jnp.dot(h.astype(BF16), w_ref[...], preferred_element_type=F32)


def _inproj(x2d, mod3, w_in_all, layer, seq_len, lat):
    nt = x2d.shape[0]
    tiles_per_seq = max(seq_len // TM_IN, 1)
    row = (lambda j, i: 1 + i // tiles_per_seq) if lat else (lambda j, i: 0)
    return pl.pallas_call(
        _inproj_kernel,
        grid=(N_P // TN_IN, nt // TM_IN),
        in_specs=[pl.BlockSpec((TM_IN, D_MODEL), lambda j, i: (i, 0)),
                  _mod_spec(0, row), _mod_spec(1, row),
                  pl.BlockSpec((None, D_MODEL, TN_IN), lambda j, i: (layer, 0, j))],
        out_specs=pl.BlockSpec((TM_IN, TN_IN), lambda j, i: (i, j)),
        out_shape=jax.ShapeDtypeStruct((nt, N_P), F32),
        compiler_params=_cparams("arbitrary", "arbitrary"),
        name="in_projection",
    )(x2d, mod3, mod3, w_in_all)


def _gla_tables(rev):
    n = GLA_BLK
    idx = np.arange(n)
    if rev:
        cum = (idx[None, :] >= idx[:, None]).astype(np.float32)
    else:
        cum = (idx[None, :] <= idx[:, None]).astype(np.float32)
    rows = [cum]
    for k in range(GLA_MM_LEVELS):
        h = 1 << k
        rows.append(cum[(idx & ~(2 * h - 1)) + (h if rev else h - 1)])
    mall = np.concatenate(rows, axis=0)
    t, s = idx[:, None], idx[None, :]
    x = t ^ s
    top = np.where(x > 0, np.floor(np.log2(np.maximum(x, 1))).astype(np.int64) + 1, 0)
    attend = (s >= t) if rev else (s <= t)
    lv = np.where(attend, top, -1).astype(np.int32)
    return jnp.asarray(mall, BF16), jnp.asarray(lv)


def _gla_kernel(rev, q_ref, k_ref, v_ref, ga_ref, wa_ref, ba_ref, mall_ref, lv_ref, s0_ref,
                o_ref, sout_ref, st_ref, t_ref):
    j = pl.program_id(1)
    n = GLA_BLK

    @pl.when(j == 0)
    def _():
        for h in range(GLA_HEADS):
            st_ref[h] = s0_ref[0, h].T

    z = jnp.dot(ga_ref[...], wa_ref[...], precision=HIGHEST, preferred_element_type=F32) + ba_ref[...]
    a = (jnp.minimum(z, 0.0) - jnp.log(1.0 + jnp.exp(-jnp.abs(z)))) * (1.0 / GLA_TAU)
    a_hi = a.astype(BF16)
    r1 = a - a_hi.astype(F32)
    a_mid = r1.astype(BF16)
    a_lo = (r1 - a_mid.astype(F32)).astype(BF16)
    mall = mall_ref[...]
    t_ref[...] = (jnp.dot(mall, a_hi, preferred_element_type=F32)
                  + jnp.dot(mall, a_mid, preferred_element_type=F32)
                  + jnp.dot(mall, a_lo, preferred_element_type=F32))

    lv = lv_ref[...]
    for h in range(GLA_HEADS):
        kc = slice(GLA_DK * h, GLA_DK * (h + 1))
        vc = slice(GLA_DV * h, GLA_DV * (h + 1))
        b = t_ref[0:n, kc]
        btot = t_ref[0:1, kc] if rev else t_ref[n - 1:n, kc]
        q = q_ref[:, kc] * (GLA_DK ** -0.5)
        k = k_ref[:, kc]
        v = v_ref[:, vc]
        att = jnp.where(lv == 0, _nt_dot(q.astype(BF16), k.astype(BF16)), 0.0)
        for lev in range(GLA_LEVELS):
            if lev < GLA_MM_LEVELS:
                pv = t_ref[(lev + 1) * n:(lev + 2) * n, kc]
            else:
                hh = 1 << lev
                pv = jnp.concatenate(
                    [jnp.broadcast_to(t_ref[g0 + (hh if rev else hh - 1):g0 + (hh if rev else hh - 1) + 1, kc],
                                      (2 * hh, GLA_DK)) for g0 in range(0, n, 2 * hh)], axis=0)
            ql = (q * jnp.exp(jnp.minimum(b - pv, 0.0))).astype(BF16)
            kl = (k * jnp.exp(jnp.minimum(pv - b, 0.0))).astype(BF16)
            att = jnp.where(lv == lev + 1, _nt_dot(ql, kl), att)
        st = st_ref[h]
        qi = (q * jnp.exp(b)).astype(BF16)
        o = (jnp.dot(att.astype(BF16), v.astype(BF16), preferred_element_type=F32)
             + _nt_dot(qi, st.astype(BF16)))
        o_ref[:, vc] = o
        kh = (k * jnp.exp(btot - b)).astype(BF16)
        st_ref[h] = st * jnp.exp(btot) + jnp.dot(v.T.astype(BF16), kh, preferred_element_type=F32)

    @pl.when(j == pl.num_programs(1) - 1)
    def _():
        for h in range(GLA_HEADS):
            sout_ref[0, h] = st_ref[h].T


def _gla(p2d, s0, wa_pad, ba, nseq, seq_len, rev):
    nt = p2d.shape[0]
    nblk = seq_len // GLA_BLK
    mall, lv = _gla_tables(rev)

    def rb(b, j):
        return b * nblk + (nblk - 1 - j if rev else j)

    return pl.pallas_call(
        functools.partial(_gla_kernel, rev),
        grid=(nseq, nblk),
        in_specs=[pl.BlockSpec((GLA_BLK, 512), lambda b, j: (rb(b, j), P_GQ // 512)),
                  pl.BlockSpec((GLA_BLK, 512), lambda b, j: (rb(b, j), P_GK // 512)),
                  pl.BlockSpec((GLA_BLK, 1024), lambda b, j: (rb(b, j), P_GV // 1024)),
                  pl.BlockSpec((GLA_BLK, LANE), lambda b, j: (rb(b, j), P_GA // LANE)),
                  pl.BlockSpec((LANE, 512), lambda b, j: (0, 0)),
                  pl.BlockSpec((1, 512), lambda b, j: (0, 0)),
                  pl.BlockSpec(mall.shape, lambda b, j: (0, 0)),
                  pl.BlockSpec(lv.shape, lambda b, j: (0, 0)),
                  pl.BlockSpec((1, GLA_HEADS, GLA_DK, GLA_DV), lambda b, j: (b, 0, 0, 0))],
        out_specs=[pl.BlockSpec((GLA_BLK, 1024), lambda b, j: (rb(b, j), 0)),
                   pl.BlockSpec((1, GLA_HEADS, GLA_DK, GLA_DV), lambda b, j: (b, 0, 0, 0))],
        out_shape=[jax.ShapeDtypeStruct((nt, 1024), F32),
                   jax.ShapeDtypeStruct((nseq, GLA_HEADS, GLA_DK, GLA_DV), F32)],
        scratch_shapes=[pltpu.VMEM((GLA_HEADS, GLA_DV, GLA_DK), F32),
                        pltpu.VMEM(((GLA_MM_LEVELS + 1) * GLA_BLK, 512), F32)],
        compiler_params=_cparams("arbitrary", "arbitrary"),
        name="gla_bwd" if rev else "gla_fwd",
    )(p2d, p2d, p2d, p2d, wa_pad, ba, mall, lv, s0)


def _softmax_sink_heads(q_rows, k_all, v_all, bias, sink_col):
    s = _nt_dot(q_rows.astype(BF16), k_all.astype(BF16))
    if bias is not None:
        s = s + bias
    m = jnp.maximum(jnp.max(s, axis=-1, keepdims=True), sink_col)
    p = jnp.exp(s - m)
    den = jnp.sum(p, axis=-1, keepdims=True) + jnp.exp(sink_col - m)
    o = jnp.dot(p.astype(BF16), v_all.astype(BF16), preferred_element_type=F32)
    return o / den


def _ctx_attn_kernel(sink_ref, q_ref, k_ref, v_ref, o_ref):
    t = q_ref.shape[0]
    rows = lax.broadcasted_iota(jnp.int32, (ATT_GROUP * t, 1), 0)
    for g in range(ATT_KV_HEADS):
        kg = k_ref[:, ATT_HD * g:ATT_HD * (g + 1)]
        vg = v_ref[:, ATT_HD * g:ATT_HD * (g + 1)]
        qs, sink_col = [], jnp.zeros((ATT_GROUP * t, 1), F32)
        for i in range(ATT_GROUP):
            hh = g * ATT_GROUP + i
            qs.append(q_ref[:, ATT_HD * hh:ATT_HD * (hh + 1)] * (ATT_HD ** -0.5))
            sink_col = jnp.where(rows // t == i, sink_ref[hh], sink_col)
        o = _softmax_sink_heads(jnp.concatenate(qs, axis=0), kg, vg, None, sink_col)
        for i in range(ATT_GROUP):
            hh = g * ATT_GROUP + i
            o_ref[:, ATT_HD * hh:ATT_HD * (hh + 1)] = o[i * t:(i + 1) * t]


def _ctx_attention(p2d, sink, nseq, seq_len):
    nt = p2d.shape[0]
    return pl.pallas_call(
        _ctx_attn_kernel,
        grid=(nseq,),
        in_specs=[pl.BlockSpec(memory_space=pltpu.SMEM),
                  pl.BlockSpec((seq_len, 1024), lambda b: (b, P_AQ // 1024)),
                  pl.BlockSpec((seq_len, 256), lambda b: (b, P_AK // 256)),
                  pl.BlockSpec((seq_len, 256), lambda b: (b, P_AV // 256))],
        out_specs=pl.BlockSpec((seq_len, 1024), lambda b: (b, 0)),
        out_shape=jax.ShapeDtypeStruct((nt, 1024), F32),
        compiler_params=_cparams("arbitrary"),
        name="context_attention",
    )(sink, p2d, p2d, p2d)


def _rope(x, cos_t, sin_t):
    even = lax.broadcasted_iota(jnp.int32, x.shape, 1) % 2 == 0
    swapped = jnp.where(even, pltpu.roll(x, LANE - 1, 1), pltpu.roll(x, 1, 1))
    return x * cos_t + swapped * sin_t


def _lat_attn_kernel(sink_ref, q_ref, kp_ref, kc_ref, kn_ref, vp_ref, vc_ref, vn_ref,
                     ck_ref, cv_ref, cq_ref, sq_ref, cp_ref, sp_ref, cn_ref, sn_ref, o_ref):
    n = pl.program_id(1)
    nb = pl.num_programs(1)
    t = ATT_BLOCK
    kblocks = []
    for kref, c_ref, s_ref in ((kp_ref, cp_ref, sp_ref), (kc_ref, cq_ref, sq_ref), (kn_ref, cn_ref, sn_ref)):
        kblocks.append(jnp.concatenate(
            [_rope(kref[:, LANE * c:LANE * (c + 1)], c_ref[...], s_ref[...]) for c in range(2)], axis=1))
    k_loc = jnp.concatenate(kblocks, axis=0)
    v_loc = jnp.concatenate([vp_ref[...], vc_ref[...], vn_ref[...]], axis=0)
    k_all = jnp.concatenate([k_loc, ck_ref[...]], axis=0)
    v_all = jnp.concatenate([v_loc, cv_ref[...]], axis=0)
    tc = ck_ref.shape[0]
    qi = lax.broadcasted_iota(jnp.int32, (t, 3 * t + tc), 0)
    kj = lax.broadcasted_iota(jnp.int32, (t, 3 * t + tc), 1)
    ninf = jnp.float32(-jnp.inf)
    prev_bias = jnp.where(n > 0, jnp.float32(0.0), ninf)
    next_bias = jnp.where(n < nb - 1, jnp.float32(0.0), ninf)
    bias1 = jnp.where(kj < t, jnp.where(kj >= qi, prev_bias, ninf),
                      jnp.where(kj < 2 * t, 0.0,
                                jnp.where(kj < 3 * t, jnp.where(kj - 2 * t <= qi, next_bias, ninf), 0.0)))
    bias = jnp.concatenate([bias1] * ATT_GROUP, axis=0)
    rows = lax.broadcasted_iota(jnp.int32, (ATT_GROUP * t, 1), 0)
    qr = [_rope(q_ref[:, LANE * c:LANE * (c + 1)], cq_ref[...], sq_ref[...]) * (ATT_HD ** -0.5)
          for c in range(ATT_HEADS * ATT_HD // LANE)]
    for g in range(ATT_KV_HEADS):
        qs, sink_col = [], jnp.zeros((ATT_GROUP * t, 1), F32)
        for i in range(ATT_GROUP):
            hh = g * ATT_GROUP + i
            slab = qr[hh // 2]
            qs.append(slab[:, ATT_HD * (hh % 2):ATT_HD * (hh % 2 + 1)])
            sink_col = jnp.where(rows // t == i, sink_ref[hh], sink_col)
        kg = k_all[:, ATT_HD * g:ATT_HD * (g + 1)]
        vg = v_all[:, ATT_HD * g:ATT_HD * (g + 1)]
        o = _softmax_sink_heads(jnp.concatenate(qs, axis=0), kg, vg, bias, sink_col)
        for i in range(ATT_GROUP):
            hh = g * ATT_GROUP + i
            o_ref[:, ATT_HD * hh:ATT_HD * (hh + 1)] = o[i * t:(i + 1) * t]


def _lat_attention(p2d, sink, cache_k, cache_v, cos_t, sin_t, nseq, seq_len):
    nt = p2d.shape[0]
    nb = seq_len // ATT_BLOCK
    tc = cache_k.shape[1]
    cur = lambda b, n: b * nb + n
    prv = lambda b, n: b * nb + jnp.maximum(n - 1, 0)
    nxt = lambda b, n: b * nb + jnp.minimum(n + 1, nb - 1)
    kv = lambda f, col: pl.BlockSpec((ATT_BLOCK, 256), lambda b, n: (f(b, n), col))
    tab = lambda f: pl.BlockSpec((ATT_BLOCK, LANE), lambda b, n: (f(0, n), 0))
    return pl.pallas_call(
        _lat_attn_kernel,
        grid=(nseq, nb),
        in_specs=[pl.BlockSpec(memory_space=pltpu.SMEM),
                  pl.BlockSpec((ATT_BLOCK, 1024), lambda b, n: (cur(b, n), P_AQ // 1024)),
                  kv(prv, P_AK // 256), kv(cur, P_AK // 256), kv(nxt, P_AK // 256),
                  kv(prv, P_AV // 256), kv(cur, P_AV // 256), kv(nxt, P_AV // 256),
                  pl.BlockSpec((None, tc, 256), lambda b, n: (b, 0, 0)),
                  pl.BlockSpec((None, tc, 256), lambda b, n: (b, 0, 0)),
                  tab(cur), tab(cur), tab(prv), tab(prv), tab(nxt), tab(nxt)],
        out_specs=pl.BlockSpec((ATT_BLOCK, 1024), lambda b, n: (cur(b, n), 0)),
        out_shape=jax.ShapeDtypeStruct((nt, 1024), F32),
        compiler_params=_cparams("arbitrary", "arbitrary"),
        name="latent_window_attention",
    )(sink, p2d, p2d, p2d, p2d, p2d, p2d, p2d, cache_k, cache_v,
      cos_t, sin_t, cos_t, sin_t, cos_t, sin_t)


def _rope_tables(seq_len):
    rows = seq_len // GRID_W
    row = jnp.repeat(jnp.arange(rows, dtype=F32), GRID_W)
    col = jnp.tile(jnp.arange(GRID_W, dtype=F32), rows)
    half = ATT_HD // 2
    inv = ROPE_THETA ** (-jnp.arange(0, half, 2, dtype=F32) / half)
    ang = jnp.concatenate([row[:, None] * inv, col[:, None] * inv], -1)
    cos, sin = jnp.cos(ang), jnp.sin(ang)
    cos_t = jnp.tile(jnp.repeat(cos, 2, axis=1), (1, LANE // ATT_HD))
    sign = jnp.tile(jnp.asarray([-1.0, 1.0], F32), ATT_HD // 2)
    sin_t = jnp.tile(jnp.repeat(sin, 2, axis=1) * sign, (1, LANE // ATT_HD))
    return cos_t, sin_t


def _merge_kernel(tiles_per_seq, of_ref, ob_ref, gg_ref, ch_ref, cb_ref, cc_ref,
                  chp_ref, ccp_ref, chn_ref, ccn_ref, yc_ref, mga_ref, mgb_ref, mgc_ref, x_ref, g1_ref,
                  gn_ref, cw_ref, wb_ref, wo_ref, lg_ref, lb_ref, o_ref):
    i = pl.program_id(0)
    tm = x_ref.shape[0]
    gn = gn_ref[...]
    ya = []
    for h in range(GLA_HEADS):
        vc = slice(GLA_DV * h, GLA_DV * (h + 1))
        o = of_ref[:, vc] + ob_ref[:, vc]
        o = o * lax.rsqrt(jnp.mean(o * o, axis=-1, keepdims=True) + LN_EPS) * gn
        gg = gg_ref[:, vc]
        ya.append(o * (gg * jax.nn.sigmoid(gg)))
    ya = jnp.concatenate(ya, axis=1)
    z = cc_ref[...] * ch_ref[...]
    has_prev = (i % tiles_per_seq) != 0
    has_next = (i % tiles_per_seq) != tiles_per_seq - 1
    z_before = jnp.where(has_prev, ccp_ref[SUBLANE - 1:SUBLANE, :] * chp_ref[SUBLANE - 1:SUBLANE, :], 0.0)
    z_after = jnp.where(has_next, ccn_ref[0:1, :] * chn_ref[0:1, :], 0.0)
    r = lax.broadcasted_iota(jnp.int32, z.shape, 0)
    zp = jnp.where(r == 0, z_before, pltpu.roll(z, 1, 0))
    zn = jnp.where(r == tm - 1, z_after, pltpu.roll(z, tm - 1, 0))
    yb = cb_ref[...] * (cw_ref[0:1, :] * zp + cw_ref[1:2, :] * z + cw_ref[2:3, :] * zn)
    merged = jnp.zeros((tm, D_MODEL), F32)
    for bi, (y, mg_ref) in enumerate(((ya, mga_ref), (yb, mgb_ref), (yc_ref[...], mgc_ref))):
        proj = jnp.dot(y.astype(BF16), wb_ref[bi], preferred_element_type=F32)
        merged = merged + jax.nn.sigmoid(mg_ref[...]) * proj
    mix = jnp.dot(merged.astype(BF16), wo_ref[...], preferred_element_type=F32)
    o_ref[...] = _layer_norm(DN_ALPHA * x_ref[...] + g1_ref[...] * mix, lg_ref[...], lb_ref[...])


def _merge(x2d, p2d, o_f, o_b, y_c, mod3, gn, conv_w, wb_all, wo_all, layer, ln_g, ln_b, seq_len, lat):
    nt = x2d.shape[0]
    tps = seq_len // TM
    row = (lambda i: 1 + i // tps) if lat else (lambda i: 0)
    col = lambda c: pl.BlockSpec((TM, 1024), lambda i: (i, c // 1024))
    halo_rows = TM // SUBLANE
    nhalo = nt // SUBLANE
    prev = lambda c: pl.BlockSpec((SUBLANE, 1024), lambda i: (jnp.maximum(i * halo_rows - 1, 0), c // 1024))
    nxt = lambda c: pl.BlockSpec((SUBLANE, 1024),
                                 lambda i: (jnp.minimum((i + 1) * halo_rows, nhalo - 1), c // 1024))
    full = lambda shape: pl.BlockSpec(shape, lambda i: (0,) * len(shape))
    return pl.pallas_call(
        functools.partial(_merge_kernel, tps),
        grid=(nt // TM,),
        in_specs=[pl.BlockSpec((TM, 1024), lambda i: (i, 0)), pl.BlockSpec((TM, 1024), lambda i: (i, 0)),
                  col(P_GG), col(P_CH), col(P_CB), col(P_CC),
                  prev(P_CH), prev(P_CC), nxt(P_CH), nxt(P_CC),
                  pl.BlockSpec((TM, 1024), lambda i: (i, 0)),
                  col(P_MG), col(P_MG + 1024), col(P_MG + 2048),
                  pl.BlockSpec((TM, 1024), lambda i: (i, 0)),
                  _mod_spec(2, row),
                  full((1, GLA_DV)), full((SUBLANE, 1024)),
                  pl.BlockSpec((None, 3, 1024, 1024), lambda i: (layer, 0, 0, 0)),
                  pl.BlockSpec((None, 1024, 1024), lambda i: (layer, 0, 0)),
                  full((1, 1024)), full((1, 1024))],
        out_specs=pl.BlockSpec((TM, 1024), lambda i: (i, 0)),
        out_shape=jax.ShapeDtypeStruct((nt, 1024), F32),
        compiler_params=_cparams("arbitrary"),
        name="merge_ln1",
    )(o_f, o_b, p2d, p2d, p2d, p2d, p2d, p2d, p2d, p2d, y_c, p2d, p2d, p2d, x2d, mod3,
      gn, conv_w, wb_all, wo_all, ln_g, ln_b)


def _sorting_network(n):
    pairs = []
    p = 1
    while p < n:
        k = p
        while k >= 1:
            for j in range(k % p, n - k, 2 * k):
                for i in range(min(k, n - j - k)):
                    if (i + j) // (2 * p) == (i + j + k) // (2 * p):
                        pairs.append((i + j, i + j + k))
            k //= 2
        p *= 2
    return pairs


def _topk_rows_steps(s, kk, out):
    n, m = s.shape
    nslab = n // SUBLANE
    assert nslab == kk
    base = lax.broadcasted_iota(jnp.int32, (SUBLANE, m), 0).astype(F32)
    val = [s[SUBLANE * r:SUBLANE * (r + 1)] for r in range(nslab)]
    idx = [base + float(SUBLANE * r) for r in range(nslab)]
    for ci, (a, b) in enumerate(_sorting_network(nslab)):
        keep = val[a] >= val[b]
        val[a], val[b] = jnp.where(keep, val[a], val[b]), jnp.where(keep, val[b], val[a])
        idx[a], idx[b] = jnp.where(keep, idx[a], idx[b]), jnp.where(keep, idx[b], idx[a])
        if ci % 4 == 3:
            yield
    vals, idxs = [], []
    for r in range(kk):
        top = jnp.max(val[0], axis=0, keepdims=True)
        at = jnp.min(jnp.where(val[0] == top, idx[0], float(n)), axis=0, keepdims=True)
        vals.append(top)
        idxs.append(at)
        pop = idx[0] == at
        depth = kk - 1 - r
        for d in range(depth):
            val[d] = jnp.where(pop, val[d + 1], val[d])
            idx[d] = jnp.where(pop, idx[d + 1], idx[d])
        yield
    out.append((jnp.concatenate(vals, axis=0), jnp.concatenate(idxs, axis=0)))


def _staircase():
    return [(i, PEER_TOPK // (i + 1)) for i in range(PEER_TOPK)]


def _peer_topk_kernel(x_ref, sh_ref, sc_ref, w_ref, keys_ref, pos_ref, eidx_ref, gate_ref):
    h2 = (x_ref[...] * (1.0 + sc_ref[...]) + sh_ref[...]).astype(BF16)
    for hh in range(PEER_TOPK_HEADS):
        def store(e, g, hh=hh):
            eidx_ref[hh], gate_ref[hh] = e, g

        pq_fn = lambda hh=hh: jnp.dot(h2, w_ref[:, PEER_DQ * hh:PEER_DQ * (hh + 1)], preferred_element_type=F32)
        for _ in _peer_topk_head_steps(pq_fn, keys_ref[hh], pos_ref[...], store):
            pass


def _peer_topk_head_steps(pq_fn, keys, pos, store):
    half = PEER_DQ // 2
    pq = pq_fn()
    yield
    tops = []
    for p in range(2):
        s = _nt_dot(keys[p], pq[:, half * p:half * (p + 1)].astype(BF16))
        yield
        yield from _topk_rows_steps(s, PEER_TOPK, tops)
    (s1, i1), (s2, i2) = tops
    tm = s1.shape[1]
    kk = PEER_TOPK
    npad = pos.shape[0] - sum(nj for _, nj in _staircase())
    cand = jnp.concatenate([s1[i:i + 1] + s2[0:nj] for i, nj in _staircase()]
                           + [jnp.full((npad, tm), -jnp.inf, F32)], axis=0)
    cidx = jnp.concatenate([i1[i:i + 1] * float(PEER_NKEYS) + i2[0:nj] for i, nj in _staircase()]
                           + [jnp.zeros((npad, tm), F32)], axis=0)
    yield
    top_s, top_e = [], []
    for _ in range(kk):
        m = jnp.max(cand, axis=0, keepdims=True)
        at = jnp.min(jnp.where(cand == m, pos, float(2 * kk * kk)), axis=0, keepdims=True)
        hit = pos == at
        top_s.append(m)
        top_e.append(jnp.sum(jnp.where(hit, cidx, 0.0), axis=0, keepdims=True))
        cand = jnp.where(hit, -jnp.inf, cand)
        yield
    top_s = jnp.concatenate(top_s, axis=0)
    e = jnp.exp(top_s - top_s[0:1])
    store(jnp.concatenate(top_e, axis=0).astype(jnp.int32), e / jnp.sum(e, axis=0, keepdims=True))


TOPK_HEAD_STEPS = 1 + 2 * (1 + 16 + PEER_TOPK) + 1 + PEER_TOPK


def _topk_pos(tm):
    flat = [i * PEER_TOPK + j for i, nj in _staircase() for j in range(nj)]
    nrows = -(-len(flat) // SUBLANE) * SUBLANE
    flat = flat + [PEER_TOPK * PEER_TOPK] * (nrows - len(flat))
    return jnp.asarray(np.repeat(np.asarray(flat, np.float32)[:, None], tm, axis=1))


def _token_major(t, nt):
    return t.reshape(PEER_HEADS * PEER_TOPK, nt).T


def _peer_topk(x2d, mod3, w_pq_all, keys_all, layer, seq_len, lat):
    nt = x2d.shape[0]
    tps = seq_len // TM
    row = (lambda i, h: 1 + i // tps) if lat else (lambda i, h: 0)
    pos = _topk_pos(TM)
    nrows = pos.shape[0]
    return pl.pallas_call(
        _peer_topk_kernel,
        grid=(nt // TM, PEER_HEADS // PEER_TOPK_HEADS),
        in_specs=[pl.BlockSpec((TM, 1024), lambda i, h: (i, 0)),
                  _mod_spec(3, row), _mod_spec(4, row),
                  pl.BlockSpec((None, 1024, PEER_TOPK_HEADS * PEER_DQ), lambda i, h: (layer, 0, h)),
                  pl.BlockSpec((None, PEER_TOPK_HEADS, 2, PEER_NKEYS, PEER_DQ // 2),
                               lambda i, h: (layer, h, 0, 0, 0)),
                  pl.BlockSpec((nrows, TM), lambda i, h: (0, 0))],
        out_specs=[pl.BlockSpec((PEER_TOPK_HEADS, PEER_TOPK, TM), lambda i, h: (h, 0, i)),
                   pl.BlockSpec((PEER_TOPK_HEADS, PEER_TOPK, TM), lambda i, h: (h, 0, i))],
        out_shape=[jax.ShapeDtypeStruct((PEER_HEADS, PEER_TOPK, nt), jnp.int32),
                   jax.ShapeDtypeStruct((PEER_HEADS, PEER_TOPK, nt), F32)],
        compiler_params=_cparams("arbitrary", "arbitrary"),
        name="peer_topk",
    )(x2d, mod3, mod3, w_pq_all, keys_all, pos)


def _peer_expert_kernel(*refs, side):
    npair = PEER_HEADS * PEER_TOPK
    ngroup = PEER_TT // PEER_GROUP
    nrow = npair * SUBLANE
    (eidx_ref, x_ref, sh_ref, sc_ref, g2_ref, gate_ref, lg_ref, lb_ref, col_ref, exp_ref, uv_hbm), refs = refs[:11], refs[11:]
    if side is not None:
        (xb_ref, shb_ref, scb_ref, wpq_ref, keys_ref, pos_ref), refs = refs[:6], refs[6:]
        (o_ref, eidxb_ref, gateb_ref), refs = refs[:3], refs[3:]
        h2b_ref, refs = refs[-1], refs[:-1]
        h2b_ref[...] = (xb_ref[...] * (1.0 + scb_ref[...]) + shb_ref[...]).astype(BF16)
    else:
        o_ref, refs = refs[0], refs[1:]
    blocks, (sems, ffn_ref) = refs[:PEER_RING], refs[PEER_RING:]

    def ring(slot):
        return blocks[slot // PEER_GROUP].at[slot % PEER_GROUP]

    def slot_copy(slot):
        return pltpu.make_async_copy(uv_hbm.at[pl.ds(0, npair)], ring(slot), sems.at[slot])

    def issue_token(tok, slot):
        for r in range(npair):
            pltpu.make_async_copy(uv_hbm.at[eidx_ref[tok, r]], ring(slot).at[r],
                                  sems.at[slot]).start(priority=r % 2)

    @pl.when(pl.program_id(0) == 0)
    def _():
        for t in range(PEER_AHEAD * PEER_GROUP):
            issue_token(t, t)

    rowid = lax.broadcasted_iota(jnp.int32, (PEER_GROUP, 1), 0)
    own = (lax.broadcasted_iota(jnp.int32, (SUBLANE, nrow), 1) % SUBLANE
           == lax.broadcasted_iota(jnp.int32, (SUBLANE, nrow), 0))
    sc, sh = sc_ref[...], sh_ref[...]

    def side_steps(k):
        per_round = side
        for hh in range(per_round):
            h = k * per_round + hh

            def store(e, g, h=h):
                eidxb_ref[h], gateb_ref[h] = e, g

            pq_fn = lambda h=h: jnp.dot(h2b_ref[...], wpq_ref[h], preferred_element_type=F32)
            yield from _peer_topk_head_steps(pq_fn, keys_ref[h], pos_ref[...], store)

    def ring_round(k, carry):
        steps = side_steps(k) if side is not None else iter(())
        per_issue = 0 if side is None else -(-TOPK_HEAD_STEPS * side // (PEER_RING * PEER_GROUP))
        for q in range(PEER_RING):
            g = k * PEER_RING + q
            base = q * PEER_GROUP
            ahead = ((q + PEER_AHEAD) % PEER_RING) * PEER_GROUP
            r0 = pl.multiple_of(g * PEER_GROUP, PEER_GROUP)
            for jj in range(PEER_GROUP):
                slot_copy(base + jj).wait()
            z8 = jnp.zeros((PEER_GROUP, nrow), F32)
            for jj in range(PEER_GROUP):
                issue_token((g + PEER_AHEAD) * PEER_GROUP + jj, ahead + jj)
                for _ in range(per_issue):
                    next(steps, None)
                h = (x_ref[r0 + jj] * (1.0 + sc) + sh).astype(BF16)
                u = ring(base + jj)[:, 0].reshape(nrow, LANE).astype(BF16)
                y = jnp.where(own, _nt_dot(h, u), 0.0)
                z8 = jnp.where(rowid == jj, jnp.sum(y, axis=0, keepdims=True), z8)
            z_hi = z8.astype(BF16)
            z_lo = (z8 - z_hi.astype(F32)).astype(BF16)
            s8 = (jnp.dot(z_hi, col_ref[...], preferred_element_type=F32)
                  + jnp.dot(z_lo, col_ref[...], preferred_element_type=F32))
            act = 0.5 * s8 * (1.0 + lax.erf(s8 * (2.0 ** -0.5)))
            w8 = (gate_ref[pl.ds(r0, PEER_GROUP), :] * act).astype(BF16)
            wexp = jnp.dot(w8, exp_ref[...], preferred_element_type=F32)
            for jj in range(PEER_GROUP):
                wsel = jnp.where(own, wexp[jj:jj + 1, :], 0.0).astype(BF16)
                v = ring(base + jj)[:, 1].reshape(nrow, LANE).astype(BF16)
                ffn_ref[r0 + jj] = jnp.dot(wsel, v, preferred_element_type=F32)
        for _ in steps:
            pass
        return carry

    lax.fori_loop(0, ngroup // PEER_RING, ring_round, 0)

    @pl.when(pl.program_id(0) == pl.num_programs(0) - 1)
    def _():
        for t in range(PEER_AHEAD * PEER_GROUP):
            slot_copy(t).wait()

    y = DN_ALPHA * x_ref[...] + g2_ref[...] * ffn_ref[...]
    mu = jnp.sum(jnp.sum(y, axis=2, keepdims=True), axis=1, keepdims=True) * (1.0 / D_MODEL)
    yc = y - mu
    var = jnp.sum(jnp.sum(yc * yc, axis=2, keepdims=True), axis=1, keepdims=True) * (1.0 / D_MODEL)
    o_ref[...] = yc * lax.rsqrt(var + LN_EPS) * lg_ref[...] + lb_ref[...]


def _peer_experts(x2d, eidx, gate, mod, uv, layer, ln_g, ln_b, seq_len, lat, side=None):
    nt = x2d.shape[0]
    tps = seq_len // PEER_TT
    row = (lambda i: 1 + i // tps) if lat else (lambda i: 0)
    npair = PEER_HEADS * PEER_TOPK
    nrow = npair * SUBLANE
    slab = (SUBLANE, LANE)
    mod4 = mod.reshape(8, 6, *slab)
    mod_spec = lambda piece: pl.BlockSpec((None, None) + slab, lambda i: (row(i), piece, 0, 0))
    full = lambda shape: pl.BlockSpec(shape, lambda i: (0,) * len(shape))
    collapse = jnp.asarray(np.arange(nrow)[:, None] // SUBLANE == np.arange(npair)[None, :], BF16)
    ntile = nt // PEER_TT
    nahead = PEER_AHEAD * PEER_GROUP
    tiles = eidx.reshape(ntile, PEER_TT, npair) + layer * PEER_EXPERTS
    nxt = jnp.concatenate([tiles[1:, :nahead], tiles[-1:, :nahead]], axis=0)
    eidx_ext = jnp.concatenate([tiles, nxt], axis=1)
    assert (PEER_TT // PEER_GROUP) % PEER_RING == 0 and PEER_AHEAD < PEER_RING
    in_specs = [pl.BlockSpec((None, PEER_TT + nahead, npair), lambda i: (i, 0, 0), memory_space=pltpu.SMEM),
                pl.BlockSpec((PEER_TT,) + slab, lambda i: (i, 0, 0)),
                mod_spec(3), mod_spec(4), mod_spec(5),
                pl.BlockSpec((PEER_TT, npair), lambda i: (i, 0)),
                full(slab), full(slab), full((nrow, npair)), full((npair, nrow)),
                pl.BlockSpec(memory_space=pl.ANY)]
    args = [eidx_ext, x2d.reshape((nt,) + slab), mod4, mod4, mod4, gate, ln_g.reshape(slab), ln_b.reshape(slab),
            collapse, collapse.T, uv]
    out_specs = [pl.BlockSpec((PEER_TT,) + slab, lambda i: (i, 0, 0))]
    out_shape = [jax.ShapeDtypeStruct((nt,) + slab, F32)]
    scratch = ([pltpu.VMEM((PEER_GROUP, npair, 2) + slab, F32)] * PEER_RING
               + [pltpu.SemaphoreType.DMA((PEER_SLOTS,)), pltpu.VMEM((PEER_TT,) + slab, F32)])
    side_cfg = None
    if side is not None:
        xb, mod3_b, w_heads_all, keys_all, layer_b, seq_b, lat_b = side
        ntb = xb.shape[0]
        nround = PEER_TT // PEER_GROUP // PEER_RING
        tmb = next(t for t in (TM, TM // 2) if (PEER_HEADS * ntb) % (t * ntile * nround) == 0)
        heads_per_step = PEER_HEADS * (ntb // tmb) // ntile
        steps_per_tile = PEER_HEADS // heads_per_step
        side_cfg = heads_per_step // nround
        assert heads_per_step * steps_per_tile == PEER_HEADS and side_cfg * nround == heads_per_step
        tile_b = lambda i: i // steps_per_tile
        part_b = lambda i: i % steps_per_tile
        tps_b = seq_b // tmb
        row_b = (lambda i: 1 + tile_b(i) // tps_b) if lat_b else (lambda i: 0)
        pos = _topk_pos(tmb)
        in_specs += [pl.BlockSpec((tmb, D_MODEL), lambda i: (tile_b(i), 0)),
                     _mod_spec(3, row_b), _mod_spec(4, row_b),
                     pl.BlockSpec((None, heads_per_step, D_MODEL, PEER_DQ), lambda i: (layer_b, part_b(i), 0, 0)),
                     pl.BlockSpec((None, heads_per_step, 2, PEER_NKEYS, PEER_DQ // 2),
                                  lambda i: (layer_b, part_b(i), 0, 0, 0)),
                     full(pos.shape)]
        args += [xb, mod3_b, mod3_b, w_heads_all, keys_all, pos]
        out_specs += [pl.BlockSpec((heads_per_step, PEER_TOPK, tmb), lambda i: (part_b(i), 0, tile_b(i)))] * 2
        out_shape += [jax.ShapeDtypeStruct((PEER_HEADS, PEER_TOPK, ntb), jnp.int32),
                      jax.ShapeDtypeStruct((PEER_HEADS, PEER_TOPK, ntb), F32)]
        scratch += [pltpu.VMEM((tmb, D_MODEL), BF16)]
    outs = pl.pallas_call(
        functools.partial(_peer_expert_kernel, side=side_cfg),
        grid=(ntile,),
        in_specs=in_specs,
        out_specs=out_specs,
        out_shape=out_shape,
        scratch_shapes=scratch,
        compiler_params=_cparams("arbitrary"),
        name="peer_experts_ln2" if side is None else "peer_experts_ln2_topk",
    )(*args)
    x2 = outs[0].reshape(nt, D_MODEL)
    if side is None:
        return x2, None
    return x2, (_token_major(outs[1], ntb), _token_major(outs[2], ntb))


def _mixer_half(x2d, mod3, lw, nseq, seq_len, lat, s0_f, s0_b, ctx):
    p2d = _inproj(x2d, mod3, lw["w_in"], lw["layer"], seq_len, lat)
    o_f, s_f = _gla(p2d, s0_f, lw["wa_f"], lw["ba_f"], nseq, seq_len, rev=False)
    o_b, s_b = _gla(p2d, s0_b, lw["wa_b"], lw["ba_b"], nseq, seq_len, rev=True)
    if lat:
        y_c = _lat_attention(p2d, lw["sink"], ctx["k"], ctx["v"], ctx["cos"], ctx["sin"], nseq, seq_len)
    else:
        y_c = _ctx_attention(p2d, lw["sink"], nseq, seq_len)
    x1 = _merge(x2d, p2d, o_f, o_b, y_c, mod3, lw["gn"], lw["conv_w"], lw["wb"], lw["wo"], lw["layer"],
                lw["ln1_g"], lw["ln1_b"], seq_len, lat)
    return x1, p2d, s_f, s_b


def _stacked_weights(w_in, w_branch, w_out, w_pq, peer_keys, peer_u, peer_v):
    w_in_p = jnp.concatenate(
        [w_in[:, :, O_GQ:O_GA], w_in[:, :, O_CH:O_AK], w_in[:, :, O_MG:N_IN], w_in[:, :, O_AK:O_MG],
         w_in[:, :, O_GA:O_CH], jnp.zeros((DEPTH, D_MODEL, N_P - N_IN), F32)], axis=2).astype(BF16)
    w_pq_b = w_pq.astype(BF16)
    slabs = (DEPTH, PEER_EXPERTS, SUBLANE, LANE)
    uv = jnp.stack([peer_u.reshape(slabs), peer_v.reshape(slabs)], axis=2)
    return {
        "w_in": w_in_p, "wb": w_branch.astype(BF16), "wo": w_out.astype(BF16), "w_pq": w_pq_b,
        "w_heads": w_pq_b.reshape(DEPTH, D_MODEL, PEER_HEADS, PEER_DQ).transpose(0, 2, 1, 3),
        "keys": peer_keys.astype(BF16),
        "uv": uv.reshape((DEPTH * PEER_EXPERTS, 2, SUBLANE, LANE)),
    }


def _layer_weights(l, sw, w_gla_a2, b_gla_a, gla_norm_g, conv_w, attn_sink, ln1_g, ln1_b, ln2_g, ln2_b):
    def wa_pad(d):
        return jnp.zeros((LANE, 512), F32).at[GLA_LR * d:GLA_LR * (d + 1)].set(w_gla_a2[l, d])

    return {
        "w_in": sw["w_in"],
        "wa_f": wa_pad(0), "wa_b": wa_pad(1),
        "ba_f": b_gla_a[l, 0].reshape(1, 512), "ba_b": b_gla_a[l, 1].reshape(1, 512),
        "gn": gla_norm_g[l].reshape(1, GLA_DV),
        "conv_w": jnp.zeros((SUBLANE, 1024), F32).at[0:CONV_K].set(conv_w[l]),
        "sink": attn_sink[l],
        "wb": sw["wb"], "wo": sw["wo"],
        "ln1_g": ln1_g[l].reshape(1, 1024), "ln1_b": ln1_b[l].reshape(1, 1024),
        "w_pq": sw["w_pq"], "w_heads": sw["w_heads"], "keys": sw["keys"],
        "uv": sw["uv"], "layer": l,
        "ln2_g": ln2_g[l].reshape(1, 1024), "ln2_b": ln2_b[l].reshape(1, 1024),
    }


def kernel(x_prompt, x_sample, cache_k, cache_v, state_gla, c, c_ctx, ln_in_g, ln_in_b, w_mod, b_mod, w_in, w_gla_a2, b_gla_a, gla_norm_g, conv_w, attn_sink, w_branch, w_out, ln1_g, ln1_b, w_pq, peer_keys, peer_u, peer_v, ln2_g, ln2_b):
    batch, seq, _ = x_prompt.shape
    dec_batch, dec_seq, _ = x_sample.shape
    past = cache_k.shape[2]
    assert dec_batch + 1 <= 8 and seq % TM == 0 and dec_seq % TM_IN == 0 and seq % GLA_BLK == 0

    cond8 = jnp.zeros((8, D_MODEL), F32).at[0].set(c_ctx).at[1:1 + dec_batch].set(c)
    mod = _modulation(cond8, w_mod, b_mod)
    cos_t, sin_t = _rope_tables(dec_seq)

    xp = _ln_in(x_prompt.reshape(batch * seq, D_MODEL), ln_in_g, ln_in_b)
    xs = _ln_in(x_sample.reshape(dec_batch * dec_seq, D_MODEL), ln_in_g, ln_in_b)
    zeros_state = jnp.zeros((batch, GLA_HEADS, GLA_DK, GLA_DV), F32)
    ks, vs, ss = [], [], []
    pending = None
    sw = _stacked_weights(w_in, w_branch, w_out, w_pq, peer_keys, peer_u, peer_v)
    for l in range(DEPTH):
        lw = _layer_weights(l, sw, w_gla_a2, b_gla_a, gla_norm_g, conv_w, attn_sink, ln1_g, ln1_b, ln2_g, ln2_b)
        mod3 = mod[l].reshape(8, 1, 6 * D_MODEL)
        x1c, p_ctx, s_f, s_b = _mixer_half(xp, mod3, lw, batch, seq, False, zeros_state, zeros_state, None)
        ks.append(p_ctx[:, P_AK:P_AK + 256].reshape(batch, seq, ATT_KV_HEADS, ATT_HD))
        vs.append(p_ctx[:, P_AV:P_AV + 256].reshape(batch, seq, ATT_KV_HEADS, ATT_HD))
        ss.append(jnp.stack([s_f, s_b], axis=1))
        if pending is None:
            e_t, g_t = _peer_topk(x1c, mod3, lw["w_pq"], lw["keys"], l, seq, False)
            ec, gc = _token_major(e_t, batch * seq), _token_major(g_t, batch * seq)
        else:
            xs, (ec, gc) = _peer_experts(*pending, side=(x1c, mod3, lw["w_heads"], lw["keys"], l, seq, False))
        ctx = {"k": cache_k[:, l].reshape(dec_batch, past, ATT_KV_HEADS * ATT_HD),
               "v": cache_v[:, l].reshape(dec_batch, past, ATT_KV_HEADS * ATT_HD),
               "cos": cos_t, "sin": sin_t}
        x1l, _, _, _ = _mixer_half(xs, mod3, lw, dec_batch, dec_seq, True,
                                   state_gla[:, l, 0], state_gla[:, l, 1], ctx)
        xp, (el, gl) = _peer_experts(x1c, ec, gc, mod3, lw["uv"], l, lw["ln2_g"], lw["ln2_b"], seq, False,
                                     side=(x1l, mod3, lw["w_heads"], lw["keys"], l, dec_seq, True))
        pending = (x1l, el, gl, mod3, lw["uv"], l, lw["ln2_g"], lw["ln2_b"], dec_seq, True)
    xs, _ = _peer_experts(*pending)
    return (xp.reshape(batch, seq, D_MODEL), xs.reshape(dec_batch, dec_seq, D_MODEL),
            jnp.stack(ks, axis=1), jnp.stack(vs, axis=1), jnp.stack(ss, axis=1))
```

```python
import functools

import numpy as np
import jax
import jax.numpy as jnp
from jax import lax
from jax.experimental import pallas as pl
from jax.experimental.pallas import tpu as pltpu

F32 = jnp.float32
BF16 = jnp.bfloat16
HIGHEST = lax.Precision.HIGHEST

D_MODEL = 1024
DEPTH = 2
GRID_W = 64
LN_EPS = 1e-6
DN_ALPHA = float((2 * DEPTH) ** 0.25)

GLA_HEADS = 4
GLA_DK = 128
GLA_DV = 256
GLA_LR = 16
GLA_TAU = 16.0
CONV_K = 3
ATT_HD = 64
ATT_HEADS = 16
ATT_KV_HEADS = 4
ATT_GROUP = 4
ATT_BLOCK = 128
ROPE_THETA = 10000.0
PEER_HEADS = 8
PEER_NKEYS = 128
PEER_EXPERTS = PEER_NKEYS * PEER_NKEYS
PEER_TOPK = 16
PEER_DQ = 256

O_GQ, O_GK, O_GV, O_GG, O_GA = 0, 512, 1024, 2048, 3072
O_CH, O_CB, O_CC, O_AQ, O_AK, O_AV, O_MG = 3104, 4128, 5152, 6176, 7200, 7456, 7712
N_IN = 10784
P_GQ, P_GK, P_GV, P_GG = 0, 512, 1024, 2048
P_CH, P_CB, P_CC, P_AQ, P_MG = 3072, 4096, 5120, 6144, 7168
P_AK, P_AV, P_GA = 10240, 10496, 10752
N_P = 10880

LANE = 128
SUBLANE = 8
VMEM_LIMIT = 56 * 1024 * 1024

TM = 256
TM_IN = 1024
TN_IN = 2176
GLA_BLK = 256
GLA_LEVELS = 8
GLA_MM_LEVELS = 2
PEER_TOPK_HEADS = 2
PEER_TT = 128
PEER_GROUP = 8
PEER_RING = 4
PEER_AHEAD = 2
PEER_SLOTS = PEER_RING * PEER_GROUP


def _cparams(*sem):
    return pltpu.CompilerParams(dimension_semantics=sem, vmem_limit_bytes=VMEM_LIMIT)


def _layer_norm(x, g, b):
    mu = jnp.mean(x, axis=-1, keepdims=True)
    xc = x - mu
    var = jnp.mean(xc * xc, axis=-1, keepdims=True)
    return xc * lax.rsqrt(var + LN_EPS) * g + b


def _nt_dot(a, b):
    return lax.dot_general(a, b, (((1,), (1,)), ((), ())), preferred_element_type=F32)


def _mod_kernel(c_ref, w_ref, b_ref, o_ref):
    c = c_ref[...]
    s = c * jax.nn.sigmoid(c)
    o_ref[0] = jnp.dot(s, w_ref[0], precision=HIGHEST, preferred_element_type=F32) + b_ref[0]


def _modulation(cond8, w_mod, b_mod):
    tn = 1536
    return pl.pallas_call(
        _mod_kernel,
        grid=(DEPTH, 6 * D_MODEL // tn),
        in_specs=[pl.BlockSpec((8, D_MODEL), lambda l, j: (0, 0)),
                  pl.BlockSpec((1, D_MODEL, tn), lambda l, j: (l, 0, j)),
                  pl.BlockSpec((1, 1, tn), lambda l, j: (l, 0, j))],
        out_specs=pl.BlockSpec((1, 8, tn), lambda l, j: (l, 0, j)),
        out_shape=jax.ShapeDtypeStruct((DEPTH, 8, 6 * D_MODEL), F32),
        compiler_params=_cparams("arbitrary", "arbitrary"),
        name="ada_modulation",
    )(cond8, w_mod, b_mod.reshape(DEPTH, 1, 6 * D_MODEL))


def _mod_spec(piece, row_fn):
    return pl.BlockSpec((None, 1, D_MODEL), lambda *ids: (row_fn(*ids), 0, piece))


def _ln_kernel(x_ref, g_ref, b_ref, o_ref):
    o_ref[...] = _layer_norm(x_ref[...], g_ref[...], b_ref[...])


def _ln_in(x2d, g, b):
    nt = x2d.shape[0]
    return pl.pallas_call(
        _ln_kernel,
        grid=(nt // TM_IN,),
        in_specs=[pl.BlockSpec((TM_IN, D_MODEL), lambda i: (i, 0)),
                  pl.BlockSpec((1, D_MODEL), lambda i: (0, 0)),
                  pl.BlockSpec((1, D_MODEL), lambda i: (0, 0))],
        out_specs=pl.BlockSpec((TM_IN, D_MODEL), lambda i: (i, 0)),
        out_shape=jax.ShapeDtypeStruct((nt, D_MODEL), F32),
        compiler_params=_cparams("arbitrary"),
        name="ln_in",
    )(x2d, g.reshape(1, D_MODEL), b.reshape(1, D_MODEL))


def _inproj_kernel(x_ref, sh_ref, sc_ref, w_ref, o_ref):
    h = x_ref[...] * (1.0 + sc_ref[...]) + sh_ref[...]
    o_ref[...] = jnp.dot(h.astype(BF16), w_ref[...], preferred_element_type=F32)


def _inproj(x2d, mod3, w_in_all, layer, seq_len, lat):
    nt = x2d.shape[0]
    tiles_per_seq = max(seq_len // TM_IN, 1)
    row = (lambda j, i: 1 + i // tiles_per_seq) if lat else (lambda j, i: 0)
    return pl.pallas_call(
        _inproj_kernel,
        grid=(N_P // TN_IN, nt // TM_IN),
        in_specs=[pl.BlockSpec((TM_IN, D_MODEL), lambda j, i: (i, 0)),
                  _mod_spec(0, row), _mod_spec(1, row),
                  pl.BlockSpec((None, D_MODEL, TN_IN), lambda j, i: (layer, 0, j))],
        out_specs=pl.BlockSpec((TM_IN, TN_IN), lambda j, i: (i, j)),
        out_shape=jax.ShapeDtypeStruct((nt, N_P), F32),
        compiler_params=_cparams("arbitrary", "arbitrary"),
        name="in_projection",
    )(x2d, mod3, mod3, w_in_all)


def _gla_tables(rev):
    n = GLA_BLK
    idx = np.arange(n)
    if rev:
        cum = (idx[None, :] >= idx[:, None]).astype(np.float32)
    else:
        cum = (idx[None, :] <= idx[:, None]).astype(np.float32)
    rows = [cum]
    for k in range(GLA_MM_LEVELS):
        h = 1 << k
        rows.append(cum[(idx & ~(2 * h - 1)) + (h if rev else h - 1)])
    mall = np.concatenate(rows, axis=0)
    t, s = idx[:, None], idx[None, :]
    x = t ^ s
    top = np.where(x > 0, np.floor(np.log2(np.maximum(x, 1))).astype(np.int64) + 1, 0)
    attend = (s >= t) if rev else (s <= t)
    lv = np.where(attend, top, -1).astype(np.int32)
    return jnp.asarray(mall, BF16), jnp.asarray(lv)


def _gla_kernel(rev, q_ref, k_ref, v_ref, ga_ref, wa_ref, ba_ref, mall_ref, lv_ref, s0_ref,
                o_ref, sout_ref, st_ref, t_ref):
    j = pl.program_id(1)
    n = GLA_BLK

    @pl.when(j == 0)
    def _():
        for h in range(GLA_HEADS):
            st_ref[h] = s0_ref[0, h].T

    z = jnp.dot(ga_ref[...], wa_ref[...], precision=HIGHEST, preferred_element_type=F32) + ba_ref[...]
    a = (jnp.minimum(z, 0.0) - jnp.log(1.0 + jnp.exp(-jnp.abs(z)))) * (1.0 / GLA_TAU)
    a_hi = a.astype(BF16)
    r1 = a - a_hi.astype(F32)
    a_mid = r1.astype(BF16)
    a_lo = (r1 - a_mid.astype(F32)).astype(BF16)
    mall = mall_ref[...]
    t_ref[...] = (jnp.dot(mall, a_hi, preferred_element_type=F32)
                  + jnp.dot(mall, a_mid, preferred_element_type=F32)
                  + jnp.dot(mall, a_lo, preferred_element_type=F32))

    lv = lv_ref[...]
    for h in range(GLA_HEADS):
        kc = slice(GLA_DK * h, GLA_DK * (h + 1))
        vc = slice(GLA_DV * h, GLA_DV * (h + 1))
        b = t_ref[0:n, kc]
        btot = t_ref[0:1, kc] if rev else t_ref[n - 1:n, kc]
        q = q_ref[:, kc] * (GLA_DK ** -0.5)
        k = k_ref[:, kc]
        v = v_ref[:, vc]
        att = jnp.where(lv == 0, _nt_dot(q.astype(BF16), k.astype(BF16)), 0.0)
        for lev in range(GLA_LEVELS):
            if lev < GLA_MM_LEVELS:
                pv = t_ref[(lev + 1) * n:(lev + 2) * n, kc]
            else:
                hh = 1 << lev
                pv = jnp.concatenate(
                    [jnp.broadcast_to(t_ref[g0 + (hh if rev else hh - 1):g0 + (hh if rev else hh - 1) + 1, kc],
                                      (2 * hh, GLA_DK)) for g0 in range(0, n, 2 * hh)], axis=0)
            ql = (q * jnp.exp(jnp.minimum(b - pv, 0.0))).astype(BF16)
            kl = (k * jnp.exp(jnp.minimum(pv - b, 0.0))).astype(BF16)
            att = jnp.where(lv == lev + 1, _nt_dot(ql, kl), att)
        st = st_ref[h]
        qi = (q * jnp.exp(b)).astype(BF16)
        o = (jnp.dot(att.astype(BF16), v.astype(BF16), preferred_element_type=F32)
             + _nt_dot(qi, st.astype(BF16)))
        o_ref[:, vc] = o
        kh = (k * jnp.exp(btot - b)).astype(BF16)
        st_ref[h] = st * jnp.exp(btot) + jnp.dot(v.T.astype(BF16), kh, preferred_element_type=F32)

    @pl.when(j == pl.num_programs(1) - 1)
    def _():
        for h in range(GLA_HEADS):
            sout_ref[0, h] = st_ref[h].T


def _gla(p2d, s0, wa_pad, ba, nseq, seq_len, rev):
    nt = p2d.shape[0]
    nblk = seq_len // GLA_BLK
    mall, lv = _gla_tables(rev)

    def rb(b, j):
        return b * nblk + (nblk - 1 - j if rev else j)

    return pl.pallas_call(
        functools.partial(_gla_kernel, rev),
        grid=(nseq, nblk),
        in_specs=[pl.BlockSpec((GLA_BLK, 512), lambda b, j: (rb(b, j), P_GQ // 512)),
                  pl.BlockSpec((GLA_BLK, 512), lambda b, j: (rb(b, j), P_GK // 512)),
                  pl.BlockSpec((GLA_BLK, 1024), lambda b, j: (rb(b, j), P_GV // 1024)),
                  pl.BlockSpec((GLA_BLK, LANE), lambda b, j: (rb(b, j), P_GA // LANE)),
                  pl.BlockSpec((LANE, 512), lambda b, j: (0, 0)),
                  pl.BlockSpec((1, 512), lambda b, j: (0, 0)),
                  pl.BlockSpec(mall.shape, lambda b, j: (0, 0)),
                  pl.BlockSpec(lv.shape, lambda b, j: (0, 0)),
                  pl.BlockSpec((1, GLA_HEADS, GLA_DK, GLA_DV), lambda b, j: (b, 0, 0, 0))],
        out_specs=[pl.BlockSpec((GLA_BLK, 1024), lambda b, j: (rb(b, j), 0)),
                   pl.BlockSpec((1, GLA_HEADS, GLA_DK, GLA_DV), lambda b, j: (b, 0, 0, 0))],
        out_shape=[jax.ShapeDtypeStruct((nt, 1024), F32),
                   jax.ShapeDtypeStruct((nseq, GLA_HEADS, GLA_DK, GLA_DV), F32)],
        scratch_shapes=[pltpu.VMEM((GLA_HEADS, GLA_DV, GLA_DK), F32),
                        pltpu.VMEM(((GLA_MM_LEVELS + 1) * GLA_BLK, 512), F32)],
        compiler_params=_cparams("arbitrary", "arbitrary"),
        name="gla_bwd" if rev else "gla_fwd",
    )(p2d, p2d, p2d, p2d, wa_pad, ba, mall, lv, s0)


def _softmax_sink_heads(q_rows, k_all, v_all, bias, sink_col):
    s = _nt_dot(q_rows.astype(BF16), k_all.astype(BF16))
    if bias is not None:
        s = s + bias
    m = jnp.maximum(jnp.max(s, axis=-1, keepdims=True), sink_col)
    p = jnp.exp(s - m)
    den = jnp.sum(p, axis=-1, keepdims=True) + jnp.exp(sink_col - m)
    o = jnp.dot(p.astype(BF16), v_all.astype(BF16), preferred_element_type=F32)
    return o / den


def _ctx_attn_kernel(sink_ref, q_ref, k_ref, v_ref, o_ref):
    t = q_ref.shape[0]
    rows = lax.broadcasted_iota(jnp.int32, (ATT_GROUP * t, 1), 0)
    for g in range(ATT_KV_HEADS):
        kg = k_ref[:, ATT_HD * g:ATT_HD * (g + 1)]
        vg = v_ref[:, ATT_HD * g:ATT_HD * (g + 1)]
        qs, sink_col = [], jnp.zeros((ATT_GROUP * t, 1), F32)
        for i in range(ATT_GROUP):
            hh = g * ATT_GROUP + i
            qs.append(q_ref[:, ATT_HD * hh:ATT_HD * (hh + 1)] * (ATT_HD ** -0.5))
            sink_col = jnp.where(rows // t == i, sink_ref[hh], sink_col)
        o = _softmax_sink_heads(jnp.concatenate(qs, axis=0), kg, vg, None, sink_col)
        for i in range(ATT_GROUP):
            hh = g * ATT_GROUP + i
            o_ref[:, ATT_HD * hh:ATT_HD * (hh + 1)] = o[i * t:(i + 1) * t]


def _ctx_attention(p2d, sink, nseq, seq_len):
    nt = p2d.shape[0]
    return pl.pallas_call(
        _ctx_attn_kernel,
        grid=(nseq,),
        in_specs=[pl.BlockSpec(memory_space=pltpu.SMEM),
                  pl.BlockSpec((seq_len, 1024), lambda b: (b, P_AQ // 1024)),
                  pl.BlockSpec((seq_len, 256), lambda b: (b, P_AK // 256)),
                  pl.BlockSpec((seq_len, 256), lambda b: (b, P_AV // 256))],
        out_specs=pl.BlockSpec((seq_len, 1024), lambda b: (b, 0)),
        out_shape=jax.ShapeDtypeStruct((nt, 1024), F32),
        compiler_params=_cparams("arbitrary"),
        name="context_attention",
    )(sink, p2d, p2d, p2d)


def _rope(x, cos_t, sin_t):
    even = lax.broadcasted_iota(jnp.int32, x.shape, 1) % 2 == 0
    swapped = jnp.where(even, pltpu.roll(x, LANE - 1, 1), pltpu.roll(x, 1, 1))
    return x * cos_t + swapped * sin_t


def _lat_attn_kernel(sink_ref, q_ref, kp_ref, kc_ref, kn_ref, vp_ref, vc_ref, vn_ref,
                     ck_ref, cv_ref, cq_ref, sq_ref, cp_ref, sp_ref, cn_ref, sn_ref, o_ref):
    n = pl.program_id(1)
    nb = pl.num_programs(1)
    t = ATT_BLOCK
    kblocks = []
    for kref, c_ref, s_ref in ((kp_ref, cp_ref, sp_ref), (kc_ref, cq_ref, sq_ref), (kn_ref, cn_ref, sn_ref)):
        kblocks.append(jnp.concatenate(
            [_rope(kref[:, LANE * c:LANE * (c + 1)], c_ref[...], s_ref[...]) for c in range(2)], axis=1))
    k_loc = jnp.concatenate(kblocks, axis=0)
    v_loc = jnp.concatenate([vp_ref[...], vc_ref[...], vn_ref[...]], axis=0)
    k_all = jnp.concatenate([k_loc, ck_ref[...]], axis=0)
    v_all = jnp.concatenate([v_loc, cv_ref[...]], axis=0)
    tc = ck_ref.shape[0]
    qi = lax.broadcasted_iota(jnp.int32, (t, 3 * t + tc), 0)
    kj = lax.broadcasted_iota(jnp.int32, (t, 3 * t + tc), 1)
    ninf = jnp.float32(-jnp.inf)
    prev_bias = jnp.where(n > 0, jnp.float32(0.0), ninf)
    next_bias = jnp.where(n < nb - 1, jnp.float32(0.0), ninf)
    bias1 = jnp.where(kj < t, jnp.where(kj >= qi, prev_bias, ninf),
                      jnp.where(kj < 2 * t, 0.0,
                                jnp.where(kj < 3 * t, jnp.where(kj - 2 * t <= qi, next_bias, ninf), 0.0)))
    bias = jnp.concatenate([bias1] * ATT_GROUP, axis=0)
    rows = lax.broadcasted_iota(jnp.int32, (ATT_GROUP * t, 1), 0)
    qr = [_rope(q_ref[:, LANE * c:LANE * (c + 1)], cq_ref[...], sq_ref[...]) * (ATT_HD ** -0.5)
          for c in range(ATT_HEADS * ATT_HD // LANE)]
    for g in range(ATT_KV_HEADS):
        qs, sink_col = [], jnp.zeros((ATT_GROUP * t, 1), F32)
        for i in range(ATT_GROUP):
            hh = g * ATT_GROUP + i
            slab = qr[hh // 2]
            qs.append(slab[:, ATT_HD * (hh % 2):ATT_HD * (hh % 2 + 1)])
            sink_col = jnp.where(rows // t == i, sink_ref[hh], sink_col)
        kg = k_all[:, ATT_HD * g:ATT_HD * (g + 1)]
        vg = v_all[:, ATT_HD * g:ATT_HD * (g + 1)]
        o = _softmax_sink_heads(jnp.concatenate(qs, axis=0), kg, vg, bias, sink_col)
        for i in range(ATT_GROUP):
            hh = g * ATT_GROUP + i
            o_ref[:, ATT_HD * hh:ATT_HD * (hh + 1)] = o[i * t:(i + 1) * t]


def _lat_attention(p2d, sink, cache_k, cache_v, cos_t, sin_t, nseq, seq_len):
    nt = p2d.shape[0]
    nb = seq_len // ATT_BLOCK
    tc = cache_k.shape[1]
    cur = lambda b, n: b * nb + n
    prv = lambda b, n: b * nb + jnp.maximum(n - 1, 0)
    nxt = lambda b, n: b * nb + jnp.minimum(n + 1, nb - 1)
    kv = lambda f, col: pl.BlockSpec((ATT_BLOCK, 256), lambda b, n: (f(b, n), col))
    tab = lambda f: pl.BlockSpec((ATT_BLOCK, LANE), lambda b, n: (f(0, n), 0))
    return pl.pallas_call(
        _lat_attn_kernel,
        grid=(nseq, nb),
        in_specs=[pl.BlockSpec(memory_space=pltpu.SMEM),
                  pl.BlockSpec((ATT_BLOCK, 1024), lambda b, n: (cur(b, n), P_AQ // 1024)),
                  kv(prv, P_AK // 256), kv(cur, P_AK // 256), kv(nxt, P_AK // 256),
                  kv(prv, P_AV // 256), kv(cur, P_AV // 256), kv(nxt, P_AV // 256),
                  pl.BlockSpec((None, tc, 256), lambda b, n: (b, 0, 0)),
                  pl.BlockSpec((None, tc, 256), lambda b, n: (b, 0, 0)),
                  tab(cur), tab(cur), tab(prv), tab(prv), tab(nxt), tab(nxt)],
        out_specs=pl.BlockSpec((ATT_BLOCK, 1024), lambda b, n: (cur(b, n), 0)),
        out_shape=jax.ShapeDtypeStruct((nt, 1024), F32),
        compiler_params=_cparams("arbitrary", "arbitrary"),
        name="latent_window_attention",
    )(sink, p2d, p2d, p2d, p2d, p2d, p2d, p2d, cache_k, cache_v,
      cos_t, sin_t, cos_t, sin_t, cos_t, sin_t)


def _rope_tables(seq_len):
    rows = seq_len // GRID_W
    row = jnp.repeat(jnp.arange(rows, dtype=F32), GRID_W)
    col = jnp.tile(jnp.arange(GRID_W, dtype=F32), rows)
    half = ATT_HD // 2
    inv = ROPE_THETA ** (-jnp.arange(0, half, 2, dtype=F32) / half)
    ang = jnp.concatenate([row[:, None] * inv, col[:, None] * inv], -1)
    cos, sin = jnp.cos(ang), jnp.sin(ang)
    cos_t = jnp.tile(jnp.repeat(cos, 2, axis=1), (1, LANE // ATT_HD))
    sign = jnp.tile(jnp.asarray([-1.0, 1.0], F32), ATT_HD // 2)
    sin_t = jnp.tile(jnp.repeat(sin, 2, axis=1) * sign, (1, LANE // ATT_HD))
    return cos_t, sin_t


def _merge_kernel(tiles_per_seq, of_ref, ob_ref, gg_ref, ch_ref, cb_ref, cc_ref,
                  chp_ref, ccp_ref, chn_ref, ccn_ref, yc_ref, mga_ref, mgb_ref, mgc_ref, x_ref, g1_ref,
                  gn_ref, cw_ref, wb_ref, wo_ref, lg_ref, lb_ref, o_ref):
    i = pl.program_id(0)
    tm = x_ref.shape[0]
    gn = gn_ref[...]
    ya = []
    for h in range(GLA_HEADS):
        vc = slice(GLA_DV * h, GLA_DV * (h + 1))
        o = of_ref[:, vc] + ob_ref[:, vc]
        o = o * lax.rsqrt(jnp.mean(o * o, axis=-1, keepdims=True) + LN_EPS) * gn
        gg = gg_ref[:, vc]
        ya.append(o * (gg * jax.nn.sigmoid(gg)))
    ya = jnp.concatenate(ya, axis=1)
    z = cc_ref[...] * ch_ref[...]
    has_prev = (i % tiles_per_seq) != 0
    has_next = (i % tiles_per_seq) != tiles_per_seq - 1
    z_before = jnp.where(has_prev, ccp_ref[SUBLANE - 1:SUBLANE, :] * chp_ref[SUBLANE - 1:SUBLANE, :], 0.0)
    z_after = jnp.where(has_next, ccn_ref[0:1, :] * chn_ref[0:1, :], 0.0)
    r = lax.broadcasted_iota(jnp.int32, z.shape, 0)
    zp = jnp.where(r == 0, z_before, pltpu.roll(z, 1, 0))
    zn = jnp.where(r == tm - 1, z_after, pltpu.roll(z, tm - 1, 0))
    yb = cb_ref[...] * (cw_ref[0:1, :] * zp + cw_ref[1:2, :] * z + cw_ref[2:3, :] * zn)
    merged = jnp.zeros((tm, D_MODEL), F32)
    for bi, (y, mg_ref) in enumerate(((ya, mga_ref), (yb, mgb_ref), (yc_ref[...], mgc_ref))):
        proj = jnp.dot(y.astype(BF16), wb_ref[bi], preferred_element_type=F32)
        merged = merged + jax.nn.sigmoid(mg_ref[...]) * proj
    mix = jnp.dot(merged.astype(BF16), wo_ref[...], preferred_element_type=F32)
    o_ref[...] = _layer_norm(DN_ALPHA * x_ref[...] + g1_ref[...] * mix, lg_ref[...], lb_ref[...])


def _merge(x2d, p2d, o_f, o_b, y_c, mod3, gn, conv_w, wb_all, wo_all, layer, ln_g, ln_b, seq_len, lat):
    nt = x2d.shape[0]
    tps = seq_len // TM
    row = (lambda i: 1 + i // tps) if lat else (lambda i: 0)
    col = lambda c: pl.BlockSpec((TM, 1024), lambda i: (i, c // 1024))
    halo_rows = TM // SUBLANE
    nhalo = nt // SUBLANE
    prev = lambda c: pl.BlockSpec((SUBLANE, 1024), lambda i: (jnp.maximum(i * halo_rows - 1, 0), c // 1024))
    nxt = lambda c: pl.BlockSpec((SUBLANE, 1024),
                                 lambda i: (jnp.minimum((i + 1) * halo_rows, nhalo - 1), c // 1024))
    full = lambda shape: pl.BlockSpec(shape, lambda i: (0,) * len(shape))
    return pl.pallas_call(
        functools.partial(_merge_kernel, tps),
        grid=(nt // TM,),
        in_specs=[pl.BlockSpec((TM, 1024), lambda i: (i, 0)), pl.BlockSpec((TM, 1024), lambda i: (i, 0)),
                  col(P_GG), col(P_CH), col(P_CB), col(P_CC),
                  prev(P_CH), prev(P_CC), nxt(P_CH), nxt(P_CC),
                  pl.BlockSpec((TM, 1024), lambda i: (i, 0)),
                  col(P_MG), col(P_MG + 1024), col(P_MG + 2048),
                  pl.BlockSpec((TM, 1024), lambda i: (i, 0)),
                  _mod_spec(2, row),
                  full((1, GLA_DV)), full((SUBLANE, 1024)),
                  pl.BlockSpec((None, 3, 1024, 1024), lambda i: (layer, 0, 0, 0)),
                  pl.BlockSpec((None, 1024, 1024), lambda i: (layer, 0, 0)),
                  full((1, 1024)), full((1, 1024))],
        out_specs=pl.BlockSpec((TM, 1024), lambda i: (i, 0)),
        out_shape=jax.ShapeDtypeStruct((nt, 1024), F32),
        compiler_params=_cparams("arbitrary"),
        name="merge_ln1",
    )(o_f, o_b, p2d, p2d, p2d, p2d, p2d, p2d, p2d, p2d, y_c, p2d, p2d, p2d, x2d, mod3,
      gn, conv_w, wb_all, wo_all, ln_g, ln_b)


def _sorting_network(n):
    pairs = []
    p = 1
    while p < n:
        k = p
        while k >= 1:
            for j in range(k % p, n - k, 2 * k):
                for i in range(min(k, n - j - k)):
                    if (i + j) // (2 * p) == (i + j + k) // (2 * p):
                        pairs.append((i + j, i + j + k))
            k //= 2
        p *= 2
    return pairs


def _topk_rows_steps(s, kk, out):
    n, m = s.shape
    nslab = n // SUBLANE
    assert nslab == kk
    base = lax.broadcasted_iota(jnp.int32, (SUBLANE, m), 0).astype(F32)
    val = [s[SUBLANE * r:SUBLANE * (r + 1)] for r in range(nslab)]
    idx = [base + float(SUBLANE * r) for r in range(nslab)]
    for ci, (a, b) in enumerate(_sorting_network(nslab)):
        keep = val[a] >= val[b]
        val[a], val[b] = jnp.where(keep, val[a], val[b]), jnp.where(keep, val[b], val[a])
        idx[a], idx[b] = jnp.where(keep, idx[a], idx[b]), jnp.where(keep, idx[b], idx[a])
        if ci % 4 == 3:
            yield
    vals, idxs = [], []
    for r in range(kk):
        top = jnp.max(val[0], axis=0, keepdims=True)
        at = jnp.min(jnp.where(val[0] == top, idx[0], float(n)), axis=0, keepdims=True)
        vals.append(top)
        idxs.append(at)
        pop = idx[0] == at
        depth = kk - 1 - r
        for d in range(depth):
            val[d] = jnp.where(pop, val[d + 1], val[d])
            idx[d] = jnp.where(pop, idx[d + 1], idx[d])
        yield
    out.append((jnp.concatenate(vals, axis=0), jnp.concatenate(idxs, axis=0)))


def _staircase():
    return [(i, PEER_TOPK // (i + 1)) for i in range(PEER_TOPK)]


def _peer_topk_kernel(x_ref, sh_ref, sc_ref, w_ref, keys_ref, pos_ref, eidx_ref, gate_ref):
    h2 = (x_ref[...] * (1.0 + sc_ref[...]) + sh_ref[...]).astype(BF16)
    for hh in range(PEER_TOPK_HEADS):
        def store(e, g, hh=hh):
            eidx_ref[hh], gate_ref[hh] = e, g

        pq_fn = lambda hh=hh: jnp.dot(h2, w_ref[:, PEER_DQ * hh:PEER_DQ * (hh + 1)], preferred_element_type=F32)
        for _ in _peer_topk_head_steps(pq_fn, keys_ref[hh], pos_ref[...], store):
            pass


def _peer_topk_head_steps(pq_fn, keys, pos, store):
    half = PEER_DQ // 2
    pq = pq_fn()
    yield
    tops = []
    for p in range(2):
        s = _nt_dot(keys[p], pq[:, half * p:half * (p + 1)].astype(BF16))
        yield
        yield from _topk_rows_steps(s, PEER_TOPK, tops)
    (s1, i1), (s2, i2) = tops
    tm = s1.shape[1]
    kk = PEER_TOPK
    npad = pos.shape[0] - sum(nj for _, nj in _staircase())
    cand = jnp.concatenate([s1[i:i + 1] + s2[0:nj] for i, nj in _staircase()]
                           + [jnp.full((npad, tm), -jnp.inf, F32)], axis=0)
    cidx = jnp.concatenate([i1[i:i + 1] * float(PEER_NKEYS) + i2[0:nj] for i, nj in _staircase()]
                           + [jnp.zeros((npad, tm), F32)], axis=0)
    yield
    top_s, top_e = [], []
    for _ in range(kk):
        m = jnp.max(cand, axis=0, keepdims=True)
        at = jnp.min(jnp.where(cand == m, pos, float(2 * kk * kk)), axis=0, keepdims=True)
        hit = pos == at
        top_s.append(m)
        top_e.append(jnp.sum(jnp.where(hit, cidx, 0.0), axis=0, keepdims=True))
        cand = jnp.where(hit, -jnp.inf, cand)
        yield
    top_s = jnp.concatenate(top_s, axis=0)
    e = jnp.exp(top_s - top_s[0:1])
    store(jnp.concatenate(top_e, axis=0).astype(jnp.int32), e / jnp.sum(e, axis=0, keepdims=True))


TOPK_HEAD_STEPS = 1 + 2 * (1 + 16 + PEER_TOPK) + 1 + PEER_TOPK


def _topk_pos(tm):
    flat = [i * PEER_TOPK + j for i, nj in _staircase() for j in range(nj)]
    nrows = -(-len(flat) // SUBLANE) * SUBLANE
    flat = flat + [PEER_TOPK * PEER_TOPK] * (nrows - len(flat))
    return jnp.asarray(np.repeat(np.asarray(flat, np.float32)[:, None], tm, axis=1))


def _token_major(t, nt):
    return t.reshape(PEER_HEADS * PEER_TOPK, nt).T


def _peer_topk(x2d, mod3, w_pq_all, keys_all, layer, seq_len, lat):
    nt = x2d.shape[0]
    tps = seq_len // TM
    row = (lambda i, h: 1 + i // tps) if lat else (lambda i, h: 0)
    pos = _topk_pos(TM)
    nrows = pos.shape[0]
    return pl.pallas_call(
        _peer_topk_kernel,
        grid=(nt // TM, PEER_HEADS // PEER_TOPK_HEADS),
        in_specs=[pl.BlockSpec((TM, 1024), lambda i, h: (i, 0)),
                  _mod_spec(3, row), _mod_spec(4, row),
                  pl.BlockSpec((None, 1024, PEER_TOPK_HEADS * PEER_DQ), lambda i, h: (layer, 0, h)),
                  pl.BlockSpec((None, PEER_TOPK_HEADS, 2, PEER_NKEYS, PEER_DQ // 2),
                               lambda i, h: (layer, h, 0, 0, 0)),
                  pl.BlockSpec((nrows, TM), lambda i, h: (0, 0))],
        out_specs=[pl.BlockSpec((PEER_TOPK_HEADS, PEER_TOPK, TM), lambda i, h: (h, 0, i)),
                   pl.BlockSpec((PEER_TOPK_HEADS, PEER_TOPK, TM), lambda i, h: (h, 0, i))],
        out_shape=[jax.ShapeDtypeStruct((PEER_HEADS, PEER_TOPK, nt), jnp.int32),
                   jax.ShapeDtypeStruct((PEER_HEADS, PEER_TOPK, nt), F32)],
        compiler_params=_cparams("arbitrary", "arbitrary"),
        name="peer_topk",
    )(x2d, mod3, mod3, w_pq_all, keys_all, pos)


def _peer_expert_kernel(*refs, side):
    npair = PEER_HEADS * PEER_TOPK
    ngroup = PEER_TT // PEER_GROUP
    nrow = npair * SUBLANE
    (eidx_ref, x_ref, sh_ref, sc_ref, g2_ref, gate_ref, lg_ref, lb_ref, col_ref, exp_ref, uv_hbm), refs = refs[:11], refs[11:]
    if side is not None:
        (xb_ref, shb_ref, scb_ref, wpq_ref, keys_ref, pos_ref), refs = refs[:6], refs[6:]
        (o_ref, eidxb_ref, gateb_ref), refs = refs[:3], refs[3:]
        h2b_ref, refs = refs[-1], refs[:-1]
        h2b_ref[...] = (xb_ref[...] * (1.0 + scb_ref[...]) + shb_ref[...]).astype(BF16)
    else:
        o_ref, refs = refs[0], refs[1:]
    blocks, (sems, ffn_ref) = refs[:PEER_RING], refs[PEER_RING:]

    def ring(slot):
        return blocks[slot // PEER_GROUP].at[slot % PEER_GROUP]

    def slot_copy(slot):
        return pltpu.make_async_copy(uv_hbm.at[pl.ds(0, npair)], ring(slot), sems.at[slot])

    def issue_token(tok, slot):
        for r in range(npair):
            pltpu.make_async_copy(uv_hbm.at[eidx_ref[tok, r]], ring(slot).at[r],
                                  sems.at[slot]).start(priority=r % 2)

    @pl.when(pl.program_id(0) == 0)
    def _():
        for t in range(PEER_AHEAD * PEER_GROUP):
            issue_token(t, t)

    rowid = lax.broadcasted_iota(jnp.int32, (PEER_GROUP, 1), 0)
    own = (lax.broadcasted_iota(jnp.int32, (SUBLANE, nrow), 1) % SUBLANE
           == lax.broadcasted_iota(jnp.int32, (SUBLANE, nrow), 0))
    sc, sh = sc_ref[...], sh_ref[...]

    def side_steps(k):
        per_round = side
        for hh in range(per_round):
            h = k * per_round + hh

            def store(e, g, h=h):
                eidxb_ref[h], gateb_ref[h] = e, g

            pq_fn = lambda h=h: jnp.dot(h2b_ref[...], wpq_ref[h], preferred_element_type=F32)
            yield from _peer_topk_head_steps(pq_fn, keys_ref[h], pos_ref[...], store)

    def ring_round(k, carry):
        steps = side_steps(k) if side is not None else iter(())
        per_issue = 0 if side is None else -(-TOPK_HEAD_STEPS * side // (PEER_RING * PEER_GROUP))
        for q in range(PEER_RING):
            g = k * PEER_RING + q
            base = q * PEER_GROUP
            ahead = ((q + PEER_AHEAD) % PEER_RING) * PEER_GROUP
            r0 = pl.multiple_of(g * PEER_GROUP, PEER_GROUP)
            for jj in range(PEER_GROUP):
                slot_copy(base + jj).wait()
            z8 = jnp.zeros((PEER_GROUP, nrow), F32)
            x8 = x_ref[pl.ds(r0, PEER_GROUP), :]
            for jj in range(PEER_GROUP):
                issue_token((g + PEER_AHEAD) * PEER_GROUP + jj, ahead + jj)
                for _ in range(per_issue):
                    next(steps, None)
                xt = jnp.concatenate([x8[jj:jj + 1, LANE * j:LANE * (j + 1)]
                                      for j in range(SUBLANE)], axis=0)
                h = (xt * (1.0 + sc) + sh).astype(BF16)
                u = ring(base + jj)[:, 0].reshape(nrow, LANE).astype(BF16)
                y = jnp.where(own, _nt_dot(h, u), 0.0)
                z8 = jnp.where(rowid == jj, jnp.sum(y, axis=0, keepdims=True), z8)
            z_hi = z8.astype(BF16)
            z_lo = (z8 - z_hi.astype(F32)).astype(BF16)
            s8 = (jnp.dot(z_hi, col_ref[...], preferred_element_type=F32)
                  + jnp.dot(z_lo, col_ref[...], preferred_element_type=F32))
            act = 0.5 * s8 * (1.0 + lax.erf(s8 * (2.0 ** -0.5)))
            w8 = (gate_ref[pl.ds(r0, PEER_GROUP), :] * act).astype(BF16)
            wexp = jnp.dot(w8, exp_ref[...], preferred_element_type=F32)
            rows = []
            for jj in range(PEER_GROUP):
                wsel = jnp.where(own, wexp[jj:jj + 1, :], 0.0).astype(BF16)
                v = ring(base + jj)[:, 1].reshape(nrow, LANE).astype(BF16)
                out = jnp.dot(wsel, v, preferred_element_type=F32)
                rows.append(jnp.concatenate([out[j:j + 1, :] for j in range(SUBLANE)], axis=1))
            ffn_ref[pl.ds(r0, PEER_GROUP), :] = jnp.concatenate(rows, axis=0)
        for _ in steps:
            pass
        return carry

    lax.fori_loop(0, ngroup // PEER_RING, ring_round, 0)

    @pl.when(pl.program_id(0) == pl.num_programs(0) - 1)
    def _():
        for t in range(PEER_AHEAD * PEER_GROUP):
            slot_copy(t).wait()

    o_ref[...] = _layer_norm(DN_ALPHA * x_ref[...] + g2_ref[...] * ffn_ref[...], lg_ref[...], lb_ref[...])


def _peer_experts(x2d, eidx, gate, mod, uv, layer, ln_g, ln_b, seq_len, lat, side=None):
    nt = x2d.shape[0]
    tps = seq_len // PEER_TT
    row = (lambda i: 1 + i // tps) if lat else (lambda i: 0)
    npair = PEER_HEADS * PEER_TOPK
    nrow = npair * SUBLANE
    slab = (SUBLANE, LANE)
    mod4 = mod.reshape(8, 6, *slab)
    mod_spec = lambda piece: pl.BlockSpec((None, None) + slab, lambda i: (row(i), piece, 0, 0))
    full = lambda shape: pl.BlockSpec(shape, lambda i: (0,) * len(shape))
    collapse = jnp.asarray(np.arange(nrow)[:, None] // SUBLANE == np.arange(npair)[None, :], BF16)
    ntile = nt // PEER_TT
    nahead = PEER_AHEAD * PEER_GROUP
    tiles = eidx.reshape(ntile, PEER_TT, npair) + layer * PEER_EXPERTS
    nxt = jnp.concatenate([tiles[1:, :nahead], tiles[-1:, :nahead]], axis=0)
    eidx_ext = jnp.concatenate([tiles, nxt], axis=1)
    assert (PEER_TT // PEER_GROUP) % PEER_RING == 0 and PEER_AHEAD < PEER_RING
    in_specs = [pl.BlockSpec((None, PEER_TT + nahead, npair), lambda i: (i, 0, 0), memory_space=pltpu.SMEM),
                pl.BlockSpec((PEER_TT, D_MODEL), lambda i: (i, 0)),
                mod_spec(3), mod_spec(4), _mod_spec(5, row),
                pl.BlockSpec((PEER_TT, npair), lambda i: (i, 0)),
                full((1, D_MODEL)), full((1, D_MODEL)), full((nrow, npair)), full((npair, nrow)),
                pl.BlockSpec(memory_space=pl.ANY)]
    args = [eidx_ext, x2d, mod4, mod4, mod, gate, ln_g, ln_b, collapse, collapse.T, uv]
    out_specs = [pl.BlockSpec((PEER_TT, D_MODEL), lambda i: (i, 0))]
    out_shape = [jax.ShapeDtypeStruct((nt, D_MODEL), F32)]
    scratch = ([pltpu.VMEM((PEER_GROUP, npair, 2) + slab, F32)] * PEER_RING
               + [pltpu.SemaphoreType.DMA((PEER_SLOTS,)), pltpu.VMEM((PEER_TT, D_MODEL), F32)])
    side_cfg = None
    if side is not None:
        xb, mod3_b, w_heads_all, keys_all, layer_b, seq_b, lat_b = side
        ntb = xb.shape[0]
        nround = PEER_TT // PEER_GROUP // PEER_RING
        tmb = next(t for t in (TM, TM // 2) if (PEER_HEADS * ntb) % (t * ntile * nround) == 0)
        heads_per_step = PEER_HEADS * (ntb // tmb) // ntile
        steps_per_tile = PEER_HEADS // heads_per_step
        side_cfg = heads_per_step // nround
        assert heads_per_step * steps_per_tile == PEER_HEADS and side_cfg * nround == heads_per_step
        ntile_b = ntb // tmb
        tile_b = lambda i: i % ntile_b
        part_b = lambda i: i // ntile_b
        tps_b = seq_b // tmb
        row_b = (lambda i: 1 + tile_b(i) // tps_b) if lat_b else (lambda i: 0)
        pos = _topk_pos(tmb)
        in_specs += [pl.BlockSpec((tmb, D_MODEL), lambda i: (tile_b(i), 0)),
                     _mod_spec(3, row_b), _mod_spec(4, row_b),
                     pl.BlockSpec((None, heads_per_step, D_MODEL, PEER_DQ), lambda i: (layer_b, part_b(i), 0, 0)),
                     pl.BlockSpec((None, heads_per_step, 2, PEER_NKEYS, PEER_DQ // 2),
                                  lambda i: (layer_b, part_b(i), 0, 0, 0)),
                     full(pos.shape)]
        args += [xb, mod3_b, mod3_b, w_heads_all, keys_all, pos]
        out_specs += [pl.BlockSpec((heads_per_step, PEER_TOPK, tmb), lambda i: (part_b(i), 0, tile_b(i)))] * 2
        out_shape += [jax.ShapeDtypeStruct((PEER_HEADS, PEER_TOPK, ntb), jnp.int32),
                      jax.ShapeDtypeStruct((PEER_HEADS, PEER_TOPK, ntb), F32)]
        scratch += [pltpu.VMEM((tmb, D_MODEL), BF16)]
    outs = pl.pallas_call(
        functools.partial(_peer_expert_kernel, side=side_cfg),
        grid=(ntile,),
        in_specs=in_specs,
        out_specs=out_specs,
        out_shape=out_shape,
        scratch_shapes=scratch,
        compiler_params=_cparams("arbitrary"),
        name="peer_experts_ln2" if side is None else "peer_experts_ln2_topk",
    )(*args)
    x2 = outs[0]
    if side is None:
        return x2, None
    return x2, (_token_major(outs[1], ntb), _token_major(outs[2], ntb))


def _mixer_half(x2d, mod3, lw, nseq, seq_len, lat, s0_f, s0_b, ctx):
    p2d = _inproj(x2d, mod3, lw["w_in"], lw["layer"], seq_len, lat)
    o_f, s_f = _gla(p2d, s0_f, lw["wa_f"], lw["ba_f"], nseq, seq_len, rev=False)
    o_b, s_b = _gla(p2d, s0_b, lw["wa_b"], lw["ba_b"], nseq, seq_len, rev=True)
    if lat:
        y_c = _lat_attention(p2d, lw["sink"], ctx["k"], ctx["v"], ctx["cos"], ctx["sin"], nseq, seq_len)
    else:
        y_c = _ctx_attention(p2d, lw["sink"], nseq, seq_len)
    x1 = _merge(x2d, p2d, o_f, o_b, y_c, mod3, lw["gn"], lw["conv_w"], lw["wb"], lw["wo"], lw["layer"],
                lw["ln1_g"], lw["ln1_b"], seq_len, lat)
    return x1, p2d, s_f, s_b


def _stacked_weights(w_in, w_branch, w_out, w_pq, peer_keys, peer_u, peer_v):
    w_in_p = jnp.concatenate(
        [w_in[:, :, O_GQ:O_GA], w_in[:, :, O_CH:O_AK], w_in[:, :, O_MG:N_IN], w_in[:, :, O_AK:O_MG],
         w_in[:, :, O_GA:O_CH], jnp.zeros((DEPTH, D_MODEL, N_P - N_IN), F32)], axis=2).astype(BF16)
    w_pq_b = w_pq.astype(BF16)
    slabs = (DEPTH, PEER_EXPERTS, SUBLANE, LANE)
    uv = jnp.stack([peer_u.reshape(slabs), peer_v.reshape(slabs)], axis=2)
    return {
        "w_in": w_in_p, "wb": w_branch.astype(BF16), "wo": w_out.astype(BF16), "w_pq": w_pq_b,
        "w_heads": w_pq_b.reshape(DEPTH, D_MODEL, PEER_HEADS, PEER_DQ).transpose(0, 2, 1, 3),
        "keys": peer_keys.astype(BF16),
        "uv": uv.reshape((DEPTH * PEER_EXPERTS, 2, SUBLANE, LANE)),
    }


def _layer_weights(l, sw, w_gla_a2, b_gla_a, gla_norm_g, conv_w, attn_sink, ln1_g, ln1_b, ln2_g, ln2_b):
    def wa_pad(d):
        return jnp.zeros((LANE, 512), F32).at[GLA_LR * d:GLA_LR * (d + 1)].set(w_gla_a2[l, d])

    return {
        "w_in": sw["w_in"],
        "wa_f": wa_pad(0), "wa_b": wa_pad(1),
        "ba_f": b_gla_a[l, 0].reshape(1, 512), "ba_b": b_gla_a[l, 1].reshape(1, 512),
        "gn": gla_norm_g[l].reshape(1, GLA_DV),
        "conv_w": jnp.zeros((SUBLANE, 1024), F32).at[0:CONV_K].set(conv_w[l]),
        "sink": attn_sink[l],
        "wb": sw["wb"], "wo": sw["wo"],
        "ln1_g": ln1_g[l].reshape(1, 1024), "ln1_b": ln1_b[l].reshape(1, 1024),
        "w_pq": sw["w_pq"], "w_heads": sw["w_heads"], "keys": sw["keys"],
        "uv": sw["uv"], "layer": l,
        "ln2_g": ln2_g[l].reshape(1, 1024), "ln2_b": ln2_b[l].reshape(1, 1024),
    }


def kernel(x_prompt, x_sample, cache_k, cache_v, state_gla, c, c_ctx, ln_in_g, ln_in_b, w_mod, b_mod, w_in, w_gla_a2, b_gla_a, gla_norm_g, conv_w, attn_sink, w_branch, w_out, ln1_g, ln1_b, w_pq, peer_keys, peer_u, peer_v, ln2_g, ln2_b):
    batch, seq, _ = x_prompt.shape
    dec_batch, dec_seq, _ = x_sample.shape
    past = cache_k.shape[2]
    assert dec_batch + 1 <= 8 and seq % TM == 0 and dec_seq % TM_IN == 0 and seq % GLA_BLK == 0

    cond8 = jnp.zeros((8, D_MODEL), F32).at[0].set(c_ctx).at[1:1 + dec_batch].set(c)
    mod = _modulation(cond8, w_mod, b_mod)
    cos_t, sin_t = _rope_tables(dec_seq)

    xp = _ln_in(x_prompt.reshape(batch * seq, D_MODEL), ln_in_g, ln_in_b)
    xs = _ln_in(x_sample.reshape(dec_batch * dec_seq, D_MODEL), ln_in_g, ln_in_b)
    zeros_state = jnp.zeros((batch, GLA_HEADS, GLA_DK, GLA_DV), F32)
    ks, vs, ss = [], [], []
    pending = None
    sw = _stacked_weights(w_in, w_branch, w_out, w_pq, peer_keys, peer_u, peer_v)
    for l in range(DEPTH):
        lw = _layer_weights(l, sw, w_gla_a2, b_gla_a, gla_norm_g, conv_w, attn_sink, ln1_g, ln1_b, ln2_g, ln2_b)
        mod3 = mod[l].reshape(8, 1, 6 * D_MODEL)
        x1c, p_ctx, s_f, s_b = _mixer_half(xp, mod3, lw, batch, seq, False, zeros_state, zeros_state, None)
        ks.append(p_ctx[:, P_AK:P_AK + 256].reshape(batch, seq, ATT_KV_HEADS, ATT_HD))
        vs.append(p_ctx[:, P_AV:P_AV + 256].reshape(batch, seq, ATT_KV_HEADS, ATT_HD))
        ss.append(jnp.stack([s_f, s_b], axis=1))
        if pending is None:
            e_t, g_t = _peer_topk(x1c, mod3, lw["w_pq"], lw["keys"], l, seq, False)
            ec, gc = _token_major(e_t, batch * seq), _token_major(g_t, batch * seq)
        else:
            xs, (ec, gc) = _peer_experts(*pending, side=(x1c, mod3, lw["w_heads"], lw["keys"], l, seq, False))
        ctx = {"k": cache_k[:, l].reshape(dec_batch, past, ATT_KV_HEADS * ATT_HD),
               "v": cache_v[:, l].reshape(dec_batch, past, ATT_KV_HEADS * ATT_HD),
               "cos": cos_t, "sin": sin_t}
        x1l, _, _, _ = _mixer_half(xs, mod3, lw, dec_batch, dec_seq, True,
                                   state_gla[:, l, 0], state_gla[:, l, 1], ctx)
        xp, (el, gl) = _peer_experts(x1c, ec, gc, mod3, lw["uv"], l, lw["ln2_g"], lw["ln2_b"], seq, False,
                                     side=(x1l, mod3, lw["w_heads"], lw["keys"], l, dec_seq, True))
        pending = (x1l, el, gl, mod3, lw["uv"], l, lw["ln2_g"], lw["ln2_b"], dec_seq, True)
    xs, _ = _peer_experts(*pending)
    return (xp.reshape(batch, seq, D_MODEL), xs.reshape(dec_batch, dec_seq, D_MODEL),
            jnp.stack(ks, axis=1), jnp.stack(vs, axis=1), jnp.stack(ss, axis=1))
```

```python
import functools

import numpy as np
import jax
import jax.numpy as jnp
from jax import lax
from jax.experimental import pallas as pl
from jax.experimental.pallas import tpu as pltpu

F32 = jnp.float32
BF16 = jnp.bfloat16
HIGHEST = lax.Precision.HIGHEST

D_MODEL = 1024
DEPTH = 2
GRID_W = 64
LN_EPS = 1e-6
DN_ALPHA = float((2 * DEPTH) ** 0.25)

GLA_HEADS = 4
GLA_DK = 128
GLA_DV = 256
GLA_LR = 16
GLA_TAU = 16.0
CONV_K = 3
ATT_HD = 64
ATT_HEADS = 16
ATT_KV_HEADS = 4
ATT_GROUP = 4
ATT_BLOCK = 128
ROPE_THETA = 10000.0
PEER_HEADS = 8
PEER_NKEYS = 128
PEER_EXPERTS = PEER_NKEYS * PEER_NKEYS
PEER_TOPK = 16
PEER_DQ = 256

O_GQ, O_GK, O_GV, O_GG, O_GA = 0, 512, 1024, 2048, 3072
O_CH, O_CB, O_CC, O_AQ, O_AK, O_AV, O_MG = 3104, 4128, 5152, 6176, 7200, 7456, 7712
N_IN = 10784
P_GQ, P_GK, P_GV, P_GG = 0, 512, 1024, 2048
P_CH, P_CB, P_CC, P_AQ, P_MG = 3072, 4096, 5120, 6144, 7168
P_AK, P_AV, P_GA = 10240, 10496, 10752
N_P = 10880

LANE = 128
SUBLANE = 8
VMEM_LIMIT = 56 * 1024 * 1024

TM = 256
TM_IN = 1024
TN_IN = 2176
GLA_BLK = 256
GLA_LEVELS = 8
GLA_MM_LEVELS = 2
PEER_TOPK_HEADS = 2
PEER_TT = 128
PEER_GROUP = 8
PEER_RING = 4
PEER_AHEAD = 2
PEER_SLOTS = PEER_RING * PEER_GROUP


def _cparams(*sem):
    return pltpu.CompilerParams(dimension_semantics=sem, vmem_limit_bytes=VMEM_LIMIT)


def _layer_norm(x, g, b):
    mu = jnp.mean(x, axis=-1, keepdims=True)
    xc = x - mu
    var = jnp.mean(xc * xc, axis=-1, keepdims=True)
    return xc * lax.rsqrt(var + LN_EPS) * g + b


def _nt_dot(a, b):
    return lax.dot_general(a, b, (((1,), (1,)), ((), ())), preferred_element_type=F32)


def _mod_kernel(c_ref, w_ref, b_ref, o_ref):
    c = c_ref[...]
    s = c * jax.nn.sigmoid(c)
    o_ref[0] = jnp.dot(s, w_ref[0], precision=HIGHEST, preferred_element_type=F32) + b_ref[0]


def _modulation(cond8, w_mod, b_mod):
    tn = 1536
    return pl.pallas_call(
        _mod_kernel,
        grid=(DEPTH, 6 * D_MODEL // tn),
        in_specs=[pl.BlockSpec((8, D_MODEL), lambda l, j: (0, 0)),
                  pl.BlockSpec((1, D_MODEL, tn), lambda l, j: (l, 0, j)),
                  pl.BlockSpec((1, 1, tn), lambda l, j: (l, 0, j))],
        out_specs=pl.BlockSpec((1, 8, tn), lambda l, j: (l, 0, j)),
        out_shape=jax.ShapeDtypeStruct((DEPTH, 8, 6 * D_MODEL), F32),
        compiler_params=_cparams("arbitrary", "arbitrary"),
        name="ada_modulation",
    )(cond8, w_mod, b_mod.reshape(DEPTH, 1, 6 * D_MODEL))


def _mod_spec(piece, row_fn):
    return pl.BlockSpec((None, 1, D_MODEL), lambda *ids: (row_fn(*ids), 0, piece))


def _ln_kernel(x_ref, g_ref, b_ref, o_ref):
    o_ref[...] = _layer_norm(x_ref[...], g_ref[...], b_ref[...])


def _ln_in(x2d, g, b):
    nt = x2d.shape[0]
    return pl.pallas_call(
        _ln_kernel,
        grid=(nt // TM_IN,),
        in_specs=[pl.BlockSpec((TM_IN, D_MODEL), lambda i: (i, 0)),
                  pl.BlockSpec((1, D_MODEL), lambda i: (0, 0)),
                  pl.BlockSpec((1, D_MODEL), lambda i: (0, 0))],
        out_specs=pl.BlockSpec((TM_IN, D_MODEL), lambda i: (i, 0)),
        out_shape=jax.ShapeDtypeStruct((nt, D_MODEL), F32),
        compiler_params=_cparams("arbitrary"),
        name="ln_in",
    )(x2d, g.reshape(1, D_MODEL), b.reshape(1, D_MODEL))


def _inproj_kernel(x_ref, sh_ref, sc_ref, w_ref, o_ref):
    h = x_ref[...] * (1.0 + sc_ref[...]) + sh_ref[...]
    o_ref[...] = jnp.dot(h.astype(BF16), w_ref[...], preferred_element_type=F32)


def _inproj(x2d, mod3, w_in_all, layer, seq_len, lat):
    nt = x2d.shape[0]
    tiles_per_seq = max(seq_len // TM_IN, 1)
    row = (lambda j, i: 1 + i // tiles_per_seq) if lat else (lambda j, i: 0)
    return pl.pallas_call(
        _inproj_kernel,
        grid=(N_P // TN_IN, nt // TM_IN),
        in_specs=[pl.BlockSpec((TM_IN, D_MODEL), lambda j, i: (i, 0)),
                  _mod_spec(0, row), _mod_spec(1, row),
                  pl.BlockSpec((None, D_MODEL, TN_IN), lambda j, i: (layer, 0, j))],
        out_specs=pl.BlockSpec((TM_IN, TN_IN), lambda j, i: (i, j)),
        out_shape=jax.ShapeDtypeStruct((nt, N_P), F32),
        compiler_params=_cparams("arbitrary", "arbitrary"),
        name="in_projection",
    )(x2d, mod3, mod3, w_in_all)


def _gla_tables(rev):
    n = GLA_BLK
    idx = np.arange(n)
    if rev:
        cum = (idx[None, :] >= idx[:, None]).astype(np.float32)
    else:
        cum = (idx[None, :] <= idx[:, None]).astype(np.float32)
    rows = [cum]
    for k in range(GLA_MM_LEVELS):
        h = 1 << k
        rows.append(cum[(idx & ~(2 * h - 1)) + (h if rev else h - 1)])
    mall = np.concatenate(rows, axis=0)
    t, s = idx[:, None], idx[None, :]
    x = t ^ s
    top = np.where(x > 0, np.floor(np.log2(np.maximum(x, 1))).astype(np.int64) + 1, 0)
    attend = (s >= t) if rev else (s <= t)
    lv = np.where(attend, top, -1).astype(np.int32)
    return jnp.asarray(mall, BF16), jnp.asarray(lv)


def _gla_kernel(rev, q_ref, k_ref, v_ref, ga_ref, wa_ref, ba_ref, mall_ref, lv_ref, s0_ref,
                o_ref, sout_ref, st_ref, t_ref):
    j = pl.program_id(1)
    n = GLA_BLK

    @pl.when(j == 0)
    def _():
        for h in range(GLA_HEADS):
            st_ref[h] = s0_ref[0, h].T

    z = jnp.dot(ga_ref[...], wa_ref[...], precision=HIGHEST, preferred_element_type=F32) + ba_ref[...]
    a = (jnp.minimum(z, 0.0) - jnp.log(1.0 + jnp.exp(-jnp.abs(z)))) * (1.0 / GLA_TAU)
    a_hi = a.astype(BF16)
    r1 = a - a_hi.astype(F32)
    a_mid = r1.astype(BF16)
    a_lo = (r1 - a_mid.astype(F32)).astype(BF16)
    mall = mall_ref[...]
    t_ref[...] = (jnp.dot(mall, a_hi, preferred_element_type=F32)
                  + jnp.dot(mall, a_mid, preferred_element_type=F32)
                  + jnp.dot(mall, a_lo, preferred_element_type=F32))

    lv = lv_ref[...]
    for h in range(GLA_HEADS):
        kc = slice(GLA_DK * h, GLA_DK * (h + 1))
        vc = slice(GLA_DV * h, GLA_DV * (h + 1))
        b = t_ref[0:n, kc]
        btot = t_ref[0:1, kc] if rev else t_ref[n - 1:n, kc]
        q = q_ref[:, kc] * (GLA_DK ** -0.5)
        k = k_ref[:, kc]
        v = v_ref[:, vc]
        att = jnp.where(lv == 0, _nt_dot(q.astype(BF16), k.astype(BF16)), 0.0)
        for lev in range(GLA_LEVELS):
            if lev < GLA_MM_LEVELS:
                pv = t_ref[(lev + 1) * n:(lev + 2) * n, kc]
            else:
                hh = 1 << lev
                pv = jnp.concatenate(
                    [jnp.broadcast_to(t_ref[g0 + (hh if rev else hh - 1):g0 + (hh if rev else hh - 1) + 1, kc],
                                      (2 * hh, GLA_DK)) for g0 in range(0, n, 2 * hh)], axis=0)
            ql = (q * jnp.exp(jnp.minimum(b - pv, 0.0))).astype(BF16)
            kl = (k * jnp.exp(jnp.minimum(pv - b, 0.0))).astype(BF16)
            att = jnp.where(lv == lev + 1, _nt_dot(ql, kl), att)
        st = st_ref[h]
        qi = (q * jnp.exp(b)).astype(BF16)
        o = (jnp.dot(att.astype(BF16), v.astype(BF16), preferred_element_type=F32)
             + _nt_dot(qi, st.astype(BF16)))
        o_ref[:, vc] = o
        kh = (k * jnp.exp(btot - b)).astype(BF16)
        st_ref[h] = st * jnp.exp(btot) + jnp.dot(v.T.astype(BF16), kh, preferred_element_type=F32)

    @pl.when(j == pl.num_programs(1) - 1)
    def _():
        for h in range(GLA_HEADS):
            sout_ref[0, h] = st_ref[h].T


def _gla(p2d, s0, wa_pad, ba, nseq, seq_len, rev):
    nt = p2d.shape[0]
    nblk = seq_len // GLA_BLK
    mall, lv = _gla_tables(rev)

    def rb(b, j):
        return b * nblk + (nblk - 1 - j if rev else j)

    return pl.pallas_call(
        functools.partial(_gla_kernel, rev),
        grid=(nseq, nblk),
        in_specs=[pl.BlockSpec((GLA_BLK, 512), lambda b, j: (rb(b, j), P_GQ // 512)),
                  pl.BlockSpec((GLA_BLK, 512), lambda b, j: (rb(b, j), P_GK // 512)),
                  pl.BlockSpec((GLA_BLK, 1024), lambda b, j: (rb(b, j), P_GV // 1024)),
                  pl.BlockSpec((GLA_BLK, LANE), lambda b, j: (rb(b, j), P_GA // LANE)),
                  pl.BlockSpec((LANE, 512), lambda b, j: (0, 0)),
                  pl.BlockSpec((1, 512), lambda b, j: (0, 0)),
                  pl.BlockSpec(mall.shape, lambda b, j: (0, 0)),
                  pl.BlockSpec(lv.shape, lambda b, j: (0, 0)),
                  pl.BlockSpec((1, GLA_HEADS, GLA_DK, GLA_DV), lambda b, j: (b, 0, 0, 0))],
        out_specs=[pl.BlockSpec((GLA_BLK, 1024), lambda b, j: (rb(b, j), 0)),
                   pl.BlockSpec((1, GLA_HEADS, GLA_DK, GLA_DV), lambda b, j: (b, 0, 0, 0))],
        out_shape=[jax.ShapeDtypeStruct((nt, 1024), F32),
                   jax.ShapeDtypeStruct((nseq, GLA_HEADS, GLA_DK, GLA_DV), F32)],
        scratch_shapes=[pltpu.VMEM((GLA_HEADS, GLA_DV, GLA_DK), F32),
                        pltpu.VMEM(((GLA_MM_LEVELS + 1) * GLA_BLK, 512), F32)],
        compiler_params=_cparams("arbitrary", "arbitrary"),
        name="gla_bwd" if rev else "gla_fwd",
    )(p2d, p2d, p2d, p2d, wa_pad, ba, mall, lv, s0)


def _softmax_sink_heads(q_rows, k_all, v_all, bias, sink_col):
    s = _nt_dot(q_rows.astype(BF16), k_all.astype(BF16))
    if bias is not None:
        s = s + bias
    m = jnp.maximum(jnp.max(s, axis=-1, keepdims=True), sink_col)
    p = jnp.exp(s - m)
    den = jnp.sum(p, axis=-1, keepdims=True) + jnp.exp(sink_col - m)
    o = jnp.dot(p.astype(BF16), v_all.astype(BF16), preferred_element_type=F32)
    return o / den


def _ctx_attn_kernel(sink_ref, q_ref, k_ref, v_ref, o_ref):
    t = q_ref.shape[0]
    rows = lax.broadcasted_iota(jnp.int32, (ATT_GROUP * t, 1), 0)
    for g in range(ATT_KV_HEADS):
        kg = k_ref[:, ATT_HD * g:ATT_HD * (g + 1)]
        vg = v_ref[:, ATT_HD * g:ATT_HD * (g + 1)]
        qs, sink_col = [], jnp.zeros((ATT_GROUP * t, 1), F32)
        for i in range(ATT_GROUP):
            hh = g * ATT_GROUP + i
            qs.append(q_ref[:, ATT_HD * hh:ATT_HD * (hh + 1)] * (ATT_HD ** -0.5))
            sink_col = jnp.where(rows // t == i, sink_ref[hh], sink_col)
        o = _softmax_sink_heads(jnp.concatenate(qs, axis=0), kg, vg, None, sink_col)
        for i in range(ATT_GROUP):
            hh = g * ATT_GROUP + i
            o_ref[:, ATT_HD * hh:ATT_HD * (hh + 1)] = o[i * t:(i + 1) * t]


def _ctx_attention(p2d, sink, nseq, seq_len):
    nt = p2d.shape[0]
    return pl.pallas_call(
        _ctx_attn_kernel,
        grid=(nseq,),
        in_specs=[pl.BlockSpec(memory_space=pltpu.SMEM),
                  pl.BlockSpec((seq_len, 1024), lambda b: (b, P_AQ // 1024)),
                  pl.BlockSpec((seq_len, 256), lambda b: (b, P_AK // 256)),
                  pl.BlockSpec((seq_len, 256), lambda b: (b, P_AV // 256))],
        out_specs=pl.BlockSpec((seq_len, 1024), lambda b: (b, 0)),
        out_shape=jax.ShapeDtypeStruct((nt, 1024), F32),
        compiler_params=_cparams("arbitrary"),
        name="context_attention",
    )(sink, p2d, p2d, p2d)


def _rope(x, cos_t, sin_t):
    even = lax.broadcasted_iota(jnp.int32, x.shape, 1) % 2 == 0
    swapped = jnp.where(even, pltpu.roll(x, LANE - 1, 1), pltpu.roll(x, 1, 1))
    return x * cos_t + swapped * sin_t


def _lat_attn_kernel(sink_ref, q_ref, kp_ref, kc_ref, kn_ref, vp_ref, vc_ref, vn_ref,
                     ck_ref, cv_ref, cq_ref, sq_ref, cp_ref, sp_ref, cn_ref, sn_ref, o_ref):
    n = pl.program_id(1)
    nb = pl.num_programs(1)
    t = ATT_BLOCK
    kblocks = []
    for kref, c_ref, s_ref in ((kp_ref, cp_ref, sp_ref), (kc_ref, cq_ref, sq_ref), (kn_ref, cn_ref, sn_ref)):
        kblocks.append(jnp.concatenate(
            [_rope(kref[:, LANE * c:LANE * (c + 1)], c_ref[...], s_ref[...]) for c in range(2)], axis=1))
    k_loc = jnp.concatenate(kblocks, axis=0)
    v_loc = jnp.concatenate([vp_ref[...], vc_ref[...], vn_ref[...]], axis=0)
    k_all = jnp.concatenate([k_loc, ck_ref[...]], axis=0)
    v_all = jnp.concatenate([v_loc, cv_ref[...]], axis=0)
    tc = ck_ref.shape[0]
    qi = lax.broadcasted_iota(jnp.int32, (t, 3 * t + tc), 0)
    kj = lax.broadcasted_iota(jnp.int32, (t, 3 * t + tc), 1)
    ninf = jnp.float32(-jnp.inf)
    prev_bias = jnp.where(n > 0, jnp.float32(0.0), ninf)
    next_bias = jnp.where(n < nb - 1, jnp.float32(0.0), ninf)
    bias1 = jnp.where(kj < t, jnp.where(kj >= qi, prev_bias, ninf),
                      jnp.where(kj < 2 * t, 0.0,
                                jnp.where(kj < 3 * t, jnp.where(kj - 2 * t <= qi, next_bias, ninf), 0.0)))
    bias = jnp.concatenate([bias1] * ATT_GROUP, axis=0)
    rows = lax.broadcasted_iota(jnp.int32, (ATT_GROUP * t, 1), 0)
    qr = [_rope(q_ref[:, LANE * c:LANE * (c + 1)], cq_ref[...], sq_ref[...]) * (ATT_HD ** -0.5)
          for c in range(ATT_HEADS * ATT_HD // LANE)]
    for g in range(ATT_KV_HEADS):
        qs, sink_col = [], jnp.zeros((ATT_GROUP * t, 1), F32)
        for i in range(ATT_GROUP):
            hh = g * ATT_GROUP + i
            slab = qr[hh // 2]
            qs.append(slab[:, ATT_HD * (hh % 2):ATT_HD * (hh % 2 + 1)])
            sink_col = jnp.where(rows // t == i, sink_ref[hh], sink_col)
        kg = k_all[:, ATT_HD * g:ATT_HD * (g + 1)]
        vg = v_all[:, ATT_HD * g:ATT_HD * (g + 1)]
        o = _softmax_sink_heads(jnp.concatenate(qs, axis=0), kg, vg, bias, sink_col)
        for i in range(ATT_GROUP):
            hh = g * ATT_GROUP + i
            o_ref[:, ATT_HD * hh:ATT_HD * (hh + 1)] = o[i * t:(i + 1) * t]


def _lat_attention(p2d, sink, cache_k, cache_v, cos_t, sin_t, nseq, seq_len):
    nt = p2d.shape[0]
    nb = seq_len // ATT_BLOCK
    tc = cache_k.shape[1]
    cur = lambda b, n: b * nb + n
    prv = lambda b, n: b * nb + jnp.maximum(n - 1, 0)
    nxt = lambda b, n: b * nb + jnp.minimum(n + 1, nb - 1)
    kv = lambda f, col: pl.BlockSpec((ATT_BLOCK, 256), lambda b, n: (f(b, n), col))
    tab = lambda f: pl.BlockSpec((ATT_BLOCK, LANE), lambda b, n: (f(0, n), 0))
    return pl.pallas_call(
        _lat_attn_kernel,
        grid=(nseq, nb),
        in_specs=[pl.BlockSpec(memory_space=pltpu.SMEM),
                  pl.BlockSpec((ATT_BLOCK, 1024), lambda b, n: (cur(b, n), P_AQ // 1024)),
                  kv(prv, P_AK // 256), kv(cur, P_AK // 256), kv(nxt, P_AK // 256),
                  kv(prv, P_AV // 256), kv(cur, P_AV // 256), kv(nxt, P_AV // 256),
                  pl.BlockSpec((None, tc, 256), lambda b, n: (b, 0, 0)),
                  pl.BlockSpec((None, tc, 256), lambda b, n: (b, 0, 0)),
                  tab(cur), tab(cur), tab(prv), tab(prv), tab(nxt), tab(nxt)],
        out_specs=pl.BlockSpec((ATT_BLOCK, 1024), lambda b, n: (cur(b, n), 0)),
        out_shape=jax.ShapeDtypeStruct((nt, 1024), F32),
        compiler_params=_cparams("arbitrary", "arbitrary"),
        name="latent_window_attention",
    )(sink, p2d, p2d, p2d, p2d, p2d, p2d, p2d, cache_k, cache_v,
      cos_t, sin_t, cos_t, sin_t, cos_t, sin_t)


def _rope_tables(seq_len):
    rows = seq_len // GRID_W
    row = jnp.repeat(jnp.arange(rows, dtype=F32), GRID_W)
    col = jnp.tile(jnp.arange(GRID_W, dtype=F32), rows)
    half = ATT_HD // 2
    inv = ROPE_THETA ** (-jnp.arange(0, half, 2, dtype=F32) / half)
    ang = jnp.concatenate([row[:, None] * inv, col[:, None] * inv], -1)
    cos, sin = jnp.cos(ang), jnp.sin(ang)
    cos_t = jnp.tile(jnp.repeat(cos, 2, axis=1), (1, LANE // ATT_HD))
    sign = jnp.tile(jnp.asarray([-1.0, 1.0], F32), ATT_HD // 2)
    sin_t = jnp.tile(jnp.repeat(sin, 2, axis=1) * sign, (1, LANE // ATT_HD))
    return cos_t, sin_t


def _merge_kernel(tiles_per_seq, of_ref, ob_ref, gg_ref, ch_ref, cb_ref, cc_ref,
                  chp_ref, ccp_ref, chn_ref, ccn_ref, yc_ref, mga_ref, mgb_ref, mgc_ref, x_ref, g1_ref,
                  gn_ref, cw_ref, wb_ref, wo_ref, lg_ref, lb_ref, o_ref):
    i = pl.program_id(0)
    tm = x_ref.shape[0]
    gn = gn_ref[...]
    ya = []
    for h in range(GLA_HEADS):
        vc = slice(GLA_DV * h, GLA_DV * (h + 1))
        o = of_ref[:, vc] + ob_ref[:, vc]
        o = o * lax.rsqrt(jnp.mean(o * o, axis=-1, keepdims=True) + LN_EPS) * gn
        gg = gg_ref[:, vc]
        ya.append(o * (gg * jax.nn.sigmoid(gg)))
    ya = jnp.concatenate(ya, axis=1)
    z = cc_ref[...] * ch_ref[...]
    has_prev = (i % tiles_per_seq) != 0
    has_next = (i % tiles_per_seq) != tiles_per_seq - 1
    z_before = jnp.where(has_prev, ccp_ref[SUBLANE - 1:SUBLANE, :] * chp_ref[SUBLANE - 1:SUBLANE, :], 0.0)
    z_after = jnp.where(has_next, ccn_ref[0:1, :] * chn_ref[0:1, :], 0.0)
    r = lax.broadcasted_iota(jnp.int32, z.shape, 0)
    zp = jnp.where(r == 0, z_before, pltpu.roll(z, 1, 0))
    zn = jnp.where(r == tm - 1, z_after, pltpu.roll(z, tm - 1, 0))
    yb = cb_ref[...] * (cw_ref[0:1, :] * zp + cw_ref[1:2, :] * z + cw_ref[2:3, :] * zn)
    merged = jnp.zeros((tm, D_MODEL), F32)
    for bi, (y, mg_ref) in enumerate(((ya, mga_ref), (yb, mgb_ref), (yc_ref[...], mgc_ref))):
        proj = jnp.dot(y.astype(BF16), wb_ref[bi], preferred_element_type=F32)
        merged = merged + jax.nn.sigmoid(mg_ref[...]) * proj
    mix = jnp.dot(merged.astype(BF16), wo_ref[...], preferred_element_type=F32)
    o_ref[...] = _layer_norm(DN_ALPHA * x_ref[...] + g1_ref[...] * mix, lg_ref[...], lb_ref[...])


def _merge(x2d, p2d, o_f, o_b, y_c, mod3, gn, conv_w, wb_all, wo_all, layer, ln_g, ln_b, seq_len, lat):
    nt = x2d.shape[0]
    tps = seq_len // TM
    row = (lambda i: 1 + i // tps) if lat else (lambda i: 0)
    col = lambda c: pl.BlockSpec((TM, 1024), lambda i: (i, c // 1024))
    halo_rows = TM // SUBLANE
    nhalo = nt // SUBLANE
    prev = lambda c: pl.BlockSpec((SUBLANE, 1024), lambda i: (jnp.maximum(i * halo_rows - 1, 0), c // 1024))
    nxt = lambda c: pl.BlockSpec((SUBLANE, 1024),
                                 lambda i: (jnp.minimum((i + 1) * halo_rows, nhalo - 1), c // 1024))
    full = lambda shape: pl.BlockSpec(shape, lambda i: (0,) * len(shape))
    return pl.pallas_call(
        functools.partial(_merge_kernel, tps),
        grid=(nt // TM,),
        in_specs=[pl.BlockSpec((TM, 1024), lambda i: (i, 0)), pl.BlockSpec((TM, 1024), lambda i: (i, 0)),
                  col(P_GG), col(P_CH), col(P_CB), col(P_CC),
                  prev(P_CH), prev(P_CC), nxt(P_CH), nxt(P_CC),
                  pl.BlockSpec((TM, 1024), lambda i: (i, 0)),
                  col(P_MG), col(P_MG + 1024), col(P_MG + 2048),
                  pl.BlockSpec((TM, 1024), lambda i: (i, 0)),
                  _mod_spec(2, row),
                  full((1, GLA_DV)), full((SUBLANE, 1024)),
                  pl.BlockSpec((None, 3, 1024, 1024), lambda i: (layer, 0, 0, 0)),
                  pl.BlockSpec((None, 1024, 1024), lambda i: (layer, 0, 0)),
                  full((1, 1024)), full((1, 1024))],
        out_specs=pl.BlockSpec((TM, 1024), lambda i: (i, 0)),
        out_shape=jax.ShapeDtypeStruct((nt, 1024), F32),
        compiler_params=_cparams("arbitrary"),
        name="merge_ln1",
    )(o_f, o_b, p2d, p2d, p2d, p2d, p2d, p2d, p2d, p2d, y_c, p2d, p2d, p2d, x2d, mod3,
      gn, conv_w, wb_all, wo_all, ln_g, ln_b)


def _sorting_network(n):
    pairs = []
    p = 1
    while p < n:
        k = p
        while k >= 1:
            for j in range(k % p, n - k, 2 * k):
                for i in range(min(k, n - j - k)):
                    if (i + j) // (2 * p) == (i + j + k) // (2 * p):
                        pairs.append((i + j, i + j + k))
            k //= 2
        p *= 2
    return pairs


def _topk_rows_steps(s, kk, out):
    n, m = s.shape
    nslab = n // SUBLANE
    assert nslab == kk
    base = lax.broadcasted_iota(jnp.int32, (SUBLANE, m), 0).astype(F32)
    val = [s[SUBLANE * r:SUBLANE * (r + 1)] for r in range(nslab)]
    idx = [base + float(SUBLANE * r) for r in range(nslab)]
    for ci, (a, b) in enumerate(_sorting_network(nslab)):
        keep = val[a] >= val[b]
        val[a], val[b] = jnp.where(keep, val[a], val[b]), jnp.where(keep, val[b], val[a])
        idx[a], idx[b] = jnp.where(keep, idx[a], idx[b]), jnp.where(keep, idx[b], idx[a])
        if ci % 4 == 3:
            yield
    vals, idxs = [], []
    for r in range(kk):
        top = jnp.max(val[0], axis=0, keepdims=True)
        at = jnp.min(jnp.where(val[0] == top, idx[0], float(n)), axis=0, keepdims=True)
        vals.append(top)
        idxs.append(at)
        pop = idx[0] == at
        depth = kk - 1 - r
        for d in range(depth):
            val[d] = jnp.where(pop, val[d + 1], val[d])
            idx[d] = jnp.where(pop, idx[d + 1], idx[d])
        yield
    out.append((jnp.concatenate(vals, axis=0), jnp.concatenate(idxs, axis=0)))


def _staircase():
    return [(i, PEER_TOPK // (i + 1)) for i in range(PEER_TOPK)]


def _peer_topk_kernel(x_ref, sh_ref, sc_ref, w_ref, keys_ref, pos_ref, eidx_ref, gate_ref):
    h2 = (x_ref[...] * (1.0 + sc_ref[...]) + sh_ref[...]).astype(BF16)
    for hh in range(PEER_TOPK_HEADS):
        def store(e, g, hh=hh):
            eidx_ref[hh], gate_ref[hh] = e, g

        pq_fn = lambda hh=hh: jnp.dot(h2, w_ref[:, PEER_DQ * hh:PEER_DQ * (hh + 1)], preferred_element_type=F32)
        for _ in _peer_topk_head_steps(pq_fn, keys_ref[hh], pos_ref[...], store):
            pass


def _peer_topk_head_steps(pq_fn, keys, pos, store):
    half = PEER_DQ // 2
    pq = pq_fn()
    yield
    tops = []
    for p in range(2):
        s = _nt_dot(keys[p], pq[:, half * p:half * (p + 1)].astype(BF16))
        yield
        yield from _topk_rows_steps(s, PEER_TOPK, tops)
    (s1, i1), (s2, i2) = tops
    tm = s1.shape[1]
    kk = PEER_TOPK
    npad = pos.shape[0] - sum(nj for _, nj in _staircase())
    cand = jnp.concatenate([s1[i:i + 1] + s2[0:nj] for i, nj in _staircase()]
                           + [jnp.full((npad, tm), -jnp.inf, F32)], axis=0)
    cidx = jnp.concatenate([i1[i:i + 1] * float(PEER_NKEYS) + i2[0:nj] for i, nj in _staircase()]
                           + [jnp.zeros((npad, tm), F32)], axis=0)
    yield
    top_s, top_e = [], []
    for _ in range(kk):
        m = jnp.max(cand, axis=0, keepdims=True)
        at = jnp.min(jnp.where(cand == m, pos, float(2 * kk * kk)), axis=0, keepdims=True)
        hit = pos == at
        top_s.append(m)
        top_e.append(jnp.sum(jnp.where(hit, cidx, 0.0), axis=0, keepdims=True))
        cand = jnp.where(hit, -jnp.inf, cand)
        yield
    top_s = jnp.concatenate(top_s, axis=0)
    e = jnp.exp(top_s - top_s[0:1])
    store(jnp.concatenate(top_e, axis=0).astype(jnp.int32), e / jnp.sum(e, axis=0, keepdims=True))


TOPK_HEAD_STEPS = 1 + 2 * (1 + 16 + PEER_TOPK) + 1 + PEER_TOPK


def _topk_pos(tm):
    flat = [i * PEER_TOPK + j for i, nj in _staircase() for j in range(nj)]
    nrows = -(-len(flat) // SUBLANE) * SUBLANE
    flat = flat + [PEER_TOPK * PEER_TOPK] * (nrows - len(flat))
    return jnp.asarray(np.repeat(np.asarray(flat, np.float32)[:, None], tm, axis=1))


def _token_major(t, nt):
    return t.reshape(PEER_HEADS * PEER_TOPK, nt).T


def _peer_topk(x2d, mod3, w_pq_all, keys_all, layer, seq_len, lat):
    nt = x2d.shape[0]
    tps = seq_len // TM
    row = (lambda i, h: 1 + i // tps) if lat else (lambda i, h: 0)
    pos = _topk_pos(TM)
    nrows = pos.shape[0]
    return pl.pallas_call(
        _peer_topk_kernel,
        grid=(nt // TM, PEER_HEADS // PEER_TOPK_HEADS),
        in_specs=[pl.BlockSpec((TM, 1024), lambda i, h: (i, 0)),
                  _mod_spec(3, row), _mod_spec(4, row),
                  pl.BlockSpec((None, 1024, PEER_TOPK_HEADS * PEER_DQ), lambda i, h: (layer, 0, h)),
                  pl.BlockSpec((None, PEER_TOPK_HEADS, 2, PEER_NKEYS, PEER_DQ // 2),
                               lambda i, h: (layer, h, 0, 0, 0)),
                  pl.BlockSpec((nrows, TM), lambda i, h: (0, 0))],
        out_specs=[pl.BlockSpec((PEER_TOPK_HEADS, PEER_TOPK, TM), lambda i, h: (h, 0, i)),
                   pl.BlockSpec((PEER_TOPK_HEADS, PEER_TOPK, TM), lambda i, h: (h, 0, i))],
        out_shape=[jax.ShapeDtypeStruct((PEER_HEADS, PEER_TOPK, nt), jnp.int32),
                   jax.ShapeDtypeStruct((PEER_HEADS, PEER_TOPK, nt), F32)],
        compiler_params=_cparams("arbitrary", "arbitrary"),
        name="peer_topk",
    )(x2d, mod3, mod3, w_pq_all, keys_all, pos)


def _peer_expert_kernel(*refs, side):
    npair = PEER_HEADS * PEER_TOPK
    ngroup = PEER_TT // PEER_GROUP
    nrow = npair * SUBLANE
    (eidx_ref, x_ref, sh_ref, sc_ref, g2_ref, gate_ref, lg_ref, lb_ref, col_ref, exp_ref, uv_hbm), refs = refs[:11], refs[11:]
    if side is not None:
        (xb_ref, shb_ref, scb_ref, wpq_ref, keys_ref, pos_ref), refs = refs[:6], refs[6:]
        (o_ref, eidxb_ref, gateb_ref), refs = refs[:3], refs[3:]
        h2b_ref, refs = refs[-1], refs[:-1]
        h2b_ref[...] = (xb_ref[...] * (1.0 + scb_ref[...]) + shb_ref[...]).astype(BF16)
    else:
        o_ref, refs = refs[0], refs[1:]
    blocks, (sems, ffn_ref) = refs[:PEER_RING], refs[PEER_RING:]

    def ring(slot):
        return blocks[slot // PEER_GROUP].at[slot % PEER_GROUP]

    def slot_copy(slot):
        return pltpu.make_async_copy(uv_hbm.at[pl.ds(0, npair)], ring(slot), sems.at[slot])

    def issue_token(tok, slot):
        for r in range(npair):
            pltpu.make_async_copy(uv_hbm.at[eidx_ref[tok, r]], ring(slot).at[r],
                                  sems.at[slot]).start(priority=r % 2)

    @pl.when(pl.program_id(0) == 0)
    def _():
        for t in range(PEER_AHEAD * PEER_GROUP):
            issue_token(t, t)

    rowid = lax.broadcasted_iota(jnp.int32, (PEER_GROUP, 1), 0)
    own = (lax.broadcasted_iota(jnp.int32, (SUBLANE, nrow), 1) % SUBLANE
           == lax.broadcasted_iota(jnp.int32, (SUBLANE, nrow), 0))
    sc, sh = sc_ref[...], sh_ref[...]

    def side_steps(k):
        per_round = side
        for hh in range(per_round):
            h = k * per_round + hh

            def store(e, g, h=h):
                eidxb_ref[h], gateb_ref[h] = e, g

            pq_fn = lambda h=h: jnp.dot(h2b_ref[...], wpq_ref[h], preferred_element_type=F32)
            yield from _peer_topk_head_steps(pq_fn, keys_ref[h], pos_ref[...], store)

    def ring_round(k, carry):
        steps = side_steps(k) if side is not None else iter(())
        per_issue = 0 if side is None else -(-TOPK_HEAD_STEPS * side // (PEER_RING * PEER_GROUP))
        for q in range(PEER_RING):
            g = k * PEER_RING + q
            base = q * PEER_GROUP
            ahead = ((q + PEER_AHEAD) % PEER_RING) * PEER_GROUP
            r0 = pl.multiple_of(g * PEER_GROUP, PEER_GROUP)
            for jj in range(PEER_GROUP):
                slot_copy(base + jj).wait()
            z8 = jnp.zeros((PEER_GROUP, nrow), F32)
            for jj in range(PEER_GROUP):
                issue_token((g + PEER_AHEAD) * PEER_GROUP + jj, ahead + jj)
                for _ in range(per_issue):
                    next(steps, None)
                h = (x_ref[r0 + jj] * (1.0 + sc) + sh).astype(BF16)
                u = ring(base + jj)[:, 0].reshape(nrow, LANE).astype(BF16)
                y = jnp.where(own, _nt_dot(h, u), 0.0)
                z8 = jnp.where(rowid == jj, jnp.sum(y, axis=0, keepdims=True), z8)
            z_hi = z8.astype(BF16)
            z_lo = (z8 - z_hi.astype(F32)).astype(BF16)
            s8 = (jnp.dot(z_hi, col_ref[...], preferred_element_type=F32)
                  + jnp.dot(z_lo, col_ref[...], preferred_element_type=F32))
            act = 0.5 * s8 * (1.0 + lax.erf(s8 * (2.0 ** -0.5)))
            w8 = (gate_ref[pl.ds(r0, PEER_GROUP), :] * act).astype(BF16)
            wexp = jnp.dot(w8, exp_ref[...], preferred_element_type=F32)
            for jj in range(PEER_GROUP):
                wsel = jnp.where(own, wexp[jj:jj + 1, :], 0.0).astype(BF16)
                v = ring(base + jj)[:, 1].reshape(nrow, LANE).astype(BF16)
                ffn_ref[r0 + jj] = jnp.dot(wsel, v, preferred_element_type=F32)
        for _ in steps:
            pass
        return carry

    lax.fori_loop(0, ngroup // PEER_RING, ring_round, 0)

    @pl.when(pl.program_id(0) == pl.num_programs(0) - 1)
    def _():
        for t in range(PEER_AHEAD * PEER_GROUP):
            slot_copy(t).wait()

    y = DN_ALPHA * x_ref[...] + g2_ref[...] * ffn_ref[...]
    mu = jnp.sum(jnp.sum(y, axis=2, keepdims=True), axis=1, keepdims=True) * (1.0 / D_MODEL)
    yc = y - mu
    var = jnp.sum(jnp.sum(yc * yc, axis=2, keepdims=True), axis=1, keepdims=True) * (1.0 / D_MODEL)
    o_ref[...] = yc * lax.rsqrt(var + LN_EPS) * lg_ref[...] + lb_ref[...]


def _peer_experts(x2d, eidx, gate, mod, uv, layer, ln_g, ln_b, seq_len, lat, side=None):
    nt = x2d.shape[0]
    tps = seq_len // PEER_TT
    row = (lambda i: 1 + i // tps) if lat else (lambda i: 0)
    npair = PEER_HEADS * PEER_TOPK
    nrow = npair * SUBLANE
    slab = (SUBLANE, LANE)
    mod4 = mod.reshape(8, 6, *slab)
    mod_spec = lambda piece: pl.BlockSpec((None, None) + slab, lambda i: (row(i), piece, 0, 0))
    full = lambda shape: pl.BlockSpec(shape, lambda i: (0,) * len(shape))
    collapse = jnp.asarray(np.arange(nrow)[:, None] // SUBLANE == np.arange(npair)[None, :], BF16)
    ntile = nt // PEER_TT
    nahead = PEER_AHEAD * PEER_GROUP
    tiles = eidx.reshape(ntile, PEER_TT, npair) + layer * PEER_EXPERTS
    nxt = jnp.concatenate([tiles[1:, :nahead], tiles[-1:, :nahead]], axis=0)
    eidx_ext = jnp.concatenate([tiles, nxt], axis=1)
    assert (PEER_TT // PEER_GROUP) % PEER_RING == 0 and PEER_AHEAD < PEER_RING
    in_specs = [pl.BlockSpec((None, PEER_TT + nahead, npair), lambda i: (i, 0, 0), memory_space=pltpu.SMEM),
                pl.BlockSpec((PEER_TT,) + slab, lambda i: (i, 0, 0)),
                mod_spec(3), mod_spec(4), mod_spec(5),
                pl.BlockSpec((PEER_TT, npair), lambda i: (i, 0)),
                full(slab), full(slab), full((nrow, npair)), full((npair, nrow)),
                pl.BlockSpec(memory_space=pl.ANY)]
    args = [eidx_ext, x2d.reshape((nt,) + slab), mod4, mod4, mod4, gate, ln_g.reshape(slab), ln_b.reshape(slab),
            collapse, collapse.T, uv]
    out_specs = [pl.BlockSpec((PEER_TT,) + slab, lambda i: (i, 0, 0))]
    out_shape = [jax.ShapeDtypeStruct((nt,) + slab, F32)]
    scratch = ([pltpu.VMEM((PEER_GROUP, npair, 2) + slab, F32)] * PEER_RING
               + [pltpu.SemaphoreType.DMA((PEER_SLOTS,)), pltpu.VMEM((PEER_TT,) + slab, F32)])
    side_cfg = None
    if side is not None:
        xb, mod3_b, w_heads_all, keys_all, layer_b, seq_b, lat_b = side
        ntb = xb.shape[0]
        nround = PEER_TT // PEER_GROUP // PEER_RING
        tmb = next(t for t in (TM, TM // 2) if (PEER_HEADS * ntb) % (t * ntile * nround) == 0)
        heads_per_step = PEER_HEADS * (ntb // tmb) // ntile
        steps_per_tile = PEER_HEADS // heads_per_step
        side_cfg = heads_per_step // nround
        assert heads_per_step * steps_per_tile == PEER_HEADS and side_cfg * nround == heads_per_step
        ntile_b = ntb // tmb
        tile_b = lambda i: i % ntile_b
        part_b = lambda i: i // ntile_b
        tps_b = seq_b // tmb
        row_b = (lambda i: 1 + tile_b(i) // tps_b) if lat_b else (lambda i: 0)
        pos = _topk_pos(tmb)
        in_specs += [pl.BlockSpec((tmb, D_MODEL), lambda i: (tile_b(i), 0)),
                     _mod_spec(3, row_b), _mod_spec(4, row_b),
                     pl.BlockSpec((None, heads_per_step, D_MODEL, PEER_DQ), lambda i: (layer_b, part_b(i), 0, 0)),
                     pl.BlockSpec((None, heads_per_step, 2, PEER_NKEYS, PEER_DQ // 2),
                                  lambda i: (layer_b, part_b(i), 0, 0, 0)),
                     full(pos.shape)]
        args += [xb, mod3_b, mod3_b, w_heads_all, keys_all, pos]
        out_specs += [pl.BlockSpec((heads_per_step, PEER_TOPK, tmb), lambda i: (part_b(i), 0, tile_b(i)))] * 2
        out_shape += [jax.ShapeDtypeStruct((PEER_HEADS, PEER_TOPK, ntb), jnp.int32),
                      jax.ShapeDtypeStruct((PEER_HEADS, PEER_TOPK, ntb), F32)]
        scratch += [pltpu.VMEM((tmb, D_MODEL), BF16)]
    outs = pl.pallas_call(
        functools.partial(_peer_expert_kernel, side=side_cfg),
        grid=(ntile,),
        in_specs=in_specs,
        out_specs=out_specs,
        out_shape=out_shape,
        scratch_shapes=scratch,
        compiler_params=_cparams("arbitrary"),
        name="peer_experts_ln2" if side is None else "peer_experts_ln2_topk",
    )(*args)
    x2 = outs[0].reshape(nt, D_MODEL)
    if side is None:
        return x2, None
    return x2, (_token_major(outs[1], ntb), _token_major(outs[2], ntb))


def _mixer_half(x2d, mod3, lw, nseq, seq_len, lat, s0_f, s0_b, ctx):
    p2d = _inproj(x2d, mod3, lw["w_in"], lw["layer"], seq_len, lat)
    o_f, s_f = _gla(p2d, s0_f, lw["wa_f"], lw["ba_f"], nseq, seq_len, rev=False)
    o_b, s_b = _gla(p2d, s0_b, lw["wa_b"], lw["ba_b"], nseq, seq_len, rev=True)
    if lat:
        y_c = _lat_attention(p2d, lw["sink"], ctx["k"], ctx["v"], ctx["cos"], ctx["sin"], nseq, seq_len)
    else:
        y_c = _ctx_attention(p2d, lw["sink"], nseq, seq_len)
    x1 = _merge(x2d, p2d, o_f, o_b, y_c, mod3, lw["gn"], lw["conv_w"], lw["wb"], lw["wo"], lw["layer"],
                lw["ln1_g"], lw["ln1_b"], seq_len, lat)
    return x1, p2d, s_f, s_b


def _stacked_weights(w_in, w_branch, w_out, w_pq, peer_keys, peer_u, peer_v):
    w_in_p = jnp.concatenate(
        [w_in[:, :, O_GQ:O_GA], w_in[:, :, O_CH:O_AK], w_in[:, :, O_MG:N_IN], w_in[:, :, O_AK:O_MG],
         w_in[:, :, O_GA:O_CH], jnp.zeros((DEPTH, D_MODEL, N_P - N_IN), F32)], axis=2).astype(BF16)
    w_pq_b = w_pq.astype(BF16)
    slabs = (DEPTH, PEER_EXPERTS, SUBLANE, LANE)
    uv = jnp.stack([peer_u.reshape(slabs), peer_v.reshape(slabs)], axis=2)
    return {
        "w_in": w_in_p, "wb": w_branch.astype(BF16), "wo": w_out.astype(BF16), "w_pq": w_pq_b,
        "w_heads": w_pq_b.reshape(DEPTH, D_MODEL, PEER_HEADS, PEER_DQ).transpose(0, 2, 1, 3),
        "keys": peer_keys.astype(BF16),
        "uv": uv.reshape((DEPTH * PEER_EXPERTS, 2, SUBLANE, LANE)),
    }


def _layer_weights(l, sw, w_gla_a2, b_gla_a, gla_norm_g, conv_w, attn_sink, ln1_g, ln1_b, ln2_g, ln2_b):
    def wa_pad(d):
        return jnp.zeros((LANE, 512), F32).at[GLA_LR * d:GLA_LR * (d + 1)].set(w_gla_a2[l, d])

    return {
        "w_in": sw["w_in"],
        "wa_f": wa_pad(0), "wa_b": wa_pad(1),
        "ba_f": b_gla_a[l, 0].reshape(1, 512), "ba_b": b_gla_a[l, 1].reshape(1, 512),
        "gn": gla_norm_g[l].reshape(1, GLA_DV),
        "conv_w": jnp.zeros((SUBLANE, 1024), F32).at[0:CONV_K].set(conv_w[l]),
        "sink": attn_sink[l],
        "wb": sw["wb"], "wo": sw["wo"],
        "ln1_g": ln1_g[l].reshape(1, 1024), "ln1_b": ln1_b[l].reshape(1, 1024),
        "w_pq": sw["w_pq"], "w_heads": sw["w_heads"], "keys": sw["keys"],
        "uv": sw["uv"], "layer": l,
        "ln2_g": ln2_g[l].reshape(1, 1024), "ln2_b": ln2_b[l].reshape(1, 1024),
    }


def kernel(x_prompt, x_sample, cache_k, cache_v, state_gla, c, c_ctx, ln_in_g, ln_in_b, w_mod, b_mod, w_in, w_gla_a2, b_gla_a, gla_norm_g, conv_w, attn_sink, w_branch, w_out, ln1_g, ln1_b, w_pq, peer_keys, peer_u, peer_v, ln2_g, ln2_b):
    batch, seq, _ = x_prompt.shape
    dec_batch, dec_seq, _ = x_sample.shape
    past = cache_k.shape[2]
    assert dec_batch + 1 <= 8 and seq % TM == 0 and dec_seq % TM_IN == 0 and seq % GLA_BLK == 0

    cond8 = jnp.zeros((8, D_MODEL), F32).at[0].set(c_ctx).at[1:1 + dec_batch].set(c)
    mod = _modulation(cond8, w_mod, b_mod)
    cos_t, sin_t = _rope_tables(dec_seq)

    xp = _ln_in(x_prompt.reshape(batch * seq, D_MODEL), ln_in_g, ln_in_b)
    xs = _ln_in(x_sample.reshape(dec_batch * dec_seq, D_MODEL), ln_in_g, ln_in_b)
    zeros_state = jnp.zeros((batch, GLA_HEADS, GLA_DK, GLA_DV), F32)
    ks, vs, ss = [], [], []
    pending = None
    sw = _stacked_weights(w_in, w_branch, w_out, w_pq, peer_keys, peer_u, peer_v)
    for l in range(DEPTH):
        lw = _layer_weights(l, sw, w_gla_a2, b_gla_a, gla_norm_g, conv_w, attn_sink, ln1_g, ln1_b, ln2_g, ln2_b)
        mod3 = mod[l].reshape(8, 1, 6 * D_MODEL)
        x1c, p_ctx, s_f, s_b = _mixer_half(xp, mod3, lw, batch, seq, False, zeros_state, zeros_state, None)
        ks.append(p_ctx[:, P_AK:P_AK + 256].reshape(batch, seq, ATT_KV_HEADS, ATT_HD))
        vs.append(p_ctx[:, P_AV:P_AV + 256].reshape(batch, seq, ATT_KV_HEADS, ATT_HD))
        ss.append(jnp.stack([s_f, s_b], axis=1))
        if pending is None:
            e_t, g_t = _peer_topk(x1c, mod3, lw["w_pq"], lw["keys"], l, seq, False)
            ec, gc = _token_major(e_t, batch * seq), _token_major(g_t, batch * seq)
        else:
            xs, (ec, gc) = _peer_experts(*pending, side=(x1c, mod3, lw["w_heads"], lw["keys"], l, seq, False))
        ctx = {"k": cache_k[:, l].reshape(dec_batch, past, ATT_KV_HEADS * ATT_HD),
               "v": cache_v[:, l].reshape(dec_batch, past, ATT_KV_HEADS * ATT_HD),
               "cos": cos_t, "sin": sin_t}
        x1l, _, _, _ = _mixer_half(xs, mod3, lw, dec_batch, dec_seq, True,
                                   state_gla[:, l, 0], state_gla[:, l, 1], ctx)
        xp, (el, gl) = _peer_experts(x1c, ec, gc, mod3, lw["uv"], l, lw["ln2_g"], lw["ln2_b"], seq, False,
                                     side=(x1l, mod3, lw["w_heads"], lw["keys"], l, dec_seq, True))
        pending = (x1l, el, gl, mod3, lw["uv"], l, lw["ln2_g"], lw["ln2_b"], dec_seq, True)
    xs, _ = _peer_experts(*pending)
    return (xp.reshape(batch, seq, D_MODEL), xs.reshape(dec_batch, dec_seq, D_MODEL),
            jnp.stack(ks, axis=1), jnp.stack(vs, axis=1), jnp.stack(ss, axis=1))
```
